```python
import jax, jax.numpy as jnp
from jax import lax
import numpy as np

D_MODEL = 2048
BATCH = 4
SEQ = 4096
DEPTH = 2

GRID_W = 64
CTX_LEN = 256
N_HEADS = 16
HEAD_DIM = 64
ATTN_WIDTH = N_HEADS * HEAD_DIM
WIN_ROWS = 8
WIN_COLS = 16
COL_BLOCK = 16
KEY_SPAN = 32
ROPE_BASE = 10000.0
CONV_WIDTH = 512
CONV_K = 31
SC_WIDTH = 512
SC_K = 3
N_BRANCH = 3
N_EXPERTS = 16
EXPERT_FF = 1024
CAP_FACTOR = 2
EPS = 1e-6
NEG_INF = -1e30
PROJ_SPLITS = (ATTN_WIDTH, ATTN_WIDTH, ATTN_WIDTH, CONV_WIDTH, CONV_WIDTH, SC_WIDTH, SC_WIDTH, SC_WIDTH, N_BRANCH * D_MODEL)
PROJ_WIDTH = 3 * ATTN_WIDTH + 2 * CONV_WIDTH + 3 * SC_WIDTH + N_BRANCH * D_MODEL

kernel_name = "hybrid_natten_conformer_shortconv_ec_moe_dit"


def rmsnorm(x, g):
    x32 = x.astype(jnp.float32)
    y = x32 * lax.rsqrt(jnp.mean(x32 * x32, axis=-1, keepdims=True) + EPS)
    return (y * g.astype(jnp.float32)).astype(x.dtype)


def layernorm(x, g, b):
    x32 = x.astype(jnp.float32)
    mu = jnp.mean(x32, axis=-1, keepdims=True)
    xc = x32 - mu
    y = xc * lax.rsqrt(jnp.mean(xc * xc, axis=-1, keepdims=True) + EPS)
    return (y * g.astype(jnp.float32) + b.astype(jnp.float32)).astype(x.dtype)


def modulate(h, shift, scale):
    return h * (1 + scale) + shift


def heads(t):
    return t.reshape(*t.shape[:-1], N_HEADS, HEAD_DIM)


def split_proj(p):
    idx = np.cumsum(PROJ_SPLITS)[:-1].tolist()
    return jnp.split(p, idx, axis=-1)


def depthwise_conv(u, w):
    k = w.shape[0]
    kern = w.astype(u.dtype)[:, None, :]
    return lax.conv_general_dilated(u, kern, window_strides=(1,), padding=[(k // 2, k // 2)],
                                    dimension_numbers=('NWC', 'WIO', 'NWC'),
                                    feature_group_count=u.shape[-1])


def axial_rope(x, row_pos, col_pos):
    half = HEAD_DIM // 2
    quarter = half // 2
    freqs = 1.0 / (ROPE_BASE ** (jnp.arange(quarter, dtype=jnp.float32) / quarter))

    def rot(xp, pos):
        ang = pos.astype(jnp.float32)[:, None] * freqs[None, :]
        cos = jnp.cos(ang)[None, :, None, :]
        sin = jnp.sin(ang)[None, :, None, :]
        x1 = xp[..., :quarter].astype(jnp.float32)
        x2 = xp[..., quarter:].astype(jnp.float32)
        return jnp.concatenate([x1 * cos - x2 * sin, x2 * cos + x1 * sin], axis=-1)

    out = jnp.concatenate([rot(x[..., :half], row_pos), rot(x[..., half:], col_pos)], axis=-1)
    return out.astype(x.dtype)


def context_attention(qc, kc, vc):
    s = jnp.einsum('bqhd,bkhd->bhqk', qc, kc).astype(jnp.float32) * (HEAD_DIM ** -0.5)
    p = jax.nn.softmax(s, axis=-1).astype(vc.dtype)
    o = jnp.einsum('bhqk,bkhd->bqhd', p, vc)
    return o.reshape(*o.shape[:2], ATTN_WIDTH)


def neighbourhood_attention(q, k, v, q_plain, kc, vc, rpb):
    B, S, H, Dh = q.shape
    rows = S // GRID_W
    kr = min(WIN_ROWS, rows)
    n_cb = GRID_W // COL_BLOCK
    scale = Dh ** -0.5
    qg = q.reshape(B, rows, n_cb, COL_BLOCK, H, Dh)
    qpg = q_plain.reshape(B, rows, n_cb, COL_BLOCK, H, Dh)
    kg = k.reshape(B, rows, GRID_W, H, Dh)
    vg = v.reshape(B, rows, GRID_W, H, Dh)
    q_col = np.arange(GRID_W).reshape(n_cb, COL_BLOCK)
    q_cstart = np.clip(q_col - WIN_COLS // 2, 0, GRID_W - WIN_COLS)
    blk_start = np.clip(np.arange(n_cb) * COL_BLOCK - WIN_COLS // 2, 0, GRID_W - KEY_SPAN)
    key_col = blk_start[:, None] + np.arange(KEY_SPAN)
    col_valid = ((key_col[:, None, :] >= q_cstart[..., None]) &
                 (key_col[:, None, :] < q_cstart[..., None] + WIN_COLS))
    col_off = np.clip(key_col[:, None, :] - q_col[..., None] + WIN_COLS - 1, 0, 2 * WIN_COLS - 2)
    rpb_cols = rpb[:, :, col_off]
    n_loc = kr * KEY_SPAN

    def row_step(r):
        rs = jnp.clip(r - kr // 2, 0, rows - kr)
        kb = lax.dynamic_slice_in_dim(kg, rs, kr, axis=1)[:, :, key_col]
        vb = lax.dynamic_slice_in_dim(vg, rs, kr, axis=1)[:, :, key_col]
        qr = lax.dynamic_index_in_dim(qg, r, axis=1, keepdims=False)
        qpr = lax.dynamic_index_in_dim(qpg, r, axis=1, keepdims=False)
        row_off = rs + jnp.arange(kr) - r + WIN_ROWS - 1
        bias = jnp.take(rpb_cols, row_off, axis=1).transpose(0, 2, 3, 1, 4)
        s_loc = (jnp.einsum('bjqhd,brjkhd->bhjqrk', qr, kb).astype(jnp.float32) * scale
                 + bias[None].astype(jnp.float32))
        s_loc = jnp.where(col_valid[None, None, :, :, None, :], s_loc, NEG_INF)
        s_ctx = jnp.einsum('bjqhd,bkhd->bhjqk', qpr, kc).astype(jnp.float32) * scale
        s = jnp.concatenate([s_loc.reshape(B, H, n_cb, COL_BLOCK, n_loc), s_ctx], axis=-1)
        p = jax.nn.softmax(s, axis=-1).astype(v.dtype)
        p_loc = p[..., :n_loc].reshape(B, H, n_cb, COL_BLOCK, kr, KEY_SPAN)
        p_ctx = p[..., n_loc:]
        o = (jnp.einsum('bhjqrk,brjkhd->bjqhd', p_loc, vb)
             + jnp.einsum('bhjqk,bkhd->bjqhd', p_ctx, vc))
        return o.reshape(B, GRID_W, H * Dh)

    out = lax.map(row_step, jnp.arange(rows))
    return out.transpose(1, 0, 2, 3).reshape(B, S, H * Dh)


def conformer_conv(a, g, dw_w, dw_b, ln_g, ln_b, w_out):
    u = depthwise_conv(a * jax.nn.sigmoid(g), dw_w) + dw_b
    u = jax.nn.silu(layernorm(u, ln_g, ln_b))
    return u @ w_out


def short_conv(b_gate, c_gate, u, w, w_out):
    return (b_gate * depthwise_conv(c_gate * u, w)) @ w_out


def merge_branches(attn_out, pieces, w_attn_o, conv_dw_w, conv_dw_b, conv_ln_g, conv_ln_b,
                   w_conv_o, sc_w, w_sc_o, w_o):
    _, _, _, glu_a, glu_g, sc_b, sc_c, sc_x, gates = pieces
    y_a = attn_out @ w_attn_o
    y_b = conformer_conv(glu_a, glu_g, conv_dw_w, conv_dw_b, conv_ln_g, conv_ln_b, w_conv_o)
    y_c = short_conv(sc_b, sc_c, sc_x, sc_w, w_sc_o)
    g_a, g_b, g_c = jnp.split(jax.nn.sigmoid(gates), N_BRANCH, axis=-1)
    return (g_a * y_a + g_b * y_b + g_c * y_c) @ w_o


def expert_choice_ffn(h, w_router, w1, w3, w2):
    B, n, D = h.shape
    cap = max(1, CAP_FACTOR * n // N_EXPERTS)
    aff = jax.nn.softmax(jnp.einsum('bnd,de->bne', h, w_router).astype(jnp.float32), axis=-1)
    g, idx = lax.top_k(aff.transpose(0, 2, 1), cap)
    xs = jax.vmap(lambda hb, ib: hb[ib])(h, idx)
    a = jnp.einsum('becd,edf->becf', xs, w1)
    b3 = jnp.einsum('becd,edf->becf', xs, w3)
    y = jnp.einsum('becf,efd->becd', jax.nn.silu(a) * b3, w2) * g[..., None].astype(h.dtype)
    return jax.vmap(lambda yb, ib: jnp.zeros((n, D), yb.dtype).at[ib.reshape(-1)].add(yb.reshape(-1, D)))(y, idx)


def setup_inputs(seed: int = 0) -> dict:
    key = jax.random.key(seed)
    ks = jax.random.split(key, 32)
    L, D = DEPTH, D_MODEL

    def nrm(k, shape, scale):
        return jax.random.normal(k, shape, jnp.float32) * scale

    return {
        "x": nrm(ks[0], (BATCH, SEQ, D), 1.0),
        "c": nrm(ks[1], (BATCH, D), 1.0),
        "ctx": nrm(ks[2], (BATCH, CTX_LEN, D), 1.0),
        "c_ctx": nrm(ks[3], (D,), 1.0),
        "w_ada": nrm(ks[4], (L, D, 6 * D), 0.25 * D ** -0.5),
        "b_ada": nrm(ks[5], (L, 6 * D), 0.02),
        "g_norm1": 1.0 + nrm(ks[6], (L, D), 0.1),
        "g_norm2": 1.0 + nrm(ks[7], (L, D), 0.1),
        "w_in": nrm(ks[8], (L, D, PROJ_WIDTH), D ** -0.5),
        "b_in": nrm(ks[9], (L, PROJ_WIDTH), 0.02),
        "g_q": 1.0 + nrm(ks[10], (L, HEAD_DIM), 0.1),
        "g_k": 1.0 + nrm(ks[11], (L, HEAD_DIM), 0.1),
        "rpb": nrm(ks[12], (L, N_HEADS, 2 * WIN_ROWS - 1, 2 * WIN_COLS - 1), 0.5),
        "w_attn_o": nrm(ks[13], (L, ATTN_WIDTH, D), ATTN_WIDTH ** -0.5),
        "conv_dw_w": nrm(ks[14], (L, CONV_K, CONV_WIDTH), CONV_K ** -0.5),
        "conv_dw_b": nrm(ks[15], (L, CONV_WIDTH), 0.02),
        "conv_ln_g": 1.0 + nrm(ks[16], (L, CONV_WIDTH), 0.1),
        "conv_ln_b": nrm(ks[17], (L, CONV_WIDTH), 0.02),
        "w_conv_o": nrm(ks[18], (L, CONV_WIDTH, D), CONV_WIDTH ** -0.5),
        "sc_w": nrm(ks[19], (L, SC_K, SC_WIDTH), SC_K ** -0.5),
        "w_sc_o": nrm(ks[20], (L, SC_WIDTH, D), SC_WIDTH ** -0.5),
        "w_o": nrm(ks[21], (L, D, D), D ** -0.5),
        "w_router": nrm(ks[22], (L, D, N_EXPERTS), D ** -0.5),
        "w_e1": nrm(ks[23], (L, N_EXPERTS, D, EXPERT_FF), D ** -0.5),
        "w_e3": nrm(ks[24], (L, N_EXPERTS, D, EXPERT_FF), D ** -0.5),
        "w_e2": nrm(ks[25], (L, N_EXPERTS, EXPERT_FF, D), EXPERT_FF ** -0.5),
    }


def reference(x, c, ctx, c_ctx, w_ada, b_ada, g_norm1, g_norm2, w_in, b_in, g_q, g_k, rpb, w_attn_o,
              conv_dw_w, conv_dw_b, conv_ln_g, conv_ln_b, w_conv_o, sc_w, w_sc_o, w_o, w_router,
              w_e1, w_e3, w_e2):
    B, S, D = x.shape
    t = jnp.arange(S)
    row_pos, col_pos = t // GRID_W, t % GRID_W
    xc = ctx
    for l in range(DEPTH):
        last = l == DEPTH - 1
        mod = jnp.split(jax.nn.silu(c) @ w_ada[l] + b_ada[l], 6, axis=-1)
        n_mod = 2 if last else 6
        mod_c = jnp.split(jax.nn.silu(c_ctx) @ w_ada[l][:, :n_mod * D] + b_ada[l][:n_mod * D], n_mod, axis=-1)
        branch_params = (w_attn_o[l], conv_dw_w[l], conv_dw_b[l], conv_ln_g[l], conv_ln_b[l],
                         w_conv_o[l], sc_w[l], w_sc_o[l], w_o[l])

        hc = modulate(rmsnorm(xc, g_norm1[l]), mod_c[0], mod_c[1])
        if last:
            kv = hc @ w_in[l][:, ATTN_WIDTH:3 * ATTN_WIDTH] + b_in[l][ATTN_WIDTH:3 * ATTN_WIDTH]
            kc_raw, vc_raw = jnp.split(kv, 2, axis=-1)
            kc = rmsnorm(heads(kc_raw), g_k[l])
            vc = heads(vc_raw)
        else:
            pc = split_proj(hc @ w_in[l] + b_in[l])
            qc = rmsnorm(heads(pc[0]), g_q[l])
            kc = rmsnorm(heads(pc[1]), g_k[l])
            vc = heads(pc[2])
            mix_c = merge_branches(context_attention(qc, kc, vc), pc, *branch_params)
            xc_mid = xc + mod_c[2] * mix_c
            hc2 = modulate(rmsnorm(xc_mid, g_norm2[l]), mod_c[3], mod_c[4])
            xc_next = xc_mid + mod_c[5] * expert_choice_ffn(hc2, w_router[l], w_e1[l], w_e3[l], w_e2[l])

        h = modulate(rmsnorm(x, g_norm1[l]), mod[0][:, None], mod[1][:, None])
        p = split_proj(h @ w_in[l] + b_in[l])
        q = rmsnorm(heads(p[0]), g_q[l])
        k = rmsnorm(heads(p[1]), g_k[l])
        v = heads(p[2])
        q_rot = axial_rope(q, row_pos, col_pos)
        k_rot = axial_rope(k, row_pos, col_pos)
        attn = neighbourhood_attention(q_rot, k_rot, v, q, kc, vc, rpb[l])
        x = x + mod[2][:, None] * merge_branches(attn, p, *branch_params)
        h2 = modulate(rmsnorm(x, g_norm2[l]), mod[3][:, None], mod[4][:, None])
        x = x + mod[5][:, None] * expert_choice_ffn(h2, w_router[l], w_e1[l], w_e3[l], w_e2[l])

        if not last:
            xc = xc_next
    return x
```

```python
import functools

import numpy as np
import jax
import jax.numpy as jnp
from jax import lax
from jax.experimental import pallas as pl
from jax.experimental.pallas import tpu as pltpu

F32 = jnp.float32
BF16 = jnp.bfloat16

GRID_W = 64
N_HEADS = 16
HEAD_DIM = 64
ATTN_WIDTH = N_HEADS * HEAD_DIM
N_HEAD_PAIRS = N_HEADS // 2
WIN_ROWS = 8
WIN_COLS = 16
ROPE_BASE = 10000.0
CONV_WIDTH = 512
CONV_K = 31
SC_WIDTH = 512
SC_K = 3
N_EXPERTS = 16
CAP_FACTOR = 2
EPS = 1e-6
NEG_INF = -1e30

LANES = 128
HALO = 16
ROW_GROUP = 4
KEY_ROWS = 12
N_PAIR_BLOCKS = 26
VMEM_LIMIT = 56 * 1024 * 1024


def _cparams(*sem):
    return pltpu.CompilerParams(dimension_semantics=sem, vmem_limit_bytes=VMEM_LIMIT)


def _dot(a, b):
    return jnp.dot(a, b, preferred_element_type=F32)


def _dot_t(a, b):
    return lax.dot_general(a, b, (((1,), (1,)), ((), ())), preferred_element_type=F32)


def _silu(x):
    return x * jax.nn.sigmoid(x)


def _ada_kernel(cc_ref, w_ref, b_ref, o_ref):
    s = _silu(cc_ref[...]).astype(BF16)
    o_ref[0] = _dot(s, w_ref[0].astype(BF16)) + b_ref[0]


def _ada(cc, w_ada, b_ada):
    L, D, W = w_ada.shape
    tn = 1024
    return pl.pallas_call(
        _ada_kernel,
        grid=(L, W // tn),
        in_specs=[pl.BlockSpec((8, D), lambda l, j: (0, 0)),
                  pl.BlockSpec((1, D, tn), lambda l, j: (l, 0, j)),
                  pl.BlockSpec((1, 1, tn), lambda l, j: (l, 0, j))],
        out_specs=pl.BlockSpec((1, 8, tn), lambda l, j: (l, 0, j)),
        out_shape=jax.ShapeDtypeStruct((L, 8, W), F32),
        compiler_params=_cparams("arbitrary", "arbitrary"),
        name="ada",
    )(cc, w_ada, b_ada.reshape(L, 1, W))


def _mod_spec(nmod, tiles_per_mod, D):
    if nmod == 1:
        return pl.BlockSpec((1, 1, D), lambda i: (0, 0, 0))
    return pl.BlockSpec((1, 1, D), lambda i: (i // tiles_per_mod, 0, 0))


def _normmod_kernel(*refs, has_moe):
    if has_moe:
        x_ref, moe_ref, gate_ref, g_ref, sh_ref, sc_ref, xo_ref, h_ref = refs
        x = x_ref[...] + gate_ref[0] * moe_ref[...]
        xo_ref[...] = x
    else:
        x_ref, g_ref, sh_ref, sc_ref, h_ref = refs
        x = x_ref[...]
    ms = jnp.mean(x * x, axis=-1, keepdims=True)
    y = x * lax.rsqrt(ms + EPS) * g_ref[...]
    h_ref[...] = (y * (1.0 + sc_ref[0]) + sh_ref[0]).astype(BF16)


def _normmod(x, g, shift, scale, seq_len, moe=None, gate=None):
    rows, D = x.shape
    tm = min(512, seq_len)
    nmod = shift.shape[0]
    mspec = _mod_spec(nmod, seq_len // tm, D)
    xspec = pl.BlockSpec((tm, D), lambda i: (i, 0))
    gspec = pl.BlockSpec((1, D), lambda i: (0, 0))
    has_moe = moe is not None
    if has_moe:
        ins = (x, moe, gate, g, shift, scale)
        in_specs = [xspec, xspec, mspec, gspec, mspec, mspec]
        out_specs = [xspec, xspec]
        out_shape = [jax.ShapeDtypeStruct((rows, D), F32), jax.ShapeDtypeStruct((rows, D), BF16)]
    else:
        ins = (x, g, shift, scale)
        in_specs = [xspec, gspec, mspec, mspec]
        out_specs = xspec
        out_shape = jax.ShapeDtypeStruct((rows, D), BF16)
    return pl.pallas_call(
        functools.partial(_normmod_kernel, has_moe=has_moe),
        grid=(rows // tm,), in_specs=in_specs, out_specs=out_specs, out_shape=out_shape,
        compiler_params=_cparams("arbitrary"), name="normmod",
    )(*ins)


def _resid_kernel(x_ref, moe_ref, gate_ref, o_ref):
    o_ref[...] = x_ref[...] + gate_ref[0] * moe_ref[...]


def _resid(x, moe, gate, seq_len):
    rows, D = x.shape
    tm = min(512, seq_len)
    xspec = pl.BlockSpec((tm, D), lambda i: (i, 0))
    return pl.pallas_call(
        _resid_kernel, grid=(rows // tm,),
        in_specs=[xspec, xspec, _mod_spec(gate.shape[0], seq_len // tm, D)],
        out_specs=xspec, out_shape=jax.ShapeDtypeStruct((rows, D), F32),
        compiler_params=_cparams("arbitrary"), name="resid",
    )(x, moe, gate)


def _qkv_kernel(*refs, rope):
    if rope:
        (h_ref, w_ref, b_ref, gq_ref, gk_ref, bd_ref, ex_ref, cos_ref, sa_ref, sb_ref,
         qn_ref, kn_ref, v_ref, qr_ref, kr_ref) = refs
    else:
        h_ref, w_ref, b_ref, gq_ref, gk_ref, bd_ref, ex_ref, qn_ref, kn_ref, v_ref = refs
    h = h_ref[...]
    aw = ATTN_WIDTH

    def proj(s):
        return _dot(h, w_ref[:, s * aw:(s + 1) * aw]) + b_ref[:, s * aw:(s + 1) * aw]

    def headnorm(a, g_ref):
        ss = _dot((a * a).astype(BF16), bd_ref[...])
        inv = lax.rsqrt(ss * (1.0 / HEAD_DIM) + EPS)
        inv_hi = inv.astype(BF16)
        inv_lo = (inv - inv_hi.astype(F32)).astype(BF16)
        full = _dot(inv_hi, ex_ref[...]) + _dot(inv_lo, ex_ref[...])
        return a * full * g_ref[...]

    def store_plain(xn, out_ref):
        for c in range(N_HEAD_PAIRS):
            out_ref[c] = xn[:, c * LANES:(c + 1) * LANES].astype(BF16)

    def store_rope(xn, out_ref):
        cos, sa, sb = cos_ref[...], sa_ref[...], sb_ref[...]
        for c in range(N_HEAD_PAIRS):
            xc = xn[:, c * LANES:(c + 1) * LANES]
            r = xc * cos + pltpu.roll(xc, LANES - 16, 1) * sa + pltpu.roll(xc, 16, 1) * sb
            out_ref[c] = r.astype(BF16)

    qn = headnorm(proj(0), gq_ref)
    store_plain(qn, qn_ref)
    if rope:
        store_rope(qn, qr_ref)
    kn = headnorm(proj(1), gk_ref)
    store_plain(kn, kn_ref)
    if rope:
        store_rope(kn, kr_ref)
    store_plain(proj(2), v_ref)


def _qkv(h, w, b, gq, gk, bd, ex, rope_tabs, seq_len):
    rows, D = h.shape
    tm = min(512, seq_len)
    rope = rope_tabs is not None
    const = lambda i: (0, 0)
    in_specs = [pl.BlockSpec((tm, D), lambda i: (i, 0)),
                pl.BlockSpec((D, 3 * ATTN_WIDTH), const, pipeline_mode=pl.Buffered(1)),
                pl.BlockSpec((1, 3 * ATTN_WIDTH), const),
                pl.BlockSpec((1, ATTN_WIDTH), const),
                pl.BlockSpec((1, ATTN_WIDTH), const),
                pl.BlockSpec((ATTN_WIDTH, LANES), const),
                pl.BlockSpec((LANES, ATTN_WIDTH), const)]
    ins = [h, w, b, gq, gk, bd, ex]
    n_out = 3
    if rope:
        tiles_per_seq = seq_len // tm
        tspec = pl.BlockSpec((tm, LANES), lambda i: (i % tiles_per_seq, 0))
        in_specs += [tspec, tspec, tspec]
        ins += list(rope_tabs)
        n_out = 5
    ospec = pl.BlockSpec((N_HEAD_PAIRS, tm, LANES), lambda i: (0, i, 0))
    return pl.pallas_call(
        functools.partial(_qkv_kernel, rope=rope),
        grid=(rows // tm,), in_specs=in_specs,
        out_specs=[ospec] * n_out,
        out_shape=[jax.ShapeDtypeStruct((N_HEAD_PAIRS, rows, LANES), BF16)] * n_out,
        compiler_params=_cparams("arbitrary"), name="qkv",
    )(*ins)


def _proj_kernel(h_ref, w_ref, b_ref, o_ref):
    o_ref[...] = (_dot(h_ref[...], w_ref[...]) + b_ref[...]).astype(o_ref.dtype)


def _proj(h, w, b, tm, tn):
    rows, D = h.shape
    N = w.shape[1]
    return pl.pallas_call(
        _proj_kernel, grid=(rows // tm, N // tn),
        in_specs=[pl.BlockSpec((tm, D), lambda i, j: (i, 0)),
                  pl.BlockSpec((D, tn), lambda i, j: (0, j)),
                  pl.BlockSpec((1, tn), lambda i, j: (0, j))],
        out_specs=pl.BlockSpec((tm, tn), lambda i, j: (i, j)),
        out_shape=jax.ShapeDtypeStruct((rows, N), BF16),
        compiler_params=_cparams("arbitrary", "arbitrary"), name="proj",
    )(h, w, b)


def _attn_kernel(qn_ref, qr_ref, kr_ref, v_ref, kc_ref, vc_ref, fp_ref, o_ref, *, rows):
    scale = HEAD_DIM ** -0.5
    gq = ROW_GROUP * GRID_W
    nk = KEY_ROWS * GRID_W
    lane = lax.broadcasted_iota(jnp.int32, (1, LANES), 1)
    head_mask = (lane < HEAD_DIM, lane >= HEAD_DIM)
    key_row = lax.broadcasted_iota(jnp.int32, (1, nk), 1) // GRID_W
    kc = kc_ref[0]
    vc = vc_ref[0]

    def group(gi, carry):
        r0 = gi * ROW_GROUP
        ws = jnp.clip(r0 - WIN_ROWS // 2, 0, rows - KEY_ROWS)
        tok0 = pl.multiple_of(gi * gq, gq)
        key0 = pl.multiple_of(ws * GRID_W, GRID_W)
        qr = qr_ref[0, pl.ds(tok0, gq), :]
        qn = qn_ref[0, pl.ds(tok0, gq), :]
        kw = kr_ref[0, pl.ds(key0, nk), :]
        vw = v_ref[0, pl.ds(key0, nk), :]
        outs = []
        for hh in range(2):
            qrm = jnp.where(head_mask[hh], qr, 0) * scale
            qnm = jnp.where(head_mask[hh], qn, 0) * scale
            s_loc = _dot_t(qrm, kw)
            s_ctx = _dot_t(qnm, kc)
            slabs = []
            for i in range(ROW_GROUP):
                n0 = ws - r0 - i + (WIN_ROWS - 1) + 8
                bias = jnp.concatenate([fp_ref[hh, n0 + 2 * jp] for jp in range(KEY_ROWS // 2)], axis=1)
                lo = jnp.clip(r0 + i - WIN_ROWS // 2, 0, rows - WIN_ROWS) - ws
                valid = (key_row >= lo) & (key_row < lo + WIN_ROWS)
                slabs.append(jnp.where(valid, s_loc[i * GRID_W:(i + 1) * GRID_W] + bias, NEG_INF))
            s_loc = jnp.concatenate(slabs, axis=0)
            m = jnp.maximum(jnp.max(s_loc, axis=-1, keepdims=True), jnp.max(s_ctx, axis=-1, keepdims=True))
            p_loc = jnp.exp(s_loc - m)
            p_ctx = jnp.exp(s_ctx - m)
            denom = jnp.sum(p_loc, axis=-1, keepdims=True) + jnp.sum(p_ctx, axis=-1, keepdims=True)
            o = _dot(p_loc.astype(BF16), vw) + _dot(p_ctx.astype(BF16), vc)
            outs.append(o / denom)
        o_ref[0, pl.ds(tok0, gq), :] = jnp.where(head_mask[0], outs[0], outs[1]).astype(BF16)
        return carry

    lax.fori_loop(0, rows // ROW_GROUP, group, 0)


def _attn(qn, qr, kr, v, kc, vc, fp, B, S, C):
    rows = S // GRID_W
    tok = pl.BlockSpec((1, S, LANES), lambda b, p: (p, b, 0))
    ctx = pl.BlockSpec((1, C, LANES), lambda b, p: (p, b, 0))
    return pl.pallas_call(
        functools.partial(_attn_kernel, rows=rows),
        grid=(B, N_HEAD_PAIRS),
        in_specs=[tok, tok, tok, tok, ctx, ctx,
                  pl.BlockSpec((2, N_PAIR_BLOCKS, GRID_W, LANES), lambda b, p: (p, 0, 0, 0))],
        out_specs=tok,
        out_shape=jax.ShapeDtypeStruct((N_HEAD_PAIRS, B * S, LANES), BF16),
        compiler_params=_cparams("arbitrary", "arbitrary"), name="attn",
    )(qn, qr, kr, v, kc, vc, fp)


def _ctx_attn_kernel(q_ref, k_ref, v_ref, o_ref):
    scale = HEAD_DIM ** -0.5
    lane = lax.broadcasted_iota(jnp.int32, (1, LANES), 1)
    head_mask = (lane < HEAD_DIM, lane >= HEAD_DIM)
    q, k, v = q_ref[0], k_ref[0], v_ref[0]
    outs = []
    for hh in range(2):
        s = _dot_t(jnp.where(head_mask[hh], q, 0) * scale, k)
        m = jnp.max(s, axis=-1, keepdims=True)
        p = jnp.exp(s - m)
        outs.append(_dot(p.astype(BF16), v) / jnp.sum(p, axis=-1, keepdims=True))
    o_ref[0] = jnp.where(head_mask[0], outs[0], outs[1]).astype(BF16)


def _ctx_attn(q, k, v, B, C):
    spec = pl.BlockSpec((1, C, LANES), lambda b, p: (p, b, 0))
    return pl.pallas_call(
        _ctx_attn_kernel, grid=(B, N_HEAD_PAIRS),
        in_specs=[spec, spec, spec], out_specs=spec,
        out_shape=jax.ShapeDtypeStruct((N_HEAD_PAIRS, B * C, LANES), BF16),
        compiler_params=_cparams("arbitrary", "arbitrary"), name="ctx_attn",
    )(q, k, v)


def _branch_kernel(attn_ref, pc_ref, pcp_ref, pcn_ref, pg_ref, wa_ref, wc_ref, ws_ref,
                   dww_ref, dwb_ref, lng_ref, lnb_ref, scw_ref, o_ref, glu_ext, sc_ext, *, tm, seq_len, D):
    i = pl.program_id(0)
    pos0 = (i * tm) % seq_len
    keep_prev = (pos0 != 0).astype(F32)
    keep_next = (pos0 + tm != seq_len).astype(F32)
    cw, sw = CONV_WIDTH, SC_WIDTH

    def glu(blk):
        return blk[:, 0:cw].astype(F32) * jax.nn.sigmoid(blk[:, cw:2 * cw].astype(F32))

    def gated_x(blk):
        o = 2 * cw + sw
        return blk[:, o:o + sw].astype(F32) * blk[:, o + sw:o + 2 * sw].astype(F32)

    cur, prv, nxt = pc_ref[...], pcp_ref[...], pcn_ref[...]
    glu_ext[0:HALO, :] = glu(prv) * keep_prev
    glu_ext[HALO:HALO + tm, :] = glu(cur)
    glu_ext[HALO + tm:2 * HALO + tm, :] = glu(nxt) * keep_next
    sc_ext[0:HALO, :] = gated_x(prv) * keep_prev
    sc_ext[HALO:HALO + tm, :] = gated_x(cur)
    sc_ext[HALO + tm:2 * HALO + tm, :] = gated_x(nxt) * keep_next

    chunk = 64
    conv_rows = []
    for r in range(0, tm, chunk):
        acc = jnp.zeros((chunk, cw), F32)
        for k in range(CONV_K):
            base = HALO - CONV_K // 2 + k + r
            acc = acc + dww_ref[k:k + 1, :] * glu_ext[base:base + chunk, :]
        conv_rows.append(acc)
    u = jnp.concatenate(conv_rows, axis=0) + dwb_ref[...]
    mu = jnp.mean(u, axis=-1, keepdims=True)
    uc = u - mu
    var = jnp.mean(uc * uc, axis=-1, keepdims=True)
    u = _silu(uc * lax.rsqrt(var + EPS) * lng_ref[...] + lnb_ref[...])
    y_b = _dot(u.astype(BF16), wc_ref[...])

    c3 = jnp.zeros((tm, sw), F32)
    for k in range(SC_K):
        base = HALO - SC_K // 2 + k
        c3 = c3 + scw_ref[k:k + 1, :] * sc_ext[base:base + tm, :]
    sc_b = cur[:, 2 * cw:2 * cw + sw].astype(F32)
    y_c = _dot((sc_b * c3).astype(BF16), ws_ref[...])

    attn = jnp.concatenate([attn_ref[c] for c in range(N_HEAD_PAIRS)], axis=1)
    y_a = _dot(attn, wa_ref[...])

    g = pg_ref[...]
    merged = (jax.nn.sigmoid(g[:, 0:D].astype(F32)) * y_a
              + jax.nn.sigmoid(g[:, D:2 * D].astype(F32)) * y_b
              + jax.nn.sigmoid(g[:, 2 * D:3 * D].astype(F32)) * y_c)
    o_ref[...] = merged.astype(BF16)


def _branch(attn, pc, pg, wa, wc, ws, dww, dwb, lng, lnb, scw, seq_len, D):
    rows = pc.shape[0]
    tm = 256
    pcw = pc.shape[1]
    nhalo = rows // HALO
    const = lambda i: (0, 0)
    return pl.pallas_call(
        functools.partial(_branch_kernel, tm=tm, seq_len=seq_len, D=D),
        grid=(rows // tm,),
        in_specs=[pl.BlockSpec((N_HEAD_PAIRS, tm, LANES), lambda i: (0, i, 0)),
                  pl.BlockSpec((tm, pcw), lambda i: (i, 0)),
                  pl.BlockSpec((HALO, pcw), lambda i: (jnp.maximum(i * (tm // HALO) - 1, 0), 0)),
                  pl.BlockSpec((HALO, pcw), lambda i: (jnp.minimum((i + 1) * (tm // HALO), nhalo - 1), 0)),
                  pl.BlockSpec((tm, 3 * D), lambda i: (i, 0)),
                  pl.BlockSpec((ATTN_WIDTH, D), const),
                  pl.BlockSpec((CONV_WIDTH, D), const),
                  pl.BlockSpec((SC_WIDTH, D), const),
                  pl.BlockSpec((32, CONV_WIDTH), const),
                  pl.BlockSpec((1, CONV_WIDTH), const),
                  pl.BlockSpec((1, CONV_WIDTH), const),
                  pl.BlockSpec((1, CONV_WIDTH), const),
                  pl.BlockSpec((8, SC_WIDTH), const)],
        out_specs=pl.BlockSpec((tm, D), lambda i: (i, 0)),
        out_shape=jax.ShapeDtypeStruct((rows, D), BF16),
        scratch_shapes=[pltpu.VMEM((tm + 2 * HALO, CONV_WIDTH), F32),
                        pltpu.VMEM((tm + 2 * HALO, SC_WIDTH), F32)],
        compiler_params=_cparams("arbitrary"), name="branch",
    )(attn, pc, pc, pc, pg, wa, wc, ws, dww, dwb, lng, lnb, scw)


def _outproj_kernel(m_ref, x_ref, wo_ref, gate_ref, g2_ref, sh_ref, sc_ref, wr_ref, xmid_ref, h2_ref, aff_ref):
    x = x_ref[...] + gate_ref[0] * _dot(m_ref[...], wo_ref[...])
    xmid_ref[...] = x
    ms = jnp.mean(x * x, axis=-1, keepdims=True)
    y = x * lax.rsqrt(ms + EPS) * g2_ref[...]
    h2 = (y * (1.0 + sc_ref[0]) + sh_ref[0]).astype(BF16)
    h2_ref[...] = h2
    logits = _dot(h2, wr_ref[...])
    lane = lax.broadcasted_iota(jnp.int32, logits.shape, 1)
    logits = jnp.where(lane < N_EXPERTS, logits, NEG_INF)
    e = jnp.exp(logits - jnp.max(logits, axis=-1, keepdims=True))
    aff_ref[...] = e / jnp.sum(e, axis=-1, keepdims=True)


def _outproj(merged, x, wo, gate, g2, shift, scale, wr, seq_len):
    rows, D = x.shape
    tm = 256
    mspec = _mod_spec(gate.shape[0], seq_len // tm, D)
    row = lambda i: (i, 0)
    const = lambda i: (0, 0)
    return pl.pallas_call(
        _outproj_kernel, grid=(rows // tm,),
        in_specs=[pl.BlockSpec((tm, D), row), pl.BlockSpec((tm, D), row),
                  pl.BlockSpec((D, D), const), mspec,
                  pl.BlockSpec((1, D), const), mspec, mspec,
                  pl.BlockSpec((D, LANES), const)],
        out_specs=[pl.BlockSpec((tm, D), row), pl.BlockSpec((tm, D), row), pl.BlockSpec((tm, LANES), row)],
        out_shape=[jax.ShapeDtypeStruct((rows, D), F32), jax.ShapeDtypeStruct((rows, D), BF16),
                   jax.ShapeDtypeStruct((rows, LANES), F32)],
        compiler_params=_cparams("arbitrary"), name="outproj",
    )(merged, x, wo, gate, g2, shift, scale, wr)


MAX_BISECT = 160


def _select_kernel(aff_ref, tri_ref, srow_ref, scol_ref, *, n, cap, blk):
    E = N_EXPERTS
    a = aff_ref[0].T[0:E, :]
    capf = float(cap)

    def cond(st):
        it, _, _, ndone = st
        return jnp.logical_and(it < MAX_BISECT, ndone < float(E))

    def in_range(lo, hi):
        return jnp.logical_and(a >= lo, a < hi)

    def body(st):
        it, lo, hi, _ = st
        mid = 0.5 * (lo + hi)
        cnt = jnp.sum((a >= mid).astype(F32), axis=-1, keepdims=True)
        ge = cnt >= capf
        lo = jnp.where(ge, mid, lo)
        hi = jnp.where(ge, hi, mid)
        r = in_range(lo, hi)
        vmin = jnp.min(jnp.where(r, a, 4.0), axis=-1, keepdims=True)
        vmax = jnp.max(jnp.where(r, a, -1.0), axis=-1, keepdims=True)
        ndone = jnp.sum((vmin == vmax).astype(F32))
        return it + 1, lo, hi, ndone

    _, lo, hi, _ = lax.while_loop(cond, body, (jnp.int32(0), jnp.zeros((E, 1), F32), jnp.full((E, 1), 2.0, F32),
                                               jnp.float32(0.0)))
    thr = jnp.max(jnp.where(in_range(lo, hi), a, -1.0), axis=-1, keepdims=True)

    def cumsum(x):
        outs, carry = [], jnp.zeros((E, 1), F32)
        for c in range(n // blk):
            part = _dot(x[:, c * blk:(c + 1) * blk].astype(BF16), tri_ref[...]) + carry
            outs.append(part)
            carry = part[:, blk - 1:blk]
        return jnp.concatenate(outs, axis=1)

    gt = a > thr
    eq = (a == thr).astype(F32)
    need = capf - jnp.sum(gt.astype(F32), axis=-1, keepdims=True)
    eq_rank = cumsum(eq) - eq
    sel = jnp.logical_or(gt, jnp.logical_and(eq > 0.0, eq_rank < need)).astype(F32)
    slot1 = sel * cumsum(sel)
    srow_ref[0] = slot1
    scol_ref[0] = jnp.concatenate([slot1, jnp.zeros((LANES - E, n), F32)], axis=0).T


def _select(aff, B, n, cap):
    blk = min(512, n)
    tri = (np.arange(blk)[:, None] <= np.arange(blk)[None, :]).astype(np.float32)
    return pl.pallas_call(
        functools.partial(_select_kernel, n=n, cap=cap, blk=blk),
        grid=(B,),
        in_specs=[pl.BlockSpec((1, n, LANES), lambda b: (b, 0, 0)),
                  pl.BlockSpec((blk, blk), lambda b: (0, 0))],
        out_specs=[pl.BlockSpec((1, N_EXPERTS, n), lambda b: (b, 0, 0)),
                   pl.BlockSpec((1, n, LANES), lambda b: (b, 0, 0))],
        out_shape=[jax.ShapeDtypeStruct((B, N_EXPERTS, n), F32),
                   jax.ShapeDtypeStruct((B, n, LANES), F32)],
        compiler_params=_cparams("arbitrary"), name="select",
    )(aff.reshape(B, n, LANES), jnp.asarray(tri, BF16))


def _dispatch_kernel(srow_ref, h_ref, o_ref, *, n, cap, tchunk):
    slot = lax.broadcasted_iota(jnp.int32, (cap, 1), 0).astype(F32) + 1.0
    acc = None
    for c in range(n // tchunk):
        onehot = (srow_ref[0, 0, :, c * tchunk:(c + 1) * tchunk] == slot).astype(BF16)
        part = _dot(onehot, h_ref[0, c * tchunk:(c + 1) * tchunk, :])
        acc = part if acc is None else acc + part
    o_ref[0, 0] = acc.astype(BF16)


def _dispatch(srow, h2, B, n, cap):
    D = h2.shape[1]
    dn = min(1024, D)
    tchunk = min(1024, n)
    return pl.pallas_call(
        functools.partial(_dispatch_kernel, n=n, cap=cap, tchunk=tchunk),
        grid=(B, D // dn, N_EXPERTS),
        in_specs=[pl.BlockSpec((1, 1, 1, n), lambda b, d, e: (b, e, 0, 0)),
                  pl.BlockSpec((1, n, dn), lambda b, d, e: (b, 0, d))],
        out_specs=pl.BlockSpec((1, 1, cap, dn), lambda b, d, e: (b, e, 0, d)),
        out_shape=jax.ShapeDtypeStruct((B, N_EXPERTS, cap, D), BF16),
        compiler_params=_cparams("arbitrary", "arbitrary", "arbitrary"), name="dispatch",
    )(srow, h2.reshape(B, n, D))


def _expert_kernel(x_ref, w1_ref, w3_ref, w2_ref, o_ref, *, bb, cap):
    D = x_ref.shape[-1]
    x = x_ref[...].reshape(bb * cap, D)
    a = _dot(x, w1_ref[0])
    b3 = _dot(x, w3_ref[0])
    y = _dot((_silu(a) * b3).astype(BF16), w2_ref[0])
    o_ref[...] = y.astype(BF16).reshape(bb, 1, cap, D)


def _experts(xs, w1, w3, w2, bb):
    B, E, cap, D = xs.shape
    FF = w1.shape[-1]
    return pl.pallas_call(
        functools.partial(_expert_kernel, bb=bb, cap=cap),
        grid=(E, B // bb),
        in_specs=[pl.BlockSpec((bb, 1, cap, D), lambda e, b: (b, e, 0, 0)),
                  pl.BlockSpec((1, D, FF), lambda e, b: (e, 0, 0)),
                  pl.BlockSpec((1, D, FF), lambda e, b: (e, 0, 0)),
                  pl.BlockSpec((1, FF, D), lambda e, b: (e, 0, 0))],
        out_specs=pl.BlockSpec((bb, 1, cap, D), lambda e, b: (b, e, 0, 0)),
        out_shape=jax.ShapeDtypeStruct((B, E, cap, D), BF16),
        compiler_params=_cparams("arbitrary", "arbitrary"), name="experts",
    )(xs, w1, w3, w2)


def _combine_kernel(scol_ref, aff_ref, y_ref, o_ref, *, cap):
    e = pl.program_id(3)
    pick = (lax.broadcasted_iota(jnp.int32, (LANES, LANES), 0) == e).astype(BF16)
    s = scol_ref[0]
    s_hi = jnp.floor(s * (1.0 / 32.0))
    s_lo = s - 32.0 * s_hi
    slot = 32.0 * _dot(s_hi.astype(BF16), pick) + _dot(s_lo.astype(BF16), pick)
    a = aff_ref[0]
    a_hi = a.astype(BF16)
    r1 = a - a_hi.astype(F32)
    a_mid = r1.astype(BF16)
    a_lo = (r1 - a_mid.astype(F32)).astype(BF16)
    gate = _dot(a_hi, pick) + _dot(a_mid, pick) + _dot(a_lo, pick)
    width = min(cap, LANES)
    lane1 = lax.broadcasted_iota(jnp.int32, (1, width), 1).astype(F32) + 1.0
    onehot = jnp.concatenate([(slot[:, 0:width] == lane1 + float(j * width)).astype(BF16)
                              for j in range(cap // width)], axis=1)
    contrib = _dot(onehot, y_ref[0, 0])
    nd = contrib.shape[1]
    contrib = contrib * jnp.concatenate([gate] * (nd // LANES), axis=1)

    @pl.when(e == 0)
    def _():
        o_ref[0] = contrib

    @pl.when(e != 0)
    def _():
        o_ref[0] += contrib


def _combine(scol, aff, y, B, n, cap):
    D = y.shape[-1]
    tt = min(2048, n)
    dn = 1024
    return pl.pallas_call(
        functools.partial(_combine_kernel, cap=cap),
        grid=(B, n // tt, D // dn, N_EXPERTS),
        in_specs=[pl.BlockSpec((1, tt, LANES), lambda b, t, d, e: (b, t, 0)),
                  pl.BlockSpec((1, tt, LANES), lambda b, t, d, e: (b, t, 0)),
                  pl.BlockSpec((1, 1, cap, dn), lambda b, t, d, e: (b, e, 0, d))],
        out_specs=pl.BlockSpec((1, tt, dn), lambda b, t, d, e: (b, t, d)),
        out_shape=jax.ShapeDtypeStruct((B, n, D), F32),
        compiler_params=_cparams("arbitrary", "arbitrary", "arbitrary", "arbitrary"), name="combine",
    )(scol, aff.reshape(B, n, LANES), y)


def _moe(h2, aff, w1, w3, w2, B, n, bb):
    cap = max(1, CAP_FACTOR * n // N_EXPERTS)
    srow, scol = _select(aff, B, n, cap)
    xs = _dispatch(srow.reshape(B, N_EXPERTS, 1, n), h2, B, n, cap)
    y = _experts(xs, w1, w3, w2, bb)
    return _combine(scol, aff, y, B, n, cap).reshape(B * n, h2.shape[1])


def _rope_tables(S):
    half, quarter = HEAD_DIM // 2, HEAD_DIM // 4
    t = jnp.arange(S)
    freqs = 1.0 / (ROPE_BASE ** (jnp.arange(quarter, dtype=F32) / quarter))
    d = np.arange(LANES) % HEAD_DIM
    pos = jnp.where(jnp.asarray(d < half)[None, :], (t // GRID_W)[:, None], (t % GRID_W)[:, None]).astype(F32)
    ang = pos * freqs[jnp.asarray(d % quarter)][None, :]
    first = jnp.asarray((d % half) < quarter)[None, :]
    sin = jnp.sin(ang)
    return jnp.cos(ang), jnp.where(first, -sin, 0.0), jnp.where(first, 0.0, sin)


def _bias_pair_blocks(rpb):
    cq = np.arange(GRID_W)
    ck = np.arange(GRID_W)
    cstart = np.clip(cq - WIN_COLS // 2, 0, GRID_W - WIN_COLS)
    valid = (ck[None, :] >= cstart[:, None]) & (ck[None, :] < cstart[:, None] + WIN_COLS)
    col_off = np.clip(ck[None, :] - cq[:, None] + WIN_COLS - 1, 0, 2 * WIN_COLS - 2)
    ro = np.clip(np.arange(N_PAIR_BLOCKS + 1) - 8, 0, 2 * WIN_ROWS - 2)
    T = jnp.where(jnp.asarray(valid)[None, None], rpb[:, ro][:, :, col_off], NEG_INF)
    return jnp.concatenate([T[:, :-1], T[:, 1:]], axis=-1)


def _head_sum_tables():
    lane = np.arange(ATTN_WIDTH)
    bd = (lane[:, None] // HEAD_DIM == np.arange(LANES)[None, :]).astype(np.float32)
    return jnp.asarray(bd, BF16), jnp.asarray(bd.T, BF16)


def kernel(x, c, ctx, c_ctx, w_ada, b_ada, g_norm1, g_norm2, w_in, b_in, g_q, g_k, rpb, w_attn_o, conv_dw_w,
           conv_dw_b, conv_ln_g, conv_ln_b, w_conv_o, sc_w, w_sc_o, w_o, w_router, w_e1, w_e3, w_e2):
    B, S, D = x.shape
    C = ctx.shape[1]
    L = w_ada.shape[0]
    aw3 = 3 * ATTN_WIDTH
    pcw = 2 * CONV_WIDTH + 3 * SC_WIDTH

    cc = jnp.zeros((8, D), F32).at[0:B].set(c).at[B].set(c_ctx)
    mod = _ada(cc, w_ada, b_ada)
    rope_tabs = _rope_tables(S)
    bd, ex = _head_sum_tables()

    xl = x.reshape(B * S, D)
    xc = ctx.reshape(B * C, D)
    moe_l = moe_c = gate_l = gate_c = None

    for l in range(L):
        last = l == L - 1
        ml = lambda j: mod[l, 0:B, j * D:(j + 1) * D].reshape(B, 1, D)
        mc = lambda j: mod[l, B:B + 1, j * D:(j + 1) * D].reshape(1, 1, D)
        g1 = g_norm1[l].reshape(1, D)
        g2 = g_norm2[l].reshape(1, D)
        w_qkv = w_in[l][:, :aw3].astype(BF16)
        b_qkv = b_in[l][:aw3].reshape(1, aw3)
        gq = jnp.tile(g_q[l], N_HEADS).reshape(1, ATTN_WIDTH)
        gk = jnp.tile(g_k[l], N_HEADS).reshape(1, ATTN_WIDTH)
        w_pc = w_in[l][:, aw3:aw3 + pcw].astype(BF16)
        b_pc = b_in[l][aw3:aw3 + pcw].reshape(1, pcw)
        w_pg = w_in[l][:, aw3 + pcw:].astype(BF16)
        b_pg = b_in[l][aw3 + pcw:].reshape(1, 3 * D)
        branch_w = (w_attn_o[l].astype(BF16), w_conv_o[l].astype(BF16), w_sc_o[l].astype(BF16),
                    jnp.zeros((32, CONV_WIDTH), F32).at[:CONV_K].set(conv_dw_w[l]),
                    conv_dw_b[l].reshape(1, -1), conv_ln_g[l].reshape(1, -1), conv_ln_b[l].reshape(1, -1),
                    jnp.zeros((8, SC_WIDTH), F32).at[:SC_K].set(sc_w[l]))
        wo = w_o[l].astype(BF16)
        wr = jnp.zeros((D, LANES), F32).at[:, :N_EXPERTS].set(w_router[l]).astype(BF16)
        we = (w_e1[l].astype(BF16), w_e3[l].astype(BF16), w_e2[l].astype(BF16))

        if moe_c is not None:
            xc, hc = _normmod(xc, g1, mc(0), mc(1), C, moe=moe_c, gate=gate_c)
        else:
            hc = _normmod(xc, g1, mc(0), mc(1), C)
        qc, kc, vc = _qkv(hc, w_qkv, b_qkv, gq, gk, bd, ex, None, C)
        if not last:
            attn_c = _ctx_attn(qc, kc, vc, B, C)
            pc_c = _proj(hc, w_pc, b_pc, C, pcw)
            pg_c = _proj(hc, w_pg, b_pg, C, D)
            merged_c = _branch(attn_c, pc_c, pg_c, *branch_w, C, D)
            xc, hc2, aff_c = _outproj(merged_c, xc, wo, mc(2), g2, mc(3), mc(4), wr, C)
            moe_c = _moe(hc2, aff_c, *we, B, C, B)
            gate_c = mc(5)

        if moe_l is not None:
            xl, h = _normmod(xl, g1, ml(0), ml(1), S, moe=moe_l, gate=gate_l)
        else:
            h = _normmod(xl, g1, ml(0), ml(1), S)
        qn, kn, v, qr, kr = _qkv(h, w_qkv, b_qkv, gq, gk, bd, ex, rope_tabs, S)
        attn = _attn(qn, qr, kr, v, kc, vc, _bias_pair_blocks(rpb[l]), B, S, C)
        pc = _proj(h, w_pc, b_pc, 1024, pcw)
        pg = _proj(h, w_pg, b_pg, 1024, D)
        merged = _branch(attn, pc, pg, *branch_w, S, D)
        xl, h2, aff = _outproj(merged, xl, wo, ml(2), g2, ml(3), ml(4), wr, S)
        moe_l = _moe(h2, aff, *we, B, S, 1)
        gate_l = ml(5)

    return _resid(xl, moe_l, gate_l, S).reshape(B, S, D)
```

```python
import functools

import numpy as np
import jax
import jax.numpy as jnp
from jax import lax
from jax.experimental import pallas as pl
from jax.experimental.pallas import tpu as pltpu

F32 = jnp.float32
BF16 = jnp.bfloat16

GRID_W = 64
N_HEADS = 16
HEAD_DIM = 64
ATTN_WIDTH = N_HEADS * HEAD_DIM
N_HEAD_PAIRS = N_HEADS // 2
WIN_ROWS = 8
WIN_COLS = 16
ROPE_BASE = 10000.0
CONV_WIDTH = 512
CONV_K = 31
SC_WIDTH = 512
SC_K = 3
N_EXPERTS = 16
CAP_FACTOR = 2
EPS = 1e-6
NEG_INF = -1e30

LANES = 128
HALO = 16
ROW_GROUP = 4
KEY_ROWS = 12
N_PAIR_BLOCKS = 26
VMEM_LIMIT = 56 * 1024 * 1024


def _cparams(*sem):
    return pltpu.CompilerParams(dimension_semantics=sem, vmem_limit_bytes=VMEM_LIMIT)


def _dot(a, b):
    return jnp.dot(a, b, preferred_element_type=F32)


def _dot_t(a, b):
    return lax.dot_general(a, b, (((1,), (1,)), ((), ())), preferred_element_type=F32)


def _silu(x):
    return x * jax.nn.sigmoid(x)


def _ada_kernel(cc_ref, w_ref, b_ref, o_ref):
    s = _silu(cc_ref[...]).astype(BF16)
    o_ref[0] = _dot(s, w_ref[0].astype(BF16)) + b_ref[0]


def _ada(cc, w_ada, b_ada):
    L, D, W = w_ada.shape
    tn = 1024
    return pl.pallas_call(
        _ada_kernel,
        grid=(L, W // tn),
        in_specs=[pl.BlockSpec((8, D), lambda l, j: (0, 0)),
                  pl.BlockSpec((1, D, tn), lambda l, j: (l, 0, j)),
                  pl.BlockSpec((1, 1, tn), lambda l, j: (l, 0, j))],
        out_specs=pl.BlockSpec((1, 8, tn), lambda l, j: (l, 0, j)),
        out_shape=jax.ShapeDtypeStruct((L, 8, W), F32),
        compiler_params=_cparams("arbitrary", "arbitrary"),
        name="ada",
    )(cc, w_ada, b_ada.reshape(L, 1, W))


def _mod_spec(nmod, tiles_per_mod, D):
    if nmod == 1:
        return pl.BlockSpec((1, 1, D), lambda i: (0, 0, 0))
    return pl.BlockSpec((1, 1, D), lambda i: (i // tiles_per_mod, 0, 0))


def _normmod_kernel(x_ref, g_ref, sh_ref, sc_ref, h_ref):
    x = x_ref[...]
    ms = jnp.mean(x * x, axis=-1, keepdims=True)
    y = x * lax.rsqrt(ms + EPS) * g_ref[...]
    h_ref[...] = (y * (1.0 + sc_ref[0]) + sh_ref[0]).astype(BF16)


def _normmod(x, g, shift, scale, seq_len):
    rows, D = x.shape
    tm = min(512, seq_len)
    mspec = _mod_spec(shift.shape[0], seq_len // tm, D)
    xspec = pl.BlockSpec((tm, D), lambda i: (i, 0))
    return pl.pallas_call(
        _normmod_kernel, grid=(rows // tm,),
        in_specs=[xspec, pl.BlockSpec((1, D), lambda i: (0, 0)), mspec, mspec],
        out_specs=xspec, out_shape=jax.ShapeDtypeStruct((rows, D), BF16),
        compiler_params=_cparams("arbitrary"), name="normmod",
    )(x, g, shift, scale)


def _qkv_kernel(*refs, rope):
    if rope:
        (h_ref, w_ref, b_ref, gq_ref, gk_ref, bd_ref, ex_ref, cos_ref, sa_ref, sb_ref,
         qn_ref, kn_ref, v_ref, qr_ref, kr_ref) = refs
    else:
        h_ref, w_ref, b_ref, gq_ref, gk_ref, bd_ref, ex_ref, qn_ref, kn_ref, v_ref = refs
    h = h_ref[...]
    aw = ATTN_WIDTH

    def proj(s):
        return _dot(h, w_ref[:, s * aw:(s + 1) * aw]) + b_ref[:, s * aw:(s + 1) * aw]

    def headnorm(a, g_ref):
        ss = _dot((a * a).astype(BF16), bd_ref[...])
        inv = lax.rsqrt(ss * (1.0 / HEAD_DIM) + EPS)
        inv_hi = inv.astype(BF16)
        inv_lo = (inv - inv_hi.astype(F32)).astype(BF16)
        full = _dot(inv_hi, ex_ref[...]) + _dot(inv_lo, ex_ref[...])
        return a * full * g_ref[...]

    def store_plain(xn, out_ref):
        for c in range(N_HEAD_PAIRS):
            out_ref[c] = xn[:, c * LANES:(c + 1) * LANES].astype(BF16)

    def store_rope(xn, out_ref):
        cos, sa, sb = cos_ref[...], sa_ref[...], sb_ref[...]
        for c in range(N_HEAD_PAIRS):
            xc = xn[:, c * LANES:(c + 1) * LANES]
            r = xc * cos + pltpu.roll(xc, LANES - 16, 1) * sa + pltpu.roll(xc, 16, 1) * sb
            out_ref[c] = r.astype(BF16)

    qn = headnorm(proj(0), gq_ref)
    store_plain(qn, qn_ref)
    if rope:
        store_rope(qn, qr_ref)
    kn = headnorm(proj(1), gk_ref)
    store_plain(kn, kn_ref)
    if rope:
        store_rope(kn, kr_ref)
    store_plain(proj(2), v_ref)


def _qkv(h, w, b, gq, gk, bd, ex, rope_tabs, seq_len):
    rows, D = h.shape
    tm = min(512, seq_len)
    rope = rope_tabs is not None
    const = lambda i: (0, 0)
    in_specs = [pl.BlockSpec((tm, D), lambda i: (i, 0)),
                pl.BlockSpec((D, 3 * ATTN_WIDTH), const, pipeline_mode=pl.Buffered(1)),
                pl.BlockSpec((1, 3 * ATTN_WIDTH), const),
                pl.BlockSpec((1, ATTN_WIDTH), const),
                pl.BlockSpec((1, ATTN_WIDTH), const),
                pl.BlockSpec((ATTN_WIDTH, LANES), const),
                pl.BlockSpec((LANES, ATTN_WIDTH), const)]
    ins = [h, w, b, gq, gk, bd, ex]
    n_out = 3
    if rope:
        tiles_per_seq = seq_len // tm
        tspec = pl.BlockSpec((tm, LANES), lambda i: (i % tiles_per_seq, 0))
        in_specs += [tspec, tspec, tspec]
        ins += list(rope_tabs)
        n_out = 5
    ospec = pl.BlockSpec((N_HEAD_PAIRS, tm, LANES), lambda i: (0, i, 0))
    return pl.pallas_call(
        functools.partial(_qkv_kernel, rope=rope),
        grid=(rows // tm,), in_specs=in_specs,
        out_specs=[ospec] * n_out,
        out_shape=[jax.ShapeDtypeStruct((N_HEAD_PAIRS, rows, LANES), BF16)] * n_out,
        compiler_params=_cparams("arbitrary"), name="qkv",
    )(*ins)


def _proj_kernel(h_ref, w_ref, b_ref, o_ref, wb_ref):
    @pl.when(pl.program_id(1) == 0)
    def _():
        wb_ref[...] = w_ref[...].astype(BF16)

    o_ref[...] = (_dot(h_ref[...], wb_ref[...]) + b_ref[...]).astype(o_ref.dtype)


def _proj(h, w, b, col0, ncols):
    rows, D = h.shape
    tm = min(2048, rows)
    tn = 512
    j0 = col0 // tn
    return pl.pallas_call(
        _proj_kernel, grid=(ncols // tn, rows // tm),
        in_specs=[pl.BlockSpec((tm, D), lambda j, i: (i, 0)),
                  pl.BlockSpec((D, tn), lambda j, i: (0, j + j0)),
                  pl.BlockSpec((1, tn), lambda j, i: (0, j + j0))],
        out_specs=pl.BlockSpec((tm, tn), lambda j, i: (i, j)),
        out_shape=jax.ShapeDtypeStruct((rows, ncols), BF16),
        scratch_shapes=[pltpu.VMEM((D, tn), BF16)],
        compiler_params=_cparams("arbitrary", "arbitrary"), name="proj",
    )(h, w, b)


def _attn_kernel(qn_ref, qr_ref, kr_ref, v_ref, kc_ref, vc_ref, fp_ref, o_ref, *, rows):
    scale = HEAD_DIM ** -0.5
    gq = ROW_GROUP * GRID_W
    nk = KEY_ROWS * GRID_W
    lane = lax.broadcasted_iota(jnp.int32, (1, LANES), 1)
    head_mask = (lane < HEAD_DIM, lane >= HEAD_DIM)
    key_row = lax.broadcasted_iota(jnp.int32, (1, nk), 1) // GRID_W
    kc = kc_ref[0]
    vc = vc_ref[0]

    def group(gi, carry):
        r0 = gi * ROW_GROUP
        ws = jnp.clip(r0 - WIN_ROWS // 2, 0, rows - KEY_ROWS)
        tok0 = pl.multiple_of(gi * gq, gq)
        key0 = pl.multiple_of(ws * GRID_W, GRID_W)
        qr = qr_ref[0, pl.ds(tok0, gq), :]
        qn = qn_ref[0, pl.ds(tok0, gq), :]
        kw = kr_ref[0, pl.ds(key0, nk), :]
        vw = v_ref[0, pl.ds(key0, nk), :]
        outs = []
        for hh in range(2):
            qrm = jnp.where(head_mask[hh], qr, 0) * scale
            qnm = jnp.where(head_mask[hh], qn, 0) * scale
            s_loc = _dot_t(qrm, kw)
            s_ctx = _dot_t(qnm, kc)
            slabs = []
            for i in range(ROW_GROUP):
                n0 = ws - r0 - i + (WIN_ROWS - 1) + 8
                bias = jnp.concatenate([fp_ref[hh, n0 + 2 * jp] for jp in range(KEY_ROWS // 2)], axis=1)
                lo = jnp.clip(r0 + i - WIN_ROWS // 2, 0, rows - WIN_ROWS) - ws
                valid = (key_row >= lo) & (key_row < lo + WIN_ROWS)
                slabs.append(jnp.where(valid, s_loc[i * GRID_W:(i + 1) * GRID_W] + bias, NEG_INF))
            s_loc = jnp.concatenate(slabs, axis=0)
            m = jnp.maximum(jnp.max(s_loc, axis=-1, keepdims=True), jnp.max(s_ctx, axis=-1, keepdims=True))
            p_loc = jnp.exp(s_loc - m)
            p_ctx = jnp.exp(s_ctx - m)
            denom = jnp.sum(p_loc, axis=-1, keepdims=True) + jnp.sum(p_ctx, axis=-1, keepdims=True)
            o = _dot(p_loc.astype(BF16), vw) + _dot(p_ctx.astype(BF16), vc)
            outs.append(o / denom)
        o_ref[0, pl.ds(tok0, gq), :] = jnp.where(head_mask[0], outs[0], outs[1]).astype(BF16)
        return carry

    lax.fori_loop(0, rows // ROW_GROUP, group, 0)


def _attn(qn, qr, kr, v, kc, vc, fp, B, S, C):
    rows = S // GRID_W
    tok = pl.BlockSpec((1, S, LANES), lambda b, p: (p, b, 0))
    ctx = pl.BlockSpec((1, C, LANES), lambda b, p: (p, b, 0))
    return pl.pallas_call(
        functools.partial(_attn_kernel, rows=rows),
        grid=(B, N_HEAD_PAIRS),
        in_specs=[tok, tok, tok, tok, ctx, ctx,
                  pl.BlockSpec((2, N_PAIR_BLOCKS, GRID_W, LANES), lambda b, p: (p, 0, 0, 0))],
        out_specs=tok,
        out_shape=jax.ShapeDtypeStruct((N_HEAD_PAIRS, B * S, LANES), BF16),
        compiler_params=_cparams("arbitrary", "arbitrary"), name="attn",
    )(qn, qr, kr, v, kc, vc, fp)


def _ctx_attn_kernel(q_ref, k_ref, v_ref, o_ref):
    scale = HEAD_DIM ** -0.5
    lane = lax.broadcasted_iota(jnp.int32, (1, LANES), 1)
    head_mask = (lane < HEAD_DIM, lane >= HEAD_DIM)
    q, k, v = q_ref[0], k_ref[0], v_ref[0]
    outs = []
    for hh in range(2):
        s = _dot_t(jnp.where(head_mask[hh], q, 0) * scale, k)
        m = jnp.max(s, axis=-1, keepdims=True)
        p = jnp.exp(s - m)
        outs.append(_dot(p.astype(BF16), v) / jnp.sum(p, axis=-1, keepdims=True))
    o_ref[0] = jnp.where(head_mask[0], outs[0], outs[1]).astype(BF16)


def _ctx_attn(q, k, v, B, C):
    spec = pl.BlockSpec((1, C, LANES), lambda b, p: (p, b, 0))
    return pl.pallas_call(
        _ctx_attn_kernel, grid=(B, N_HEAD_PAIRS),
        in_specs=[spec, spec, spec], out_specs=spec,
        out_shape=jax.ShapeDtypeStruct((N_HEAD_PAIRS, B * C, LANES), BF16),
        compiler_params=_cparams("arbitrary", "arbitrary"), name="ctx_attn",
    )(q, k, v)


def _branch_kernel(attn_ref, pr_ref, pcp_ref, pcn_ref, wa_ref, wc_ref, ws_ref,
                   dww_ref, dwb_ref, lng_ref, lnb_ref, scw_ref, o_ref, glu_ext, sc_ext, *, tm, seq_len, D):
    i = pl.program_id(0)
    pos0 = (i * tm) % seq_len
    keep_prev = (pos0 != 0).astype(F32)
    keep_next = (pos0 + tm != seq_len).astype(F32)
    cw, sw = CONV_WIDTH, SC_WIDTH

    def glu(blk):
        return blk[:, 0:cw].astype(F32) * jax.nn.sigmoid(blk[:, cw:2 * cw].astype(F32))

    def gated_x(blk):
        o = 2 * cw + sw
        return blk[:, o:o + sw].astype(F32) * blk[:, o + sw:o + 2 * sw].astype(F32)

    pcw = 2 * cw + 3 * sw
    cur, prv, nxt = pr_ref[:, 0:pcw], pcp_ref[...], pcn_ref[...]
    glu_ext[0:HALO, :] = glu(prv) * keep_prev
    glu_ext[HALO:HALO + tm, :] = glu(cur)
    glu_ext[HALO + tm:2 * HALO + tm, :] = glu(nxt) * keep_next
    sc_ext[0:HALO, :] = gated_x(prv) * keep_prev
    sc_ext[HALO:HALO + tm, :] = gated_x(cur)
    sc_ext[HALO + tm:2 * HALO + tm, :] = gated_x(nxt) * keep_next

    chunk = 64
    conv_rows = []
    for r in range(0, tm, chunk):
        acc = jnp.zeros((chunk, cw), F32)
        for k in range(CONV_K):
            base = HALO - CONV_K // 2 + k + r
            acc = acc + dww_ref[k:k + 1, :] * glu_ext[base:base + chunk, :]
        conv_rows.append(acc)
    u = jnp.concatenate(conv_rows, axis=0) + dwb_ref[...]
    mu = jnp.mean(u, axis=-1, keepdims=True)
    uc = u - mu
    var = jnp.mean(uc * uc, axis=-1, keepdims=True)
    u = _silu(uc * lax.rsqrt(var + EPS) * lng_ref[...] + lnb_ref[...])
    y_b = _dot(u.astype(BF16), wc_ref[...])

    c3 = jnp.zeros((tm, sw), F32)
    for k in range(SC_K):
        base = HALO - SC_K // 2 + k
        c3 = c3 + scw_ref[k:k + 1, :] * sc_ext[base:base + tm, :]
    sc_b = cur[:, 2 * cw:2 * cw + sw].astype(F32)
    y_c = _dot((sc_b * c3).astype(BF16), ws_ref[...])

    attn = jnp.concatenate([attn_ref[c] for c in range(N_HEAD_PAIRS)], axis=1)
    y_a = _dot(attn, wa_ref[...])

    g = pr_ref[:, pcw:pcw + 3 * D]
    merged = (jax.nn.sigmoid(g[:, 0:D].astype(F32)) * y_a
              + jax.nn.sigmoid(g[:, D:2 * D].astype(F32)) * y_b
              + jax.nn.sigmoid(g[:, 2 * D:3 * D].astype(F32)) * y_c)
    o_ref[...] = merged.astype(BF16)


def _branch(attn, pr, wa, wc, ws, dww, dwb, lng, lnb, scw, seq_len, D):
    rows, prw = pr.shape
    tm = 256
    pcw = 2 * CONV_WIDTH + 3 * SC_WIDTH
    nhalo = rows // HALO
    const = lambda i: (0, 0)
    return pl.pallas_call(
        functools.partial(_branch_kernel, tm=tm, seq_len=seq_len, D=D),
        grid=(rows // tm,),
        in_specs=[pl.BlockSpec((N_HEAD_PAIRS, tm, LANES), lambda i: (0, i, 0)),
                  pl.BlockSpec((tm, prw), lambda i: (i, 0)),
                  pl.BlockSpec((HALO, pcw), lambda i: (jnp.maximum(i * (tm // HALO) - 1, 0), 0)),
                  pl.BlockSpec((HALO, pcw), lambda i: (jnp.minimum((i + 1) * (tm // HALO), nhalo - 1), 0)),
                  pl.BlockSpec((ATTN_WIDTH, D), const),
                  pl.BlockSpec((CONV_WIDTH, D), const),
                  pl.BlockSpec((SC_WIDTH, D), const),
                  pl.BlockSpec((32, CONV_WIDTH), const),
                  pl.BlockSpec((1, CONV_WIDTH), const),
                  pl.BlockSpec((1, CONV_WIDTH), const),
                  pl.BlockSpec((1, CONV_WIDTH), const),
                  pl.BlockSpec((8, SC_WIDTH), const)],
        out_specs=pl.BlockSpec((tm, D), lambda i: (i, 0)),
        out_shape=jax.ShapeDtypeStruct((rows, D), BF16),
        scratch_shapes=[pltpu.VMEM((tm + 2 * HALO, CONV_WIDTH), F32),
                        pltpu.VMEM((tm + 2 * HALO, SC_WIDTH), F32)],
        compiler_params=_cparams("arbitrary"), name="branch",
    )(attn, pr, pr, pr, wa, wc, ws, dww, dwb, lng, lnb, scw)


def _outproj_kernel(m_ref, x_ref, wo_ref, gate_ref, g2_ref, sh_ref, sc_ref, wr_ref, xmid_ref, h2_ref, aff_ref):
    x = x_ref[...] + gate_ref[0] * _dot(m_ref[...], wo_ref[...])
    xmid_ref[...] = x
    ms = jnp.mean(x * x, axis=-1, keepdims=True)
    y = x * lax.rsqrt(ms + EPS) * g2_ref[...]
    h2 = y * (1.0 + sc_ref[0]) + sh_ref[0]
    h2_ref[...] = h2
    logits = _dot(h2.astype(BF16), wr_ref[...])
    lane = lax.broadcasted_iota(jnp.int32, logits.shape, 1)
    logits = jnp.where(lane < N_EXPERTS, logits, NEG_INF)
    e = jnp.exp(logits - jnp.max(logits, axis=-1, keepdims=True))
    aff_ref[...] = e / jnp.sum(e, axis=-1, keepdims=True)


def _outproj(merged, x, wo, gate, g2, shift, scale, wr, seq_len):
    rows, D = x.shape
    tm = 256
    mspec = _mod_spec(gate.shape[0], seq_len // tm, D)
    row = lambda i: (i, 0)
    const = lambda i: (0, 0)
    return pl.pallas_call(
        _outproj_kernel, grid=(rows // tm,),
        in_specs=[pl.BlockSpec((tm, D), row), pl.BlockSpec((tm, D), row),
                  pl.BlockSpec((D, D), const), mspec,
                  pl.BlockSpec((1, D), const), mspec, mspec,
                  pl.BlockSpec((D, LANES), const)],
        out_specs=[pl.BlockSpec((tm, D), row), pl.BlockSpec((tm, D), row), pl.BlockSpec((tm, LANES), row)],
        out_shape=[jax.ShapeDtypeStruct((rows, D), F32), jax.ShapeDtypeStruct((rows, D), F32),
                   jax.ShapeDtypeStruct((rows, LANES), F32)],
        compiler_params=_cparams("arbitrary"), name="outproj",
    )(merged, x, wo, gate, g2, shift, scale, wr)


MAX_BISECT = 160
POS_LANE = 3 * N_EXPERTS


def _select_kernel(aff_ref, tri_ref, tv_ref, idx_ref, g_ref, s1_ref, r_ref, *, n, cap, blk, tchunk):
    E = N_EXPERTS
    aff = aff_ref[0]
    a = aff.T[0:E, :]
    capf = float(cap)

    def cond(st):
        it, _, _, ndone = st
        return jnp.logical_and(it < MAX_BISECT, ndone < float(E))

    def in_range(lo, hi):
        return jnp.logical_and(a >= lo, a < hi)

    def body(st):
        it, lo, hi, _ = st
        mid = 0.5 * (lo + hi)
        cnt = jnp.sum((a >= mid).astype(F32), axis=-1, keepdims=True)
        ge = cnt >= capf
        lo = jnp.where(ge, mid, lo)
        hi = jnp.where(ge, hi, mid)
        r = in_range(lo, hi)
        vmin = jnp.min(jnp.where(r, a, 4.0), axis=-1, keepdims=True)
        vmax = jnp.max(jnp.where(r, a, -1.0), axis=-1, keepdims=True)
        ndone = jnp.sum((vmin == vmax).astype(F32))
        return it + 1, lo, hi, ndone

    _, lo, hi, _ = lax.while_loop(cond, body, (jnp.int32(0), jnp.zeros((E, 1), F32), jnp.full((E, 1), 2.0, F32),
                                               jnp.float32(0.0)))
    thr = jnp.max(jnp.where(in_range(lo, hi), a, -1.0), axis=-1, keepdims=True)

    def cumsum(x):
        outs, carry = [], jnp.zeros((E, 1), F32)
        for c in range(n // blk):
            part = _dot(x[:, c * blk:(c + 1) * blk].astype(BF16), tri_ref[...]) + carry
            outs.append(part)
            carry = part[:, blk - 1:blk]
        return jnp.concatenate(outs, axis=1)

    gt = a > thr
    eq = (a == thr).astype(F32)
    need = capf - jnp.sum(gt.astype(F32), axis=-1, keepdims=True)
    eq_rank = cumsum(eq) - eq
    sel = jnp.logical_or(gt, jnp.logical_and(eq > 0.0, eq_rank < need)).astype(F32)
    s1_ref[...] = sel * cumsum(sel)

    a_hi = aff.astype(BF16).astype(F32)
    rest = aff - a_hi
    a_mid = rest.astype(BF16).astype(F32)
    a_lo = rest - a_mid
    r_ref[...] = (a_hi + pltpu.roll(a_mid, E, 1) + pltpu.roll(a_lo, 2 * E, 1) + tv_ref[...]).astype(BF16)

    slot_ids = lax.broadcasted_iota(jnp.int32, (cap, 1), 0).astype(F32) + 1.0
    lane = lax.broadcasted_iota(jnp.int32, (1, LANES), 1)

    def per_expert(e, carry):
        row = s1_ref[pl.ds(e, 1), :]
        acc = jnp.zeros((cap, LANES), F32)
        for c in range(n // tchunk):
            onehot = (row[:, c * tchunk:(c + 1) * tchunk] == slot_ids).astype(BF16)
            acc = acc + _dot(onehot, r_ref[c * tchunk:(c + 1) * tchunk, :])
        mine = jnp.logical_and(lane % E == e, lane < POS_LANE)
        g = jnp.sum(jnp.where(mine, acc, 0.0), axis=1, keepdims=True)
        tok = jnp.sum(jnp.where(lane == POS_LANE, acc * 64.0, jnp.where(lane == POS_LANE + 1, acc, 0.0)),
                      axis=1, keepdims=True)
        g_ref[0, e] = jnp.broadcast_to(g, (cap, LANES))
        idx_ref[0, e] = jnp.broadcast_to(tok, (cap, LANES)).astype(jnp.int32)
        return carry

    lax.fori_loop(0, E, per_expert, 0)


def _select(aff, B, n, cap):
    blk = min(512, n)
    tchunk = min(1024, n)
    tri = (np.arange(blk)[:, None] <= np.arange(blk)[None, :]).astype(np.float32)
    tv = np.zeros((n, LANES), np.float32)
    tv[:, POS_LANE] = np.arange(n) // 64
    tv[:, POS_LANE + 1] = np.arange(n) % 64
    out = pl.BlockSpec((1, N_EXPERTS, cap, LANES), lambda b: (b, 0, 0, 0))
    return pl.pallas_call(
        functools.partial(_select_kernel, n=n, cap=cap, blk=blk, tchunk=tchunk),
        grid=(B,),
        in_specs=[pl.BlockSpec((1, n, LANES), lambda b: (b, 0, 0)),
                  pl.BlockSpec((blk, blk), lambda b: (0, 0)),
                  pl.BlockSpec((n, LANES), lambda b: (0, 0))],
        out_specs=[out, out],
        out_shape=[jax.ShapeDtypeStruct((B, N_EXPERTS, cap, LANES), jnp.int32),
                   jax.ShapeDtypeStruct((B, N_EXPERTS, cap, LANES), F32)],
        scratch_shapes=[pltpu.VMEM((N_EXPERTS, n), F32), pltpu.VMEM((n, LANES), BF16)],
        compiler_params=_cparams("arbitrary"), name="select",
    )(aff.reshape(B, n, LANES), jnp.asarray(tri, BF16), jnp.asarray(tv))


DMA_UNROLL = 8

def _expert_kernel(idx_ref, g_ref, gate_ref, w1_ref, w3_ref, w2_ref, h_hbm, acc_in, acc_hbm,
                   xs_buf, acc_buf, sem, *, bb, cap, n, gate_rows):
    del acc_in
    e = pl.program_id(0)
    b0 = pl.program_id(1) * bb
    rows = bb * cap
    D = xs_buf.shape[1]

    def for_rows(fn):
        for bl in range(bb):
            b = b0 + bl
            base = (b * N_EXPERTS + e) * cap

            def body(s, carry, bl=bl, b=b, base=base):
                fn(bl * cap + s, b * n + idx_ref[base + s])
                return carry

            lax.fori_loop(0, cap, body, 0, unroll=DMA_UNROLL)

    def x_copy(j, r):
        return pltpu.make_async_copy(h_hbm.at[pl.ds(r, 1), :], xs_buf.at[pl.ds(j, 1), :], sem.at[0])

    def acc_load(j, r):
        return pltpu.make_async_copy(acc_hbm.at[pl.ds(r, 1), :], acc_buf.at[pl.ds(j, 1), :], sem.at[1])

    def acc_store(j, r):
        return pltpu.make_async_copy(acc_buf.at[pl.ds(j, 1), :], acc_hbm.at[pl.ds(r, 1), :], sem.at[2])

    def start_loads(j, r):
        x_copy(j, r).start()
        acc_load(j, r).start()

    def wait_loads(j, r):
        x_copy(j, r).wait()
        acc_load(j, r).wait()

    for_rows(start_loads)
    for_rows(wait_loads)

    x = xs_buf[...].astype(BF16)
    hidden = (_silu(_dot(x, w1_ref[0])) * _dot(x, w3_ref[0])).astype(BF16)
    y = _dot(hidden, w2_ref[0])
    g = g_ref[...].reshape(rows, LANES)
    y = y * jnp.concatenate([g] * (D // LANES), axis=1)
    if bb == 1:
        y = y * gate_ref[0]
    else:
        y = y * jnp.concatenate([jnp.broadcast_to(gate_ref[bl if gate_rows > 1 else 0], (cap, D))
                                 for bl in range(bb)], axis=0)
    acc_buf[...] = acc_buf[...] + y

    for_rows(lambda j, r: acc_store(j, r).start())
    for_rows(lambda j, r: acc_store(j, r).wait())


def _experts(idx, g, gate, w1, w3, w2, h2, x_mid, B, n, cap, bb):
    E, D, FF = w1.shape
    rows = bb * cap
    gate_rows = bb if gate.shape[0] > 1 else 1
    gate_spec = (pl.BlockSpec((bb, 1, D), lambda e, b, idx: (b, 0, 0)) if gate.shape[0] > 1
                 else pl.BlockSpec((1, 1, D), lambda e, b, idx: (0, 0, 0)))
    grid_spec = pltpu.PrefetchScalarGridSpec(
        num_scalar_prefetch=1,
        grid=(E, B // bb),
        in_specs=[pl.BlockSpec((bb, 1, cap, LANES), lambda e, b, idx: (b, e, 0, 0)),
                  gate_spec,
                  pl.BlockSpec((1, D, FF), lambda e, b, idx: (e, 0, 0)),
                  pl.BlockSpec((1, D, FF), lambda e, b, idx: (e, 0, 0)),
                  pl.BlockSpec((1, FF, D), lambda e, b, idx: (e, 0, 0)),
                  pl.BlockSpec(memory_space=pl.ANY),
                  pl.BlockSpec(memory_space=pl.ANY)],
        out_specs=pl.BlockSpec(memory_space=pl.ANY),
        scratch_shapes=[pltpu.VMEM((rows, D), F32), pltpu.VMEM((rows, D), F32),
                        pltpu.SemaphoreType.DMA((3,))])
    return pl.pallas_call(
        functools.partial(_expert_kernel, bb=bb, cap=cap, n=n, gate_rows=gate_rows),
        grid_spec=grid_spec,
        out_shape=jax.ShapeDtypeStruct(x_mid.shape, F32),
        input_output_aliases={7: 0},
        compiler_params=_cparams("arbitrary", "arbitrary"), name="experts",
    )(idx, g, gate, w1, w3, w2, h2, x_mid)


def _moe(h2, aff, x_mid, gate, w1, w3, w2, B, n, bb):
    cap = max(1, CAP_FACTOR * n // N_EXPERTS)
    idx, g = _select(aff, B, n, cap)
    return _experts(idx[..., 0].reshape(-1), g, gate, w1, w3, w2, h2, x_mid, B, n, cap, bb)


def _rope_tables(S):
    half, quarter = HEAD_DIM // 2, HEAD_DIM // 4
    t = jnp.arange(S)
    freqs = 1.0 / (ROPE_BASE ** (jnp.arange(quarter, dtype=F32) / quarter))
    d = np.arange(LANES) % HEAD_DIM
    pos = jnp.where(jnp.asarray(d < half)[None, :], (t // GRID_W)[:, None], (t % GRID_W)[:, None]).astype(F32)
    ang = pos * freqs[jnp.asarray(d % quarter)][None, :]
    first = jnp.asarray((d % half) < quarter)[None, :]
    sin = jnp.sin(ang)
    return jnp.cos(ang), jnp.where(first, -sin, 0.0), jnp.where(first, 0.0, sin)


def _bias_pair_blocks(rpb):
    cq = np.arange(GRID_W)
    ck = np.arange(GRID_W)
    cstart = np.clip(cq - WIN_COLS // 2, 0, GRID_W - WIN_COLS)
    valid = (ck[None, :] >= cstart[:, None]) & (ck[None, :] < cstart[:, None] + WIN_COLS)
    col_off = np.clip(ck[None, :] - cq[:, None] + WIN_COLS - 1, 0, 2 * WIN_COLS - 2)
    ro = np.clip(np.arange(N_PAIR_BLOCKS + 1) - 8, 0, 2 * WIN_ROWS - 2)
    T = jnp.where(jnp.asarray(valid)[None, None], rpb[:, ro][:, :, col_off], NEG_INF)
    return jnp.concatenate([T[:, :-1], T[:, 1:]], axis=-1)


def _head_sum_tables():
    lane = np.arange(ATTN_WIDTH)
    bd = (lane[:, None] // HEAD_DIM == np.arange(LANES)[None, :]).astype(np.float32)
    return jnp.asarray(bd, BF16), jnp.asarray(bd.T, BF16)


def kernel(x, c, ctx, c_ctx, w_ada, b_ada, g_norm1, g_norm2, w_in, b_in, g_q, g_k, rpb, w_attn_o, conv_dw_w,
           conv_dw_b, conv_ln_g, conv_ln_b, w_conv_o, sc_w, w_sc_o, w_o, w_router, w_e1, w_e3, w_e2):
    B, S, D = x.shape
    C = ctx.shape[1]
    L = w_ada.shape[0]
    aw3 = 3 * ATTN_WIDTH
    prw = 2 * CONV_WIDTH + 3 * SC_WIDTH + 3 * D

    cc = jnp.zeros((8, D), F32).at[0:B].set(c).at[B].set(c_ctx)
    mod = _ada(cc, w_ada, b_ada)
    rope_tabs = _rope_tables(S)
    bd, ex = _head_sum_tables()

    xl = x.reshape(B * S, D)
    xc = ctx.reshape(B * C, D)

    for l in range(L):
        last = l == L - 1
        ml = lambda j: mod[l, 0:B, j * D:(j + 1) * D].reshape(B, 1, D)
        mc = lambda j: mod[l, B:B + 1, j * D:(j + 1) * D].reshape(1, 1, D)
        g1 = g_norm1[l].reshape(1, D)
        g2 = g_norm2[l].reshape(1, D)
        w_qkv = w_in[l][:, :aw3].astype(BF16)
        b_all = b_in[l].reshape(1, -1)
        b_qkv = b_all[:, :aw3]
        gq = jnp.tile(g_q[l], N_HEADS).reshape(1, ATTN_WIDTH)
        gk = jnp.tile(g_k[l], N_HEADS).reshape(1, ATTN_WIDTH)
        branch_w = (w_attn_o[l].astype(BF16), w_conv_o[l].astype(BF16), w_sc_o[l].astype(BF16),
                    jnp.zeros((32, CONV_WIDTH), F32).at[:CONV_K].set(conv_dw_w[l]),
                    conv_dw_b[l].reshape(1, -1), conv_ln_g[l].reshape(1, -1), conv_ln_b[l].reshape(1, -1),
                    jnp.zeros((8, SC_WIDTH), F32).at[:SC_K].set(sc_w[l]))
        wo = w_o[l].astype(BF16)
        wr = jnp.zeros((D, LANES), F32).at[:, :N_EXPERTS].set(w_router[l]).astype(BF16)
        we = (w_e1[l].astype(BF16), w_e3[l].astype(BF16), w_e2[l].astype(BF16))

        hc = _normmod(xc, g1, mc(0), mc(1), C)
        qc, kc, vc = _qkv(hc, w_qkv, b_qkv, gq, gk, bd, ex, None, C)
        if not last:
            attn_c = _ctx_attn(qc, kc, vc, B, C)
            pr_c = _proj(hc, w_in[l], b_all, aw3, prw)
            merged_c = _branch(attn_c, pr_c, *branch_w, C, D)
            xc_mid, hc2, aff_c = _outproj(merged_c, xc, wo, mc(2), g2, mc(3), mc(4), wr, C)
            xc = _moe(hc2, aff_c, xc_mid, mc(5), *we, B, C, B)

        h = _normmod(xl, g1, ml(0), ml(1), S)
        qn, kn, v, qr, kr = _qkv(h, w_qkv, b_qkv, gq, gk, bd, ex, rope_tabs, S)
        attn = _attn(qn, qr, kr, v, kc, vc, _bias_pair_blocks(rpb[l]), B, S, C)
        pr = _proj(h, w_in[l], b_all, aw3, prw)
        merged = _branch(attn, pr, *branch_w, S, D)
        x_mid, h2, aff = _outproj(merged, xl, wo, ml(2), g2, ml(3), ml(4), wr, S)
        xl = _moe(h2, aff, x_mid, ml(5), *we, B, S, 1)

    return xl.reshape(B, S, D)
```

```python
import functools

import numpy as np
import jax
import jax.numpy as jnp
from jax import lax
from jax.experimental import pallas as pl
from jax.experimental.pallas import tpu as pltpu

F32 = jnp.float32
BF16 = jnp.bfloat16

GRID_W = 64
N_HEADS = 16
HEAD_DIM = 64
ATTN_WIDTH = N_HEADS * HEAD_DIM
N_HEAD_PAIRS = N_HEADS // 2
WIN_ROWS = 8
WIN_COLS = 16
ROPE_BASE = 10000.0
CONV_WIDTH = 512
CONV_K = 31
SC_WIDTH = 512
SC_K = 3
N_EXPERTS = 16
CAP_FACTOR = 2
EPS = 1e-6
NEG_INF = -1e30

LANES = 128
HALO = 16
ROW_GROUP = 4
KEY_ROWS = 12
N_PAIR_BLOCKS = 26
VMEM_LIMIT = 56 * 1024 * 1024


def _cparams(*sem):
    return pltpu.CompilerParams(dimension_semantics=sem, vmem_limit_bytes=VMEM_LIMIT)


def _dot(a, b):
    return jnp.dot(a, b, preferred_element_type=F32)


def _dot_t(a, b):
    return lax.dot_general(a, b, (((1,), (1,)), ((), ())), preferred_element_type=F32)


def _silu(x):
    return x * jax.nn.sigmoid(x)


def _ada_kernel(cc_ref, w_ref, b_ref, o_ref):
    s = _silu(cc_ref[...]).astype(BF16)
    o_ref[0] = _dot(s, w_ref[0].astype(BF16)) + b_ref[0]


def _ada(cc, w_ada, b_ada):
    L, D, W = w_ada.shape
    tn = 1024
    return pl.pallas_call(
        _ada_kernel,
        grid=(L, W // tn),
        in_specs=[pl.BlockSpec((8, D), lambda l, j: (0, 0)),
                  pl.BlockSpec((1, D, tn), lambda l, j: (l, 0, j)),
                  pl.BlockSpec((1, 1, tn), lambda l, j: (l, 0, j))],
        out_specs=pl.BlockSpec((1, 8, tn), lambda l, j: (l, 0, j)),
        out_shape=jax.ShapeDtypeStruct((L, 8, W), F32),
        compiler_params=_cparams("arbitrary", "arbitrary"),
        name="ada",
    )(cc, w_ada, b_ada.reshape(L, 1, W))


def _mod_spec(nmod, tiles_per_mod, D):
    if nmod == 1:
        return pl.BlockSpec((1, 1, D), lambda i: (0, 0, 0))
    return pl.BlockSpec((1, 1, D), lambda i: (i // tiles_per_mod, 0, 0))


def _normmod_kernel(x_ref, g_ref, sh_ref, sc_ref, h_ref):
    x = x_ref[...]
    ms = jnp.mean(x * x, axis=-1, keepdims=True)
    y = x * lax.rsqrt(ms + EPS) * g_ref[...]
    h_ref[...] = (y * (1.0 + sc_ref[0]) + sh_ref[0]).astype(BF16)


def _normmod(x, g, shift, scale, seq_len):
    rows, D = x.shape
    tm = min(512, seq_len)
    mspec = _mod_spec(shift.shape[0], seq_len // tm, D)
    xspec = pl.BlockSpec((tm, D), lambda i: (i, 0))
    return pl.pallas_call(
        _normmod_kernel, grid=(rows // tm,),
        in_specs=[xspec, pl.BlockSpec((1, D), lambda i: (0, 0)), mspec, mspec],
        out_specs=xspec, out_shape=jax.ShapeDtypeStruct((rows, D), BF16),
        compiler_params=_cparams("arbitrary"), name="normmod",
    )(x, g, shift, scale)


def _qkv_kernel(*refs, rope):
    if rope:
        (h_ref, w_ref, b_ref, gq_ref, gk_ref, bd_ref, ex_ref, cos_ref, sa_ref, sb_ref,
         qn_ref, kn_ref, v_ref, qr_ref, kr_ref) = refs
    else:
        h_ref, w_ref, b_ref, gq_ref, gk_ref, bd_ref, ex_ref, qn_ref, kn_ref, v_ref = refs
    h = h_ref[...]
    aw = ATTN_WIDTH

    def proj(s):
        return _dot(h, w_ref[:, s * aw:(s + 1) * aw]) + b_ref[:, s * aw:(s + 1) * aw]

    def headnorm(a, g_ref):
        ss = _dot((a * a).astype(BF16), bd_ref[...])
        inv = lax.rsqrt(ss * (1.0 / HEAD_DIM) + EPS)
        inv_hi = inv.astype(BF16)
        inv_lo = (inv - inv_hi.astype(F32)).astype(BF16)
        full = _dot(inv_hi, ex_ref[...]) + _dot(inv_lo, ex_ref[...])
        return a * full * g_ref[...]

    def store_plain(xn, out_ref):
        for c in range(N_HEAD_PAIRS):
            out_ref[c] = xn[:, c * LANES:(c + 1) * LANES].astype(BF16)

    def store_rope(xn, out_ref):
        cos, sa, sb = cos_ref[...], sa_ref[...], sb_ref[...]
        for c in range(N_HEAD_PAIRS):
            xc = xn[:, c * LANES:(c + 1) * LANES]
            r = xc * cos + pltpu.roll(xc, LANES - 16, 1) * sa + pltpu.roll(xc, 16, 1) * sb
            out_ref[c] = r.astype(BF16)

    qn = headnorm(proj(0), gq_ref)
    store_plain(qn, qn_ref)
    if rope:
        store_rope(qn, qr_ref)
    kn = headnorm(proj(1), gk_ref)
    store_plain(kn, kn_ref)
    if rope:
        store_rope(kn, kr_ref)
    store_plain(proj(2), v_ref)


def _qkv(h, w, b, gq, gk, bd, ex, rope_tabs, seq_len):
    rows, D = h.shape
    tm = min(512, seq_len)
    rope = rope_tabs is not None
    const = lambda i: (0, 0)
    in_specs = [pl.BlockSpec((tm, D), lambda i: (i, 0)),
                pl.BlockSpec((D, 3 * ATTN_WIDTH), const, pipeline_mode=pl.Buffered(1)),
                pl.BlockSpec((1, 3 * ATTN_WIDTH), const),
                pl.BlockSpec((1, ATTN_WIDTH), const),
                pl.BlockSpec((1, ATTN_WIDTH), const),
                pl.BlockSpec((ATTN_WIDTH, LANES), const),
                pl.BlockSpec((LANES, ATTN_WIDTH), const)]
    ins = [h, w, b, gq, gk, bd, ex]
    n_out = 3
    if rope:
        tiles_per_seq = seq_len // tm
        tspec = pl.BlockSpec((tm, LANES), lambda i: (i % tiles_per_seq, 0))
        in_specs += [tspec, tspec, tspec]
        ins += list(rope_tabs)
        n_out = 5
    ospec = pl.BlockSpec((N_HEAD_PAIRS, tm, LANES), lambda i: (0, i, 0))
    return pl.pallas_call(
        functools.partial(_qkv_kernel, rope=rope),
        grid=(rows // tm,), in_specs=in_specs,
        out_specs=[ospec] * n_out,
        out_shape=[jax.ShapeDtypeStruct((N_HEAD_PAIRS, rows, LANES), BF16)] * n_out,
        compiler_params=_cparams("arbitrary"), name="qkv",
    )(*ins)


def _proj_kernel(h_ref, w_ref, b_ref, o_ref, wb_ref):
    @pl.when(pl.program_id(1) == 0)
    def _():
        wb_ref[...] = w_ref[0].astype(BF16)

    o_ref[...] = (_dot(h_ref[...], wb_ref[...]) + b_ref[0]).astype(o_ref.dtype)


def _proj(h, w, b, l, col0, ncols):
    rows, D = h.shape
    tm = min(2048, rows)
    tn = 512
    j0 = col0 // tn
    return pl.pallas_call(
        _proj_kernel, grid=(ncols // tn, rows // tm),
        in_specs=[pl.BlockSpec((tm, D), lambda j, i: (i, 0)),
                  pl.BlockSpec((1, D, tn), lambda j, i: (l, 0, j + j0)),
                  pl.BlockSpec((1, 1, tn), lambda j, i: (l, 0, j + j0))],
        out_specs=pl.BlockSpec((tm, tn), lambda j, i: (i, j)),
        out_shape=jax.ShapeDtypeStruct((rows, ncols), BF16),
        scratch_shapes=[pltpu.VMEM((D, tn), BF16)],
        compiler_params=_cparams("arbitrary", "arbitrary"), name="proj",
    )(h, w, b)


def _attn_kernel(qn_ref, qr_ref, kr_ref, v_ref, kc_ref, vc_ref, fp_ref, o_ref, *, rows):
    scale = HEAD_DIM ** -0.5
    gq = ROW_GROUP * GRID_W
    nk = KEY_ROWS * GRID_W
    lane = lax.broadcasted_iota(jnp.int32, (1, LANES), 1)
    head_mask = (lane < HEAD_DIM, lane >= HEAD_DIM)
    key_row = lax.broadcasted_iota(jnp.int32, (1, nk), 1) // GRID_W
    kc = kc_ref[0]
    vc = vc_ref[0]

    def group(gi, carry):
        r0 = gi * ROW_GROUP
        ws = jnp.clip(r0 - WIN_ROWS // 2, 0, rows - KEY_ROWS)
        tok0 = pl.multiple_of(gi * gq, gq)
        key0 = pl.multiple_of(ws * GRID_W, GRID_W)
        qr = qr_ref[0, pl.ds(tok0, gq), :]
        qn = qn_ref[0, pl.ds(tok0, gq), :]
        kw = kr_ref[0, pl.ds(key0, nk), :]
        vw = v_ref[0, pl.ds(key0, nk), :]
        outs = []
        for hh in range(2):
            qrm = jnp.where(head_mask[hh], qr, 0) * scale
            qnm = jnp.where(head_mask[hh], qn, 0) * scale
            s_loc = _dot_t(qrm, kw)
            s_ctx = _dot_t(qnm, kc)
            slabs = []
            for i in range(ROW_GROUP):
                n0 = ws - r0 - i + (WIN_ROWS - 1) + 8
                bias = jnp.concatenate([fp_ref[hh, n0 + 2 * jp] for jp in range(KEY_ROWS // 2)], axis=1)
                lo = jnp.clip(r0 + i - WIN_ROWS // 2, 0, rows - WIN_ROWS) - ws
                valid = (key_row >= lo) & (key_row < lo + WIN_ROWS)
                slabs.append(jnp.where(valid, s_loc[i * GRID_W:(i + 1) * GRID_W] + bias, NEG_INF))
            s_loc = jnp.concatenate(slabs, axis=0)
            m = jnp.maximum(jnp.max(s_loc, axis=-1, keepdims=True), jnp.max(s_ctx, axis=-1, keepdims=True))
            p_loc = jnp.exp(s_loc - m)
            p_ctx = jnp.exp(s_ctx - m)
            denom = jnp.sum(p_loc, axis=-1, keepdims=True) + jnp.sum(p_ctx, axis=-1, keepdims=True)
            o = _dot(p_loc.astype(BF16), vw) + _dot(p_ctx.astype(BF16), vc)
            outs.append(o / denom)
        o_ref[0, pl.ds(tok0, gq), :] = jnp.where(head_mask[0], outs[0], outs[1]).astype(BF16)
        return carry

    lax.fori_loop(0, rows // ROW_GROUP, group, 0)


def _attn(qn, qr, kr, v, kc, vc, fp, B, S, C):
    rows = S // GRID_W
    tok = pl.BlockSpec((1, S, LANES), lambda b, p: (p, b, 0))
    ctx = pl.BlockSpec((1, C, LANES), lambda b, p: (p, b, 0))
    return pl.pallas_call(
        functools.partial(_attn_kernel, rows=rows),
        grid=(B, N_HEAD_PAIRS),
        in_specs=[tok, tok, tok, tok, ctx, ctx,
                  pl.BlockSpec((2, N_PAIR_BLOCKS, GRID_W, LANES), lambda b, p: (p, 0, 0, 0))],
        out_specs=tok,
        out_shape=jax.ShapeDtypeStruct((N_HEAD_PAIRS, B * S, LANES), BF16),
        compiler_params=_cparams("arbitrary", "arbitrary"), name="attn",
    )(qn, qr, kr, v, kc, vc, fp)


def _ctx_attn_kernel(q_ref, k_ref, v_ref, o_ref):
    scale = HEAD_DIM ** -0.5
    lane = lax.broadcasted_iota(jnp.int32, (1, LANES), 1)
    head_mask = (lane < HEAD_DIM, lane >= HEAD_DIM)
    q, k, v = q_ref[0], k_ref[0], v_ref[0]
    outs = []
    for hh in range(2):
        s = _dot_t(jnp.where(head_mask[hh], q, 0) * scale, k)
        m = jnp.max(s, axis=-1, keepdims=True)
        p = jnp.exp(s - m)
        outs.append(_dot(p.astype(BF16), v) / jnp.sum(p, axis=-1, keepdims=True))
    o_ref[0] = jnp.where(head_mask[0], outs[0], outs[1]).astype(BF16)


def _ctx_attn(q, k, v, B, C):
    spec = pl.BlockSpec((1, C, LANES), lambda b, p: (p, b, 0))
    return pl.pallas_call(
        _ctx_attn_kernel, grid=(B, N_HEAD_PAIRS),
        in_specs=[spec, spec, spec], out_specs=spec,
        out_shape=jax.ShapeDtypeStruct((N_HEAD_PAIRS, B * C, LANES), BF16),
        compiler_params=_cparams("arbitrary", "arbitrary"), name="ctx_attn",
    )(q, k, v)


def _branch_kernel(attn_ref, pr_ref, pcp_ref, pcn_ref, wa_ref, wc_ref, ws_ref,
                   dww_ref, dwb_ref, lng_ref, lnb_ref, scw_ref, o_ref, glu_ext, sc_ext, *, tm, seq_len, D):
    i = pl.program_id(0)
    pos0 = (i * tm) % seq_len
    keep_prev = (pos0 != 0).astype(F32)
    keep_next = (pos0 + tm != seq_len).astype(F32)
    cw, sw = CONV_WIDTH, SC_WIDTH

    def glu(blk):
        return blk[:, 0:cw].astype(F32) * jax.nn.sigmoid(blk[:, cw:2 * cw].astype(F32))

    def gated_x(blk):
        o = 2 * cw + sw
        return blk[:, o:o + sw].astype(F32) * blk[:, o + sw:o + 2 * sw].astype(F32)

    pcw = 2 * cw + 3 * sw
    cur, prv, nxt = pr_ref[:, 0:pcw], pcp_ref[...], pcn_ref[...]
    glu_ext[0:HALO, :] = glu(prv) * keep_prev
    glu_ext[HALO:HALO + tm, :] = glu(cur)
    glu_ext[HALO + tm:2 * HALO + tm, :] = glu(nxt) * keep_next
    sc_ext[0:HALO, :] = gated_x(prv) * keep_prev
    sc_ext[HALO:HALO + tm, :] = gated_x(cur)
    sc_ext[HALO + tm:2 * HALO + tm, :] = gated_x(nxt) * keep_next

    chunk = 64
    conv_rows = []
    for r in range(0, tm, chunk):
        acc = jnp.zeros((chunk, cw), F32)
        for k in range(CONV_K):
            base = HALO - CONV_K // 2 + k + r
            acc = acc + dww_ref[k:k + 1, :] * glu_ext[base:base + chunk, :]
        conv_rows.append(acc)
    u = jnp.concatenate(conv_rows, axis=0) + dwb_ref[...]
    mu = jnp.mean(u, axis=-1, keepdims=True)
    uc = u - mu
    var = jnp.mean(uc * uc, axis=-1, keepdims=True)
    u = _silu(uc * lax.rsqrt(var + EPS) * lng_ref[...] + lnb_ref[...])
    y_b = _dot(u.astype(BF16), wc_ref[...])

    c3 = jnp.zeros((tm, sw), F32)
    for k in range(SC_K):
        base = HALO - SC_K // 2 + k
        c3 = c3 + scw_ref[k:k + 1, :] * sc_ext[base:base + tm, :]
    sc_b = cur[:, 2 * cw:2 * cw + sw].astype(F32)
    y_c = _dot((sc_b * c3).astype(BF16), ws_ref[...])

    attn = jnp.concatenate([attn_ref[c] for c in range(N_HEAD_PAIRS)], axis=1)
    y_a = _dot(attn, wa_ref[...])

    g = pr_ref[:, pcw:pcw + 3 * D]
    merged = (jax.nn.sigmoid(g[:, 0:D].astype(F32)) * y_a
              + jax.nn.sigmoid(g[:, D:2 * D].astype(F32)) * y_b
              + jax.nn.sigmoid(g[:, 2 * D:3 * D].astype(F32)) * y_c)
    o_ref[...] = merged.astype(BF16)


def _branch(attn, pr, wa, wc, ws, dww, dwb, lng, lnb, scw, seq_len, D):
    rows, prw = pr.shape
    tm = 256
    pcw = 2 * CONV_WIDTH + 3 * SC_WIDTH
    nhalo = rows // HALO
    const = lambda i: (0, 0)
    return pl.pallas_call(
        functools.partial(_branch_kernel, tm=tm, seq_len=seq_len, D=D),
        grid=(rows // tm,),
        in_specs=[pl.BlockSpec((N_HEAD_PAIRS, tm, LANES), lambda i: (0, i, 0)),
                  pl.BlockSpec((tm, prw), lambda i: (i, 0)),
                  pl.BlockSpec((HALO, pcw), lambda i: (jnp.maximum(i * (tm // HALO) - 1, 0), 0)),
                  pl.BlockSpec((HALO, pcw), lambda i: (jnp.minimum((i + 1) * (tm // HALO), nhalo - 1), 0)),
                  pl.BlockSpec((ATTN_WIDTH, D), const),
                  pl.BlockSpec((CONV_WIDTH, D), const),
                  pl.BlockSpec((SC_WIDTH, D), const),
                  pl.BlockSpec((32, CONV_WIDTH), const),
                  pl.BlockSpec((1, CONV_WIDTH), const),
                  pl.BlockSpec((1, CONV_WIDTH), const),
                  pl.BlockSpec((1, CONV_WIDTH), const),
                  pl.BlockSpec((8, SC_WIDTH), const)],
        out_specs=pl.BlockSpec((tm, D), lambda i: (i, 0)),
        out_shape=jax.ShapeDtypeStruct((rows, D), BF16),
        scratch_shapes=[pltpu.VMEM((tm + 2 * HALO, CONV_WIDTH), F32),
                        pltpu.VMEM((tm + 2 * HALO, SC_WIDTH), F32)],
        compiler_params=_cparams("arbitrary"), name="branch",
    )(attn, pr, pr, pr, wa, wc, ws, dww, dwb, lng, lnb, scw)


def _outproj_kernel(m_ref, x_ref, wo_ref, gate_ref, g2_ref, sh_ref, sc_ref, wr_ref, xmid_ref, h2_ref, aff_ref):
    x = x_ref[...] + gate_ref[0] * _dot(m_ref[...], wo_ref[...])
    xmid_ref[...] = x
    ms = jnp.mean(x * x, axis=-1, keepdims=True)
    y = x * lax.rsqrt(ms + EPS) * g2_ref[...]
    h2 = y * (1.0 + sc_ref[0]) + sh_ref[0]
    h2_ref[...] = h2
    logits = _dot(h2.astype(BF16), wr_ref[...])
    lane = lax.broadcasted_iota(jnp.int32, logits.shape, 1)
    logits = jnp.where(lane < N_EXPERTS, logits, NEG_INF)
    e = jnp.exp(logits - jnp.max(logits, axis=-1, keepdims=True))
    aff_ref[...] = e / jnp.sum(e, axis=-1, keepdims=True)


def _outproj(merged, x, wo, gate, g2, shift, scale, wr, seq_len):
    rows, D = x.shape
    tm = 256
    mspec = _mod_spec(gate.shape[0], seq_len // tm, D)
    row = lambda i: (i, 0)
    const = lambda i: (0, 0)
    return pl.pallas_call(
        _outproj_kernel, grid=(rows // tm,),
        in_specs=[pl.BlockSpec((tm, D), row), pl.BlockSpec((tm, D), row),
                  pl.BlockSpec((D, D), const), mspec,
                  pl.BlockSpec((1, D), const), mspec, mspec,
                  pl.BlockSpec((D, LANES), const)],
        out_specs=[pl.BlockSpec((tm, D), row), pl.BlockSpec((tm, D), row), pl.BlockSpec((tm, LANES), row)],
        out_shape=[jax.ShapeDtypeStruct((rows, D), F32), jax.ShapeDtypeStruct((rows, D), F32),
                   jax.ShapeDtypeStruct((rows, LANES), F32)],
        compiler_params=_cparams("arbitrary"), name="outproj",
    )(merged, x, wo, gate, g2, shift, scale, wr)


MAX_BISECT = 160
POS_LANE = 3 * N_EXPERTS


def _select_kernel(aff_ref, tri_ref, tv_ref, idx_ref, g_ref, s1_ref, r_ref, *, n, cap, blk, tchunk):
    E = N_EXPERTS
    aff = aff_ref[0]
    a = aff.T[0:E, :]
    capf = float(cap)

    def cond(st):
        it, _, _, ndone = st
        return jnp.logical_and(it < MAX_BISECT, ndone < float(E))

    def in_range(lo, hi):
        return jnp.logical_and(a >= lo, a < hi)

    def body(st):
        it, lo, hi, _ = st
        mid = 0.5 * (lo + hi)
        cnt = jnp.sum((a >= mid).astype(F32), axis=-1, keepdims=True)
        ge = cnt >= capf
        lo = jnp.where(ge, mid, lo)
        hi = jnp.where(ge, hi, mid)
        r = in_range(lo, hi)
        vmin = jnp.min(jnp.where(r, a, 4.0), axis=-1, keepdims=True)
        vmax = jnp.max(jnp.where(r, a, -1.0), axis=-1, keepdims=True)
        ndone = jnp.sum((vmin == vmax).astype(F32))
        return it + 1, lo, hi, ndone

    _, lo, hi, _ = lax.while_loop(cond, body, (jnp.int32(0), jnp.zeros((E, 1), F32), jnp.full((E, 1), 2.0, F32),
                                               jnp.float32(0.0)))
    thr = jnp.max(jnp.where(in_range(lo, hi), a, -1.0), axis=-1, keepdims=True)

    def cumsum(x):
        outs, carry = [], jnp.zeros((E, 1), F32)
        for c in range(n // blk):
            part = _dot(x[:, c * blk:(c + 1) * blk].astype(BF16), tri_ref[...]) + carry
            outs.append(part)
            carry = part[:, blk - 1:blk]
        return jnp.concatenate(outs, axis=1)

    gt = a > thr
    eq = (a == thr).astype(F32)
    need = capf - jnp.sum(gt.astype(F32), axis=-1, keepdims=True)
    eq_rank = cumsum(eq) - eq
    sel = jnp.logical_or(gt, jnp.logical_and(eq > 0.0, eq_rank < need)).astype(F32)
    s1_ref[...] = sel * cumsum(sel)

    a_hi = aff.astype(BF16).astype(F32)
    rest = aff - a_hi
    a_mid = rest.astype(BF16).astype(F32)
    a_lo = rest - a_mid
    r_ref[...] = (a_hi + pltpu.roll(a_mid, E, 1) + pltpu.roll(a_lo, 2 * E, 1) + tv_ref[...]).astype(BF16)

    slot_ids = lax.broadcasted_iota(jnp.int32, (cap, 1), 0).astype(F32) + 1.0
    lane = lax.broadcasted_iota(jnp.int32, (1, LANES), 1)

    def per_expert(e, carry):
        row = s1_ref[pl.ds(e, 1), :]
        acc = jnp.zeros((cap, LANES), F32)
        for c in range(n // tchunk):
            onehot = (row[:, c * tchunk:(c + 1) * tchunk] == slot_ids).astype(BF16)
            acc = acc + _dot(onehot, r_ref[c * tchunk:(c + 1) * tchunk, :])
        mine = jnp.logical_and(lane % E == e, lane < POS_LANE)
        g = jnp.sum(jnp.where(mine, acc, 0.0), axis=1, keepdims=True)
        tok = jnp.sum(jnp.where(lane == POS_LANE, acc * 64.0, jnp.where(lane == POS_LANE + 1, acc, 0.0)),
                      axis=1, keepdims=True)
        g_ref[0, e] = jnp.broadcast_to(g, (cap, LANES))
        idx_ref[0, e] = jnp.broadcast_to(tok, (cap, LANES)).astype(jnp.int32)
        return carry

    lax.fori_loop(0, E, per_expert, 0)


def _select(aff, B, n, cap):
    blk = min(512, n)
    tchunk = min(1024, n)
    tri = (np.arange(blk)[:, None] <= np.arange(blk)[None, :]).astype(np.float32)
    tv = np.zeros((n, LANES), np.float32)
    tv[:, POS_LANE] = np.arange(n) // 64
    tv[:, POS_LANE + 1] = np.arange(n) % 64
    out = pl.BlockSpec((1, N_EXPERTS, cap, LANES), lambda b: (b, 0, 0, 0))
    return pl.pallas_call(
        functools.partial(_select_kernel, n=n, cap=cap, blk=blk, tchunk=tchunk),
        grid=(B,),
        in_specs=[pl.BlockSpec((1, n, LANES), lambda b: (b, 0, 0)),
                  pl.BlockSpec((blk, blk), lambda b: (0, 0)),
                  pl.BlockSpec((n, LANES), lambda b: (0, 0))],
        out_specs=[out, out],
        out_shape=[jax.ShapeDtypeStruct((B, N_EXPERTS, cap, LANES), jnp.int32),
                   jax.ShapeDtypeStruct((B, N_EXPERTS, cap, LANES), F32)],
        scratch_shapes=[pltpu.VMEM((N_EXPERTS, n), F32), pltpu.VMEM((n, LANES), BF16)],
        compiler_params=_cparams("arbitrary"), name="select",
    )(aff.reshape(B, n, LANES), jnp.asarray(tri, BF16), jnp.asarray(tv))


DMA_UNROLL = 8

def _expert_kernel(idx_ref, g_ref, gate_ref, w1_ref, w3_ref, w2_ref, h_hbm, acc_in, acc_hbm,
                   xs_buf, acc_buf, sem, *, bb, cap, n, nb, gate_rows, pipelined):
    del acc_in
    e = pl.program_id(0)
    bi = pl.program_id(1)
    t = e * nb + bi
    n_steps = pl.num_programs(0) * nb
    slot = t % 2 if pipelined else 0
    other = 1 - slot
    rows = bb * cap
    D = xs_buf.shape[2]

    def for_rows(step_e, step_bi, fn):
        for bl in range(bb):
            b = step_bi * bb + bl
            base = (b * N_EXPERTS + step_e) * cap

            def body(s, carry, bl=bl, b=b, base=base):
                fn(bl * cap + s, b * n + idx_ref[base + s])
                return carry

            lax.fori_loop(0, cap, body, 0, unroll=DMA_UNROLL)

    def x_copy(sl, j, r):
        return pltpu.make_async_copy(h_hbm.at[pl.ds(r, 1), :], xs_buf.at[sl, pl.ds(j, 1), :], sem.at[0, sl])

    def acc_load(sl, j, r):
        return pltpu.make_async_copy(acc_hbm.at[pl.ds(r, 1), :], acc_buf.at[sl, pl.ds(j, 1), :], sem.at[1, sl])

    def acc_store(sl, j, r):
        return pltpu.make_async_copy(acc_buf.at[sl, pl.ds(j, 1), :], acc_hbm.at[pl.ds(r, 1), :], sem.at[2, sl])

    def loads(step_e, step_bi, sl, start):
        def fn(j, r):
            for cp in (x_copy(sl, j, r), acc_load(sl, j, r)):
                cp.start() if start else cp.wait()
        for_rows(step_e, step_bi, fn)

    def stores(step_e, step_bi, sl, start):
        def fn(j, r):
            cp = acc_store(sl, j, r)
            cp.start() if start else cp.wait()
        for_rows(step_e, step_bi, fn)

    if pipelined:
        @pl.when(t == 0)
        def _():
            loads(e, bi, slot, True)
    else:
        loads(e, bi, slot, True)
    loads(e, bi, slot, False)

    x = xs_buf[slot].astype(BF16)
    hidden = (_silu(_dot(x, w1_ref[0])) * _dot(x, w3_ref[0])).astype(BF16)
    y = _dot(hidden, w2_ref[0])

    if pipelined:
        wrap_lo = bi == 0
        wrap_hi = bi == nb - 1

        @pl.when(t > 0)
        def _():
            stores(jnp.where(wrap_lo, e - 1, e), jnp.where(wrap_lo, nb - 1, bi - 1), other, False)

        @pl.when(t < n_steps - 1)
        def _():
            loads(jnp.where(wrap_hi, e + 1, e), jnp.where(wrap_hi, 0, bi + 1), other, True)

    g = g_ref[...].reshape(rows, LANES)
    for bl in range(bb):
        rs = slice(bl * cap, (bl + 1) * cap)
        gate_row = gate_ref[bl if gate_rows > 1 else 0]
        for c in range(D // LANES):
            cs = slice(c * LANES, (c + 1) * LANES)
            acc_buf[slot, rs, cs] += y[rs, cs] * g[rs] * gate_row[:, cs]

    stores(e, bi, slot, True)
    if pipelined:
        @pl.when(t == n_steps - 1)
        def _():
            stores(e, bi, slot, False)
    else:
        stores(e, bi, slot, False)


def _experts(idx, g, gate, w1, w3, w2, h2, x_mid, B, n, cap, bb):
    E, D, FF = w1.shape
    rows = bb * cap
    nb = B // bb
    pipelined = nb >= 2
    nslot = 2 if pipelined else 1
    gate_rows = bb if gate.shape[0] > 1 else 1
    gate_spec = (pl.BlockSpec((bb, 1, D), lambda e, b, idx: (b, 0, 0)) if gate.shape[0] > 1
                 else pl.BlockSpec((1, 1, D), lambda e, b, idx: (0, 0, 0)))
    grid_spec = pltpu.PrefetchScalarGridSpec(
        num_scalar_prefetch=1,
        grid=(E, nb),
        in_specs=[pl.BlockSpec((bb, 1, cap, LANES), lambda e, b, idx: (b, e, 0, 0)),
                  gate_spec,
                  pl.BlockSpec((1, D, FF), lambda e, b, idx: (e, 0, 0)),
                  pl.BlockSpec((1, D, FF), lambda e, b, idx: (e, 0, 0)),
                  pl.BlockSpec((1, FF, D), lambda e, b, idx: (e, 0, 0)),
                  pl.BlockSpec(memory_space=pl.ANY),
                  pl.BlockSpec(memory_space=pl.ANY)],
        out_specs=pl.BlockSpec(memory_space=pl.ANY),
        scratch_shapes=[pltpu.VMEM((nslot, rows, D), F32), pltpu.VMEM((nslot, rows, D), F32),
                        pltpu.SemaphoreType.DMA((3, nslot))])
    return pl.pallas_call(
        functools.partial(_expert_kernel, bb=bb, cap=cap, n=n, nb=nb, gate_rows=gate_rows, pipelined=pipelined),
        grid_spec=grid_spec,
        out_shape=jax.ShapeDtypeStruct(x_mid.shape, F32),
        input_output_aliases={7: 0},
        compiler_params=_cparams("arbitrary", "arbitrary"), name="experts",
    )(idx, g, gate, w1, w3, w2, h2, x_mid)


def _moe(h2, aff, x_mid, gate, w1, w3, w2, B, n, bb):
    cap = max(1, CAP_FACTOR * n // N_EXPERTS)
    idx, g = _select(aff, B, n, cap)
    return _experts(idx[..., 0].reshape(-1), g, gate, w1, w3, w2, h2, x_mid, B, n, cap, bb)


def _rope_tables(S):
    half, quarter = HEAD_DIM // 2, HEAD_DIM // 4
    t = jnp.arange(S)
    freqs = 1.0 / (ROPE_BASE ** (jnp.arange(quarter, dtype=F32) / quarter))
    d = np.arange(LANES) % HEAD_DIM
    pos = jnp.where(jnp.asarray(d < half)[None, :], (t // GRID_W)[:, None], (t % GRID_W)[:, None]).astype(F32)
    ang = pos * freqs[jnp.asarray(d % quarter)][None, :]
    first = jnp.asarray((d % half) < quarter)[None, :]
    sin = jnp.sin(ang)
    return jnp.cos(ang), jnp.where(first, -sin, 0.0), jnp.where(first, 0.0, sin)


def _bias_pair_blocks(rpb):
    cq = np.arange(GRID_W)
    ck = np.arange(GRID_W)
    cstart = np.clip(cq - WIN_COLS // 2, 0, GRID_W - WIN_COLS)
    valid = (ck[None, :] >= cstart[:, None]) & (ck[None, :] < cstart[:, None] + WIN_COLS)
    col_off = np.clip(ck[None, :] - cq[:, None] + WIN_COLS - 1, 0, 2 * WIN_COLS - 2)
    ro = np.clip(np.arange(N_PAIR_BLOCKS + 1) - 8, 0, 2 * WIN_ROWS - 2)
    T = jnp.where(jnp.asarray(valid)[None, None], rpb[:, ro][:, :, col_off], NEG_INF)
    return jnp.concatenate([T[:, :-1], T[:, 1:]], axis=-1)


def _head_sum_tables():
    lane = np.arange(ATTN_WIDTH)
    bd = (lane[:, None] // HEAD_DIM == np.arange(LANES)[None, :]).astype(np.float32)
    return jnp.asarray(bd, BF16), jnp.asarray(bd.T, BF16)


def kernel(x, c, ctx, c_ctx, w_ada, b_ada, g_norm1, g_norm2, w_in, b_in, g_q, g_k, rpb, w_attn_o, conv_dw_w,
           conv_dw_b, conv_ln_g, conv_ln_b, w_conv_o, sc_w, w_sc_o, w_o, w_router, w_e1, w_e3, w_e2):
    B, S, D = x.shape
    C = ctx.shape[1]
    L = w_ada.shape[0]
    aw3 = 3 * ATTN_WIDTH
    prw = 2 * CONV_WIDTH + 3 * SC_WIDTH + 3 * D

    cc = jnp.zeros((8, D), F32).at[0:B].set(c).at[B].set(c_ctx)
    mod = _ada(cc, w_ada, b_ada)
    rope_tabs = _rope_tables(S)
    bd, ex = _head_sum_tables()

    xl = x.reshape(B * S, D)
    xc = ctx.reshape(B * C, D)
    b_in3 = b_in.reshape(L, 1, -1)

    for l in range(L):
        last = l == L - 1
        ml = lambda j: mod[l, 0:B, j * D:(j + 1) * D].reshape(B, 1, D)
        mc = lambda j: mod[l, B:B + 1, j * D:(j + 1) * D].reshape(1, 1, D)
        g1 = g_norm1[l].reshape(1, D)
        g2 = g_norm2[l].reshape(1, D)
        w_qkv = w_in[l][:, :aw3].astype(BF16)
        b_qkv = b_in[l][:aw3].reshape(1, aw3)
        gq = jnp.tile(g_q[l], N_HEADS).reshape(1, ATTN_WIDTH)
        gk = jnp.tile(g_k[l], N_HEADS).reshape(1, ATTN_WIDTH)
        branch_w = (w_attn_o[l].astype(BF16), w_conv_o[l].astype(BF16), w_sc_o[l].astype(BF16),
                    jnp.zeros((32, CONV_WIDTH), F32).at[:CONV_K].set(conv_dw_w[l]),
                    conv_dw_b[l].reshape(1, -1), conv_ln_g[l].reshape(1, -1), conv_ln_b[l].reshape(1, -1),
                    jnp.zeros((8, SC_WIDTH), F32).at[:SC_K].set(sc_w[l]))
        wo = w_o[l].astype(BF16)
        wr = jnp.zeros((D, LANES), F32).at[:, :N_EXPERTS].set(w_router[l]).astype(BF16)
        we = (w_e1[l].astype(BF16), w_e3[l].astype(BF16), w_e2[l].astype(BF16))

        hc = _normmod(xc, g1, mc(0), mc(1), C)
        qc, kc, vc = _qkv(hc, w_qkv, b_qkv, gq, gk, bd, ex, None, C)
        if not last:
            attn_c = _ctx_attn(qc, kc, vc, B, C)
            pr_c = _proj(hc, w_in, b_in3, l, aw3, prw)
            merged_c = _branch(attn_c, pr_c, *branch_w, C, D)
            xc_mid, hc2, aff_c = _outproj(merged_c, xc, wo, mc(2), g2, mc(3), mc(4), wr, C)
            xc = _moe(hc2, aff_c, xc_mid, mc(5), *we, B, C, B)

        h = _normmod(xl, g1, ml(0), ml(1), S)
        qn, kn, v, qr, kr = _qkv(h, w_qkv, b_qkv, gq, gk, bd, ex, rope_tabs, S)
        attn = _attn(qn, qr, kr, v, kc, vc, _bias_pair_blocks(rpb[l]), B, S, C)
        pr = _proj(h, w_in, b_in3, l, aw3, prw)
        merged = _branch(attn, pr, *branch_w, S, D)
        x_mid, h2, aff = _outproj(merged, xl, wo, ml(2), g2, ml(3), ml(4), wr, S)
        xl = _moe(h2, aff, x_mid, ml(5), *we, B, S, 1)

    return xl.reshape(B, S, D)
```

```python
import functools

import numpy as np
import jax
import jax.numpy as jnp
from jax import lax
from jax.experimental import pallas as pl
from jax.experimental.pallas import tpu as pltpu

F32 = jnp.float32
BF16 = jnp.bfloat16

GRID_W = 64
N_HEADS = 16
HEAD_DIM = 64
ATTN_WIDTH = N_HEADS * HEAD_DIM
N_HEAD_PAIRS = N_HEADS // 2
WIN_ROWS = 8
WIN_COLS = 16
ROPE_BASE = 10000.0
CONV_WIDTH = 512
CONV_K = 31
SC_WIDTH = 512
SC_K = 3
N_EXPERTS = 16
CAP_FACTOR = 2
EPS = 1e-6
NEG_INF = -1e30

LANES = 128
SUBLANES = 8
LOG2E = 1.4426950408889634
Q_SCALE = HEAD_DIM ** -0.5 * LOG2E
HALO = 16
ROW_GROUP = 4
KEY_ROWS = 12
N_PAIR_BLOCKS = 26
VMEM_LIMIT = 56 * 1024 * 1024


def _cparams(*sem):
    return pltpu.CompilerParams(dimension_semantics=sem, vmem_limit_bytes=VMEM_LIMIT)


def _dot(a, b):
    return jnp.dot(a, b, preferred_element_type=F32)


def _dot_t(a, b):
    return lax.dot_general(a, b, (((1,), (1,)), ((), ())), preferred_element_type=F32)


def _sigmoid(x):
    return 0.5 * jnp.tanh(0.5 * x) + 0.5


def _silu(x):
    return x * _sigmoid(x)


def _ada_kernel(cc_ref, w_ref, b_ref, o_ref):
    s = _silu(cc_ref[...]).astype(BF16)
    o_ref[0] = _dot(s, w_ref[0].astype(BF16)) + b_ref[0]


def _ada(cc, w_ada, b_ada):
    L, D, W = w_ada.shape
    tn = 1024
    return pl.pallas_call(
        _ada_kernel,
        grid=(L, W // tn),
        in_specs=[pl.BlockSpec((8, D), lambda l, j: (0, 0)),
                  pl.BlockSpec((1, D, tn), lambda l, j: (l, 0, j)),
                  pl.BlockSpec((1, 1, tn), lambda l, j: (l, 0, j))],
        out_specs=pl.BlockSpec((1, 8, tn), lambda l, j: (l, 0, j)),
        out_shape=jax.ShapeDtypeStruct((L, 8, W), F32),
        compiler_params=_cparams("arbitrary", "arbitrary"),
        name="ada",
    )(cc, w_ada, b_ada.reshape(L, 1, W))


def _mod_spec(nmod, tiles_per_mod, D):
    if nmod == 1:
        return pl.BlockSpec((1, 1, D), lambda i: (0, 0, 0))
    return pl.BlockSpec((1, 1, D), lambda i: (i // tiles_per_mod, 0, 0))


def _normmod_kernel(x_ref, g_ref, sh_ref, sc_ref, h_ref):
    x = x_ref[...]
    ms = jnp.mean(x * x, axis=-1, keepdims=True)
    y = x * lax.rsqrt(ms + EPS) * g_ref[...]
    h_ref[...] = (y * (1.0 + sc_ref[0]) + sh_ref[0]).astype(BF16)


def _normmod(x, g, shift, scale, seq_len):
    rows, D = x.shape
    tm = min(512, seq_len)
    mspec = _mod_spec(shift.shape[0], seq_len // tm, D)
    xspec = pl.BlockSpec((tm, D), lambda i: (i, 0))
    return pl.pallas_call(
        _normmod_kernel, grid=(rows // tm,),
        in_specs=[xspec, pl.BlockSpec((1, D), lambda i: (0, 0)), mspec, mspec],
        out_specs=xspec, out_shape=jax.ShapeDtypeStruct((rows, D), BF16),
        compiler_params=_cparams("arbitrary"), name="normmod",
    )(x, g, shift, scale)


def _qkv_kernel(*refs, rope):
    if rope:
        (h_ref, w_ref, b_ref, gq_ref, gk_ref, bd_ref, ex_ref, cos_ref, sa_ref, sb_ref,
         qn_ref, kn_ref, v_ref, qr_ref, kr_ref) = refs
    else:
        h_ref, w_ref, b_ref, gq_ref, gk_ref, bd_ref, ex_ref, qn_ref, kn_ref, v_ref = refs
    h = h_ref[...]
    aw = ATTN_WIDTH

    def proj(s):
        return _dot(h, w_ref[:, s * aw:(s + 1) * aw]) + b_ref[:, s * aw:(s + 1) * aw]

    def headnorm(a, g_ref):
        ss = _dot((a * a).astype(BF16), bd_ref[...])
        inv = lax.rsqrt(ss * (1.0 / HEAD_DIM) + EPS)
        inv_hi = inv.astype(BF16)
        inv_lo = (inv - inv_hi.astype(F32)).astype(BF16)
        full = _dot(inv_hi, ex_ref[...]) + _dot(inv_lo, ex_ref[...])
        return a * full * g_ref[...]

    def store_plain(xn, out_ref):
        for c in range(N_HEAD_PAIRS):
            out_ref[c] = xn[:, c * LANES:(c + 1) * LANES].astype(BF16)

    def store_rope(xn, out_ref):
        cos, sa, sb = cos_ref[...], sa_ref[...], sb_ref[...]
        for c in range(N_HEAD_PAIRS):
            xc = xn[:, c * LANES:(c + 1) * LANES]
            r = xc * cos + pltpu.roll(xc, LANES - 16, 1) * sa + pltpu.roll(xc, 16, 1) * sb
            out_ref[c] = r.astype(BF16)

    qn = headnorm(proj(0), gq_ref) * Q_SCALE
    store_plain(qn, qn_ref)
    if rope:
        store_rope(qn, qr_ref)
    kn = headnorm(proj(1), gk_ref)
    store_plain(kn, kn_ref)
    if rope:
        store_rope(kn, kr_ref)
    store_plain(proj(2), v_ref)


def _qkv(h, w, b, gq, gk, bd, ex, rope_tabs, seq_len):
    rows, D = h.shape
    tm = min(512, seq_len)
    rope = rope_tabs is not None
    const = lambda i: (0, 0)
    in_specs = [pl.BlockSpec((tm, D), lambda i: (i, 0)),
                pl.BlockSpec((D, 3 * ATTN_WIDTH), const, pipeline_mode=pl.Buffered(1)),
                pl.BlockSpec((1, 3 * ATTN_WIDTH), const),
                pl.BlockSpec((1, ATTN_WIDTH), const),
                pl.BlockSpec((1, ATTN_WIDTH), const),
                pl.BlockSpec((ATTN_WIDTH, LANES), const),
                pl.BlockSpec((LANES, ATTN_WIDTH), const)]
    ins = [h, w, b, gq, gk, bd, ex]
    n_out = 3
    if rope:
        tiles_per_seq = seq_len // tm
        tspec = pl.BlockSpec((tm, LANES), lambda i: (i % tiles_per_seq, 0))
        in_specs += [tspec, tspec, tspec]
        ins += list(rope_tabs)
        n_out = 5
    ospec = pl.BlockSpec((N_HEAD_PAIRS, tm, LANES), lambda i: (0, i, 0))
    return pl.pallas_call(
        functools.partial(_qkv_kernel, rope=rope),
        grid=(rows // tm,), in_specs=in_specs,
        out_specs=[ospec] * n_out,
        out_shape=[jax.ShapeDtypeStruct((N_HEAD_PAIRS, rows, LANES), BF16)] * n_out,
        compiler_params=_cparams("arbitrary"), name="qkv",
    )(*ins)


def _proj_kernel(h_ref, w_ref, b_ref, o_ref, wb_ref):
    @pl.when(pl.program_id(1) == 0)
    def _():
        wb_ref[...] = w_ref[0].astype(BF16)

    o_ref[...] = (_dot(h_ref[...], wb_ref[...]) + b_ref[0]).astype(o_ref.dtype)


def _proj(h, w, b, l, col0, ncols):
    rows, D = h.shape
    tm = min(2048, rows)
    tn = 512
    j0 = col0 // tn
    return pl.pallas_call(
        _proj_kernel, grid=(ncols // tn, rows // tm),
        in_specs=[pl.BlockSpec((tm, D), lambda j, i: (i, 0)),
                  pl.BlockSpec((1, D, tn), lambda j, i: (l, 0, j + j0)),
                  pl.BlockSpec((1, 1, tn), lambda j, i: (l, 0, j + j0))],
        out_specs=pl.BlockSpec((tm, tn), lambda j, i: (i, j)),
        out_shape=jax.ShapeDtypeStruct((rows, ncols), BF16),
        scratch_shapes=[pltpu.VMEM((D, tn), BF16)],
        compiler_params=_cparams("arbitrary", "arbitrary"), name="proj",
    )(h, w, b)


def _attn_kernel(qn_ref, qr_ref, kr_ref, v_ref, kc_ref, vc_ref, fp_ref, o_ref, *, rows):
    gq = ROW_GROUP * GRID_W
    nk = KEY_ROWS * GRID_W
    lane = lax.broadcasted_iota(jnp.int32, (1, LANES), 1)
    head_mask = (lane < HEAD_DIM, lane >= HEAD_DIM)
    key_row = lax.broadcasted_iota(jnp.int32, (1, nk), 1) // GRID_W
    kc = kc_ref[0]
    vc = [jnp.where(head_mask[hh], vc_ref[0], 1) for hh in range(2)]

    def group(gi, carry):
        r0 = gi * ROW_GROUP
        ws = jnp.clip(r0 - WIN_ROWS // 2, 0, rows - KEY_ROWS)
        tok0 = pl.multiple_of(gi * gq, gq)
        key0 = pl.multiple_of(ws * GRID_W, GRID_W)
        qr = qr_ref[0, pl.ds(tok0, gq), :]
        qn = qn_ref[0, pl.ds(tok0, gq), :]
        kw = kr_ref[0, pl.ds(key0, nk), :]
        vw = v_ref[0, pl.ds(key0, nk), :]
        scores = [(_dot_t(jnp.where(head_mask[hh], qr, 0), kw),
                   _dot_t(jnp.where(head_mask[hh], qn, 0), kc))
                  for hh in range(2)]
        outs = []
        for hh in range(2):
            s_loc, s_ctx = scores[hh]
            slabs = []
            for i in range(ROW_GROUP):
                n0 = ws - r0 - i + (WIN_ROWS - 1) + 8
                bias = jnp.concatenate([fp_ref[hh, n0 + 2 * jp] for jp in range(KEY_ROWS // 2)], axis=1)
                lo = jnp.clip(r0 + i - WIN_ROWS // 2, 0, rows - WIN_ROWS) - ws
                valid = (key_row >= lo) & (key_row < lo + WIN_ROWS)
                slabs.append(jnp.where(valid, s_loc[i * GRID_W:(i + 1) * GRID_W] + bias, NEG_INF))
            s_loc = jnp.concatenate(slabs, axis=0)
            m = jnp.maximum(jnp.max(s_loc, axis=-1, keepdims=True), jnp.max(s_ctx, axis=-1, keepdims=True))
            p_loc = jnp.exp2(s_loc - m).astype(BF16)
            p_ctx = jnp.exp2(s_ctx - m).astype(BF16)
            o = _dot(p_loc, jnp.where(head_mask[hh], vw, 1)) + _dot(p_ctx, vc[hh])
            outs.append(o / pltpu.roll(o, HEAD_DIM, 1))
        o_ref[0, pl.ds(tok0, gq), :] = jnp.where(head_mask[0], outs[0], outs[1]).astype(BF16)
        return carry

    lax.fori_loop(0, rows // ROW_GROUP, group, 0, unroll=2)


def _attn(qn, qr, kr, v, kc, vc, fp, B, S, C):
    rows = S // GRID_W
    tok = pl.BlockSpec((1, S, LANES), lambda b, p: (p, b, 0))
    ctx = pl.BlockSpec((1, C, LANES), lambda b, p: (p, b, 0))
    return pl.pallas_call(
        functools.partial(_attn_kernel, rows=rows),
        grid=(B, N_HEAD_PAIRS),
        in_specs=[tok, tok, tok, tok, ctx, ctx,
                  pl.BlockSpec((2, N_PAIR_BLOCKS, GRID_W, LANES), lambda b, p: (p, 0, 0, 0))],
        out_specs=tok,
        out_shape=jax.ShapeDtypeStruct((N_HEAD_PAIRS, B * S, LANES), BF16),
        compiler_params=_cparams("arbitrary", "arbitrary"), name="attn",
    )(qn, qr, kr, v, kc, vc, fp)


def _ctx_attn_kernel(q_ref, k_ref, v_ref, o_ref):
    lane = lax.broadcasted_iota(jnp.int32, (1, LANES), 1)
    head_mask = (lane < HEAD_DIM, lane >= HEAD_DIM)
    q, k, v = q_ref[0], k_ref[0], v_ref[0]
    outs = []
    for hh in range(2):
        s = _dot_t(jnp.where(head_mask[hh], q, 0), k)
        m = jnp.max(s, axis=-1, keepdims=True)
        p = jnp.exp2(s - m)
        outs.append(_dot(p.astype(BF16), v) / jnp.sum(p, axis=-1, keepdims=True))
    o_ref[0] = jnp.where(head_mask[0], outs[0], outs[1]).astype(BF16)


def _ctx_attn(q, k, v, B, C):
    spec = pl.BlockSpec((1, C, LANES), lambda b, p: (p, b, 0))
    return pl.pallas_call(
        _ctx_attn_kernel, grid=(B, N_HEAD_PAIRS),
        in_specs=[spec, spec, spec], out_specs=spec,
        out_shape=jax.ShapeDtypeStruct((N_HEAD_PAIRS, B * C, LANES), BF16),
        compiler_params=_cparams("arbitrary", "arbitrary"), name="ctx_attn",
    )(q, k, v)


def _branch_kernel(attn_ref, pr_ref, pcp_ref, pcn_ref, wa_ref, wc_ref, ws_ref,
                   dww_ref, dwb_ref, lng_ref, lnb_ref, scw_ref, o_ref, glu_ext, sc_ext, *, tm, seq_len, D):
    i = pl.program_id(0)
    pos0 = (i * tm) % seq_len
    keep_prev = (pos0 != 0).astype(F32)
    keep_next = (pos0 + tm != seq_len).astype(F32)
    cw, sw = CONV_WIDTH, SC_WIDTH

    def glu(blk):
        return blk[:, 0:cw].astype(F32) * _sigmoid(blk[:, cw:2 * cw].astype(F32))

    def gated_x(blk):
        o = 2 * cw + sw
        return blk[:, o:o + sw].astype(F32) * blk[:, o + sw:o + 2 * sw].astype(F32)

    pcw = 2 * cw + 3 * sw
    cur, prv, nxt = pr_ref[:, 0:pcw], pcp_ref[...], pcn_ref[...]
    glu_ext[0:HALO, :] = glu(prv) * keep_prev
    glu_ext[HALO:HALO + tm, :] = glu(cur)
    glu_ext[HALO + tm:2 * HALO + tm, :] = glu(nxt) * keep_next
    sc_ext[0:HALO, :] = gated_x(prv) * keep_prev
    sc_ext[HALO:HALO + tm, :] = gated_x(cur)
    sc_ext[HALO + tm:2 * HALO + tm, :] = gated_x(nxt) * keep_next

    chunk = 32
    window = chunk + 2 * HALO
    conv_rows = []
    for r in range(0, tm, chunk):
        win = glu_ext[r:r + window, :]
        acc = jnp.zeros((chunk, cw), F32)
        for res in range(SUBLANES):
            shifted = win if res == 0 else pltpu.roll(win, window - res, 0)
            for k in range(CONV_K):
                off = HALO - CONV_K // 2 + k
                if off % SUBLANES == res:
                    acc = acc + dww_ref[k:k + 1, :] * shifted[off - res:off - res + chunk, :]
        conv_rows.append(acc)
    u = jnp.concatenate(conv_rows, axis=0) + dwb_ref[...]
    mu = jnp.mean(u, axis=-1, keepdims=True)
    uc = u - mu
    var = jnp.mean(uc * uc, axis=-1, keepdims=True)
    u = _silu(uc * lax.rsqrt(var + EPS) * lng_ref[...] + lnb_ref[...])
    y_b = _dot(u.astype(BF16), wc_ref[...])

    c3 = jnp.zeros((tm, sw), F32)
    for k in range(SC_K):
        base = HALO - SC_K // 2 + k
        c3 = c3 + scw_ref[k:k + 1, :] * sc_ext[base:base + tm, :]
    sc_b = cur[:, 2 * cw:2 * cw + sw].astype(F32)
    y_c = _dot((sc_b * c3).astype(BF16), ws_ref[...])

    attn = jnp.concatenate([attn_ref[c] for c in range(N_HEAD_PAIRS)], axis=1)
    y_a = _dot(attn, wa_ref[...])

    g = pr_ref[:, pcw:pcw + 3 * D]
    merged = (_sigmoid(g[:, 0:D].astype(F32)) * y_a
              + _sigmoid(g[:, D:2 * D].astype(F32)) * y_b
              + _sigmoid(g[:, 2 * D:3 * D].astype(F32)) * y_c)
    o_ref[...] = merged.astype(BF16)


def _branch(attn, pr, wa, wc, ws, dww, dwb, lng, lnb, scw, seq_len, D):
    rows, prw = pr.shape
    tm = 256
    pcw = 2 * CONV_WIDTH + 3 * SC_WIDTH
    nhalo = rows // HALO
    const = lambda i: (0, 0)
    return pl.pallas_call(
        functools.partial(_branch_kernel, tm=tm, seq_len=seq_len, D=D),
        grid=(rows // tm,),
        in_specs=[pl.BlockSpec((N_HEAD_PAIRS, tm, LANES), lambda i: (0, i, 0)),
                  pl.BlockSpec((tm, prw), lambda i: (i, 0)),
                  pl.BlockSpec((HALO, pcw), lambda i: (jnp.maximum(i * (tm // HALO) - 1, 0), 0)),
                  pl.BlockSpec((HALO, pcw), lambda i: (jnp.minimum((i + 1) * (tm // HALO), nhalo - 1), 0)),
                  pl.BlockSpec((ATTN_WIDTH, D), const),
                  pl.BlockSpec((CONV_WIDTH, D), const),
                  pl.BlockSpec((SC_WIDTH, D), const),
                  pl.BlockSpec((32, CONV_WIDTH), const),
                  pl.BlockSpec((1, CONV_WIDTH), const),
                  pl.BlockSpec((1, CONV_WIDTH), const),
                  pl.BlockSpec((1, CONV_WIDTH), const),
                  pl.BlockSpec((8, SC_WIDTH), const)],
        out_specs=pl.BlockSpec((tm, D), lambda i: (i, 0)),
        out_shape=jax.ShapeDtypeStruct((rows, D), BF16),
        scratch_shapes=[pltpu.VMEM((tm + 2 * HALO, CONV_WIDTH), F32),
                        pltpu.VMEM((tm + 2 * HALO, SC_WIDTH), F32)],
        compiler_params=_cparams("arbitrary"), name="branch",
    )(attn, pr, pr, pr, wa, wc, ws, dww, dwb, lng, lnb, scw)


def _outproj_kernel(m_ref, x_ref, wo_ref, gate_ref, g2_ref, sh_ref, sc_ref, wr_ref, xmid_ref, h2_ref, aff_ref):
    x = x_ref[...] + gate_ref[0] * _dot(m_ref[...], wo_ref[...])
    xmid_ref[...] = x
    ms = jnp.mean(x * x, axis=-1, keepdims=True)
    y = x * lax.rsqrt(ms + EPS) * g2_ref[...]
    h2 = y * (1.0 + sc_ref[0]) + sh_ref[0]
    h2_ref[...] = h2
    logits = _dot(h2.astype(BF16), wr_ref[...])
    lane = lax.broadcasted_iota(jnp.int32, logits.shape, 1)
    logits = jnp.where(lane < N_EXPERTS, logits, NEG_INF)
    e = jnp.exp(logits - jnp.max(logits, axis=-1, keepdims=True))
    aff_ref[...] = e / jnp.sum(e, axis=-1, keepdims=True)


def _outproj(merged, x, wo, gate, g2, shift, scale, wr, seq_len):
    rows, D = x.shape
    tm = 256
    mspec = _mod_spec(gate.shape[0], seq_len // tm, D)
    row = lambda i: (i, 0)
    const = lambda i: (0, 0)
    return pl.pallas_call(
        _outproj_kernel, grid=(rows // tm,),
        in_specs=[pl.BlockSpec((tm, D), row), pl.BlockSpec((tm, D), row),
                  pl.BlockSpec((D, D), const), mspec,
                  pl.BlockSpec((1, D), const), mspec, mspec,
                  pl.BlockSpec((D, LANES), const)],
        out_specs=[pl.BlockSpec((tm, D), row), pl.BlockSpec((tm, D), row), pl.BlockSpec((tm, LANES), row)],
        out_shape=[jax.ShapeDtypeStruct((rows, D), F32), jax.ShapeDtypeStruct((rows, D), F32),
                   jax.ShapeDtypeStruct((rows, LANES), F32)],
        compiler_params=_cparams("arbitrary"), name="outproj",
    )(merged, x, wo, gate, g2, shift, scale, wr)


MAX_BISECT = 160
POS_LANE = 3 * N_EXPERTS


def _select_kernel(aff_ref, tri_ref, tv_ref, idx_ref, g_ref, s1_ref, r_ref, *, n, cap, blk, tchunk):
    E = N_EXPERTS
    aff = aff_ref[0]
    a = aff.T[0:E, :]
    capf = float(cap)

    def cond(st):
        it, _, _, ndone = st
        return jnp.logical_and(it < MAX_BISECT, ndone < float(E))

    def in_range(lo, hi):
        return jnp.logical_and(a >= lo, a < hi)

    def body(st):
        it, lo, hi, _ = st
        mid = 0.5 * (lo + hi)
        cnt = jnp.sum((a >= mid).astype(F32), axis=-1, keepdims=True)
        ge = cnt >= capf
        lo = jnp.where(ge, mid, lo)
        hi = jnp.where(ge, hi, mid)
        r = in_range(lo, hi)
        vmin = jnp.min(jnp.where(r, a, 4.0), axis=-1, keepdims=True)
        vmax = jnp.max(jnp.where(r, a, -1.0), axis=-1, keepdims=True)
        ndone = jnp.sum((vmin == vmax).astype(F32))
        return it + 1, lo, hi, ndone

    _, lo, hi, _ = lax.while_loop(cond, body, (jnp.int32(0), jnp.zeros((E, 1), F32), jnp.full((E, 1), 2.0, F32),
                                               jnp.float32(0.0)))
    thr = jnp.max(jnp.where(in_range(lo, hi), a, -1.0), axis=-1, keepdims=True)

    def cumsum(x):
        outs, carry = [], jnp.zeros((E, 1), F32)
        for c in range(n // blk):
            part = _dot(x[:, c * blk:(c + 1) * blk].astype(BF16), tri_ref[...]) + carry
            outs.append(part)
            carry = part[:, blk - 1:blk]
        return jnp.concatenate(outs, axis=1)

    gt = a > thr
    eq = (a == thr).astype(F32)
    need = capf - jnp.sum(gt.astype(F32), axis=-1, keepdims=True)
    eq_rank = cumsum(eq) - eq
    sel = jnp.logical_or(gt, jnp.logical_and(eq > 0.0, eq_rank < need)).astype(F32)
    s1_ref[...] = sel * cumsum(sel)

    a_hi = aff.astype(BF16).astype(F32)
    rest = aff - a_hi
    a_mid = rest.astype(BF16).astype(F32)
    a_lo = rest - a_mid
    r_ref[...] = (a_hi + pltpu.roll(a_mid, E, 1) + pltpu.roll(a_lo, 2 * E, 1) + tv_ref[...]).astype(BF16)

    slot_ids = lax.broadcasted_iota(jnp.int32, (cap, 1), 0).astype(F32) + 1.0
    lane = lax.broadcasted_iota(jnp.int32, (1, LANES), 1)

    def per_expert(e, carry):
        row = s1_ref[pl.ds(e, 1), :]
        acc = jnp.zeros((cap, LANES), F32)
        for c in range(n // tchunk):
            onehot = (row[:, c * tchunk:(c + 1) * tchunk] == slot_ids).astype(BF16)
            acc = acc + _dot(onehot, r_ref[c * tchunk:(c + 1) * tchunk, :])
        mine = jnp.logical_and(lane % E == e, lane < POS_LANE)
        g = jnp.sum(jnp.where(mine, acc, 0.0), axis=1, keepdims=True)
        tok = jnp.sum(jnp.where(lane == POS_LANE, acc * 64.0, jnp.where(lane == POS_LANE + 1, acc, 0.0)),
                      axis=1, keepdims=True)
        g_ref[0, e] = jnp.broadcast_to(g, (cap, LANES))
        idx_ref[0, e] = jnp.broadcast_to(tok, (cap, LANES)).astype(jnp.int32)
        return carry

    lax.fori_loop(0, E, per_expert, 0)


def _select(aff, B, n, cap):
    blk = min(512, n)
    tchunk = min(1024, n)
    tri = (np.arange(blk)[:, None] <= np.arange(blk)[None, :]).astype(np.float32)
    tv = np.zeros((n, LANES), np.float32)
    tv[:, POS_LANE] = np.arange(n) // 64
    tv[:, POS_LANE + 1] = np.arange(n) % 64
    out = pl.BlockSpec((1, N_EXPERTS, cap, LANES), lambda b: (b, 0, 0, 0))
    return pl.pallas_call(
        functools.partial(_select_kernel, n=n, cap=cap, blk=blk, tchunk=tchunk),
        grid=(B,),
        in_specs=[pl.BlockSpec((1, n, LANES), lambda b: (b, 0, 0)),
                  pl.BlockSpec((blk, blk), lambda b: (0, 0)),
                  pl.BlockSpec((n, LANES), lambda b: (0, 0))],
        out_specs=[out, out],
        out_shape=[jax.ShapeDtypeStruct((B, N_EXPERTS, cap, LANES), jnp.int32),
                   jax.ShapeDtypeStruct((B, N_EXPERTS, cap, LANES), F32)],
        scratch_shapes=[pltpu.VMEM((N_EXPERTS, n), F32), pltpu.VMEM((n, LANES), BF16)],
        compiler_params=_cparams("arbitrary"), name="select",
    )(aff.reshape(B, n, LANES), jnp.asarray(tri, BF16), jnp.asarray(tv))


DMA_UNROLL = 8

def _expert_kernel(idx_ref, g_ref, gate_ref, w1_ref, w3_ref, w2_ref, h_hbm, acc_in, acc_hbm,
                   xs_buf, acc_buf, sem, *, bb, cap, n, gate_rows):
    del acc_in
    e = pl.program_id(0)
    b0 = pl.program_id(1) * bb
    rows = bb * cap
    D = xs_buf.shape[1]

    def for_rows(fn):
        for bl in range(bb):
            b = b0 + bl
            base = (b * N_EXPERTS + e) * cap

            def body(s, carry, bl=bl, b=b, base=base):
                fn(bl * cap + s, b * n + idx_ref[base + s])
                return carry

            lax.fori_loop(0, cap, body, 0, unroll=DMA_UNROLL)

    def x_copy(j, r):
        return pltpu.make_async_copy(h_hbm.at[pl.ds(r, 1), :], xs_buf.at[pl.ds(j, 1), :], sem.at[0])

    def acc_load(j, r):
        return pltpu.make_async_copy(acc_hbm.at[pl.ds(r, 1), :], acc_buf.at[pl.ds(j, 1), :], sem.at[1])

    def acc_store(j, r):
        return pltpu.make_async_copy(acc_buf.at[pl.ds(j, 1), :], acc_hbm.at[pl.ds(r, 1), :], sem.at[2])

    def start_loads(j, r):
        x_copy(j, r).start()
        acc_load(j, r).start()

    def wait_loads(j, r):
        x_copy(j, r).wait()
        acc_load(j, r).wait()

    for_rows(start_loads)
    for_rows(wait_loads)

    x = xs_buf[...].astype(BF16)
    hidden = (_silu(_dot(x, w1_ref[0])) * _dot(x, w3_ref[0])).astype(BF16)
    y = _dot(hidden, w2_ref[0])
    g = g_ref[...].reshape(rows, LANES)
    for bl in range(bb):
        rs = slice(bl * cap, (bl + 1) * cap)
        gate_row = gate_ref[bl if gate_rows > 1 else 0]
        for c in range(D // LANES):
            cs = slice(c * LANES, (c + 1) * LANES)
            acc_buf[rs, cs] += y[rs, cs] * g[rs] * gate_row[:, cs]

    for_rows(lambda j, r: acc_store(j, r).start())
    for_rows(lambda j, r: acc_store(j, r).wait())


def _experts(idx, g, gate, w1, w3, w2, h2, x_mid, B, n, cap, bb):
    E, D, FF = w1.shape
    rows = bb * cap
    gate_rows = bb if gate.shape[0] > 1 else 1
    gate_spec = (pl.BlockSpec((bb, 1, D), lambda e, b, idx: (b, 0, 0)) if gate.shape[0] > 1
                 else pl.BlockSpec((1, 1, D), lambda e, b, idx: (0, 0, 0)))
    grid_spec = pltpu.PrefetchScalarGridSpec(
        num_scalar_prefetch=1,
        grid=(E, B // bb),
        in_specs=[pl.BlockSpec((bb, 1, cap, LANES), lambda e, b, idx: (b, e, 0, 0)),
                  gate_spec,
                  pl.BlockSpec((1, D, FF), lambda e, b, idx: (e, 0, 0)),
                  pl.BlockSpec((1, D, FF), lambda e, b, idx: (e, 0, 0)),
                  pl.BlockSpec((1, FF, D), lambda e, b, idx: (e, 0, 0)),
                  pl.BlockSpec(memory_space=pl.ANY),
                  pl.BlockSpec(memory_space=pl.ANY)],
        out_specs=pl.BlockSpec(memory_space=pl.ANY),
        scratch_shapes=[pltpu.VMEM((rows, D), F32), pltpu.VMEM((rows, D), F32),
                        pltpu.SemaphoreType.DMA((3,))])
    return pl.pallas_call(
        functools.partial(_expert_kernel, bb=bb, cap=cap, n=n, gate_rows=gate_rows),
        grid_spec=grid_spec,
        out_shape=jax.ShapeDtypeStruct(x_mid.shape, F32),
        input_output_aliases={7: 0},
        compiler_params=_cparams("arbitrary", "arbitrary"), name="experts",
    )(idx, g, gate, w1, w3, w2, h2, x_mid)


def _moe(h2, aff, x_mid, gate, w1, w3, w2, B, n, bb):
    cap = max(1, CAP_FACTOR * n // N_EXPERTS)
    idx, g = _select(aff, B, n, cap)
    return _experts(idx[..., 0].reshape(-1), g, gate, w1, w3, w2, h2, x_mid, B, n, cap, bb)


def _rope_tables(S):
    half, quarter = HEAD_DIM // 2, HEAD_DIM // 4
    t = jnp.arange(S)
    freqs = 1.0 / (ROPE_BASE ** (jnp.arange(quarter, dtype=F32) / quarter))
    d = np.arange(LANES) % HEAD_DIM
    pos = jnp.where(jnp.asarray(d < half)[None, :], (t // GRID_W)[:, None], (t % GRID_W)[:, None]).astype(F32)
    ang = pos * freqs[jnp.asarray(d % quarter)][None, :]
    first = jnp.asarray((d % half) < quarter)[None, :]
    sin = jnp.sin(ang)
    return jnp.cos(ang), jnp.where(first, -sin, 0.0), jnp.where(first, 0.0, sin)


def _bias_pair_blocks(rpb):
    cq = np.arange(GRID_W)
    ck = np.arange(GRID_W)
    cstart = np.clip(cq - WIN_COLS // 2, 0, GRID_W - WIN_COLS)
    valid = (ck[None, :] >= cstart[:, None]) & (ck[None, :] < cstart[:, None] + WIN_COLS)
    col_off = np.clip(ck[None, :] - cq[:, None] + WIN_COLS - 1, 0, 2 * WIN_COLS - 2)
    ro = np.clip(np.arange(N_PAIR_BLOCKS + 1) - 8, 0, 2 * WIN_ROWS - 2)
    T = jnp.where(jnp.asarray(valid)[None, None], rpb[:, ro][:, :, col_off], NEG_INF)
    return jnp.concatenate([T[:, :-1], T[:, 1:]], axis=-1) * LOG2E


def _head_sum_tables():
    lane = np.arange(ATTN_WIDTH)
    bd = (lane[:, None] // HEAD_DIM == np.arange(LANES)[None, :]).astype(np.float32)
    return jnp.asarray(bd, BF16), jnp.asarray(bd.T, BF16)


def kernel(x, c, ctx, c_ctx, w_ada, b_ada, g_norm1, g_norm2, w_in, b_in, g_q, g_k, rpb, w_attn_o, conv_dw_w,
           conv_dw_b, conv_ln_g, conv_ln_b, w_conv_o, sc_w, w_sc_o, w_o, w_router, w_e1, w_e3, w_e2):
    B, S, D = x.shape
    C = ctx.shape[1]
    L = w_ada.shape[0]
    aw3 = 3 * ATTN_WIDTH
    prw = 2 * CONV_WIDTH + 3 * SC_WIDTH + 3 * D

    cc = jnp.zeros((8, D), F32).at[0:B].set(c).at[B].set(c_ctx)
    mod = _ada(cc, w_ada, b_ada)
    rope_tabs = _rope_tables(S)
    bd, ex = _head_sum_tables()

    xl = x.reshape(B * S, D)
    xc = ctx.reshape(B * C, D)
    b_in3 = b_in.reshape(L, 1, -1)

    for l in range(L):
        last = l == L - 1
        ml = lambda j: mod[l, 0:B, j * D:(j + 1) * D].reshape(B, 1, D)
        mc = lambda j: mod[l, B:B + 1, j * D:(j + 1) * D].reshape(1, 1, D)
        g1 = g_norm1[l].reshape(1, D)
        g2 = g_norm2[l].reshape(1, D)
        w_qkv = w_in[l][:, :aw3].astype(BF16)
        b_qkv = b_in[l][:aw3].reshape(1, aw3)
        gq = jnp.tile(g_q[l], N_HEADS).reshape(1, ATTN_WIDTH)
        gk = jnp.tile(g_k[l], N_HEADS).reshape(1, ATTN_WIDTH)
        branch_w = (w_attn_o[l].astype(BF16), w_conv_o[l].astype(BF16), w_sc_o[l].astype(BF16),
                    jnp.zeros((32, CONV_WIDTH), F32).at[:CONV_K].set(conv_dw_w[l]),
                    conv_dw_b[l].reshape(1, -1), conv_ln_g[l].reshape(1, -1), conv_ln_b[l].reshape(1, -1),
                    jnp.zeros((8, SC_WIDTH), F32).at[:SC_K].set(sc_w[l]))
        wo = w_o[l].astype(BF16)
        wr = jnp.zeros((D, LANES), F32).at[:, :N_EXPERTS].set(w_router[l]).astype(BF16)
        we = (w_e1[l].astype(BF16), w_e3[l].astype(BF16), w_e2[l].astype(BF16))

        hc = _normmod(xc, g1, mc(0), mc(1), C)
        qc, kc, vc = _qkv(hc, w_qkv, b_qkv, gq, gk, bd, ex, None, C)
        if not last:
            attn_c = _ctx_attn(qc, kc, vc, B, C)
            pr_c = _proj(hc, w_in, b_in3, l, aw3, prw)
            merged_c = _branch(attn_c, pr_c, *branch_w, C, D)
            xc_mid, hc2, aff_c = _outproj(merged_c, xc, wo, mc(2), g2, mc(3), mc(4), wr, C)
            xc = _moe(hc2, aff_c, xc_mid, mc(5), *we, B, C, B)

        h = _normmod(xl, g1, ml(0), ml(1), S)
        qn, kn, v, qr, kr = _qkv(h, w_qkv, b_qkv, gq, gk, bd, ex, rope_tabs, S)
        attn = _attn(qn, qr, kr, v, kc, vc, _bias_pair_blocks(rpb[l]), B, S, C)
        pr = _proj(h, w_in, b_in3, l, aw3, prw)
        merged = _branch(attn, pr, *branch_w, S, D)
        x_mid, h2, aff = _outproj(merged, xl, wo, ml(2), g2, ml(3), ml(4), wr, S)
        xl = _moe(h2, aff, x_mid, ml(5), *we, B, S, 1)

    return xl.reshape(B, S, D)
```

```python
import functools

import numpy as np
import jax
import jax.numpy as jnp
from jax import lax
from jax.experimental import pallas as pl
from jax.experimental.pallas import tpu as pltpu

F32 = jnp.float32
BF16 = jnp.bfloat16

GRID_W = 64
N_HEADS = 16
HEAD_DIM = 64
ATTN_WIDTH = N_HEADS * HEAD_DIM
N_HEAD_PAIRS = N_HEADS // 2
WIN_ROWS = 8
WIN_COLS = 16
ROPE_BASE = 10000.0
CONV_WIDTH = 512
CONV_K = 31
SC_WIDTH = 512
SC_K = 3
N_EXPERTS = 16
CAP_FACTOR = 2
EPS = 1e-6
NEG_INF = -1e30

LANES = 128
SUBLANES = 8
LOG2E = 1.4426950408889634
Q_SCALE = HEAD_DIM ** -0.5 * LOG2E
HALO = 16
ROW_GROUP = 4
KEY_ROWS = 12
N_PAIR_BLOCKS = 26
VMEM_LIMIT = 56 * 1024 * 1024


def _cparams(*sem):
    return pltpu.CompilerParams(dimension_semantics=sem, vmem_limit_bytes=VMEM_LIMIT)


def _dot(a, b):
    return jnp.dot(a, b, preferred_element_type=F32)


def _dot_t(a, b):
    return lax.dot_general(a, b, (((1,), (1,)), ((), ())), preferred_element_type=F32)


def _sigmoid(x):
    return 0.5 * jnp.tanh(0.5 * x) + 0.5


def _silu(x):
    return x * _sigmoid(x)


def _ada_kernel(cc_ref, w_ref, b_ref, o_ref):
    s = _silu(cc_ref[...]).astype(BF16)
    o_ref[0] = _dot(s, w_ref[0].astype(BF16)) + b_ref[0]


def _ada(cc, w_ada, b_ada):
    L, D, W = w_ada.shape
    tn = 1024
    return pl.pallas_call(
        _ada_kernel,
        grid=(L, W // tn),
        in_specs=[pl.BlockSpec((8, D), lambda l, j: (0, 0)),
                  pl.BlockSpec((1, D, tn), lambda l, j: (l, 0, j)),
                  pl.BlockSpec((1, 1, tn), lambda l, j: (l, 0, j))],
        out_specs=pl.BlockSpec((1, 8, tn), lambda l, j: (l, 0, j)),
        out_shape=jax.ShapeDtypeStruct((L, 8, W), F32),
        compiler_params=_cparams("arbitrary", "arbitrary"),
        name="ada",
    )(cc, w_ada, b_ada.reshape(L, 1, W))


def _mod_spec(nmod, tiles_per_mod, D):
    if nmod == 1:
        return pl.BlockSpec((1, 1, D), lambda i: (0, 0, 0))
    return pl.BlockSpec((1, 1, D), lambda i: (i // tiles_per_mod, 0, 0))


def _normmod_kernel(x_ref, g_ref, sh_ref, sc_ref, h_ref):
    x = x_ref[...]
    ms = jnp.mean(x * x, axis=-1, keepdims=True)
    y = x * lax.rsqrt(ms + EPS) * g_ref[...]
    h_ref[...] = (y * (1.0 + sc_ref[0]) + sh_ref[0]).astype(BF16)


def _normmod(x, g, shift, scale, seq_len):
    rows, D = x.shape
    tm = min(512, seq_len)
    mspec = _mod_spec(shift.shape[0], seq_len // tm, D)
    xspec = pl.BlockSpec((tm, D), lambda i: (i, 0))
    return pl.pallas_call(
        _normmod_kernel, grid=(rows // tm,),
        in_specs=[xspec, pl.BlockSpec((1, D), lambda i: (0, 0)), mspec, mspec],
        out_specs=xspec, out_shape=jax.ShapeDtypeStruct((rows, D), BF16),
        compiler_params=_cparams("arbitrary"), name="normmod",
    )(x, g, shift, scale)


def _qkv_kernel(*refs, rope):
    if rope:
        (h_ref, w_ref, b_ref, gq_ref, gk_ref, bd_ref, ex_ref, cos_ref, sa_ref, sb_ref,
         qn_ref, kn_ref, v_ref, qr_ref, kr_ref) = refs
    else:
        h_ref, w_ref, b_ref, gq_ref, gk_ref, bd_ref, ex_ref, qn_ref, kn_ref, v_ref = refs
    h = h_ref[...]
    aw = ATTN_WIDTH

    def proj(s):
        return _dot(h, w_ref[0, :, s * aw:(s + 1) * aw]) + b_ref[:, s * aw:(s + 1) * aw]

    def headnorm(a, g_ref):
        ss = _dot((a * a).astype(BF16), bd_ref[...])
        inv = lax.rsqrt(ss * (1.0 / HEAD_DIM) + EPS)
        inv_hi = inv.astype(BF16)
        inv_lo = (inv - inv_hi.astype(F32)).astype(BF16)
        full = _dot(inv_hi, ex_ref[...]) + _dot(inv_lo, ex_ref[...])
        return a * full * g_ref[...]

    def store_plain(xn, out_ref):
        for c in range(N_HEAD_PAIRS):
            out_ref[c] = xn[:, c * LANES:(c + 1) * LANES].astype(BF16)

    def store_rope(xn, out_ref):
        cos, sa, sb = cos_ref[...], sa_ref[...], sb_ref[...]
        for c in range(N_HEAD_PAIRS):
            xc = xn[:, c * LANES:(c + 1) * LANES]
            r = xc * cos + pltpu.roll(xc, LANES - 16, 1) * sa + pltpu.roll(xc, 16, 1) * sb
            out_ref[c] = r.astype(BF16)

    qn = headnorm(proj(0), gq_ref) * Q_SCALE
    store_plain(qn, qn_ref)
    if rope:
        store_rope(qn, qr_ref)
    kn = headnorm(proj(1), gk_ref)
    store_plain(kn, kn_ref)
    if rope:
        store_rope(kn, kr_ref)
    store_plain(proj(2), v_ref)


def _qkv(h, w, l, b, gq, gk, bd, ex, rope_tabs, seq_len):
    rows, D = h.shape
    tm = min(512, seq_len)
    rope = rope_tabs is not None
    const = lambda i: (0, 0)
    in_specs = [pl.BlockSpec((tm, D), lambda i: (i, 0)),
                pl.BlockSpec((1, D, 3 * ATTN_WIDTH), lambda i: (l, 0, 0), pipeline_mode=pl.Buffered(1)),
                pl.BlockSpec((1, 3 * ATTN_WIDTH), const),
                pl.BlockSpec((1, ATTN_WIDTH), const),
                pl.BlockSpec((1, ATTN_WIDTH), const),
                pl.BlockSpec((ATTN_WIDTH, LANES), const),
                pl.BlockSpec((LANES, ATTN_WIDTH), const)]
    ins = [h, w, b, gq, gk, bd, ex]
    n_out = 3
    if rope:
        tiles_per_seq = seq_len // tm
        tspec = pl.BlockSpec((tm, LANES), lambda i: (i % tiles_per_seq, 0))
        in_specs += [tspec, tspec, tspec]
        ins += list(rope_tabs)
        n_out = 5
    ospec = pl.BlockSpec((N_HEAD_PAIRS, tm, LANES), lambda i: (0, i, 0))
    return pl.pallas_call(
        functools.partial(_qkv_kernel, rope=rope),
        grid=(rows // tm,), in_specs=in_specs,
        out_specs=[ospec] * n_out,
        out_shape=[jax.ShapeDtypeStruct((N_HEAD_PAIRS, rows, LANES), BF16)] * n_out,
        compiler_params=_cparams("arbitrary"), name="qkv",
    )(*ins)


def _proj_kernel(h_ref, w_ref, b_ref, o_ref, wb_ref):
    @pl.when(pl.program_id(1) == 0)
    def _():
        wb_ref[...] = w_ref[0].astype(BF16)

    o_ref[...] = (_dot(h_ref[...], wb_ref[...]) + b_ref[0]).astype(o_ref.dtype)


def _proj(h, w, b, l, col0, ncols):
    rows, D = h.shape
    tm = min(2048, rows)
    tn = 512
    j0 = col0 // tn
    return pl.pallas_call(
        _proj_kernel, grid=(ncols // tn, rows // tm),
        in_specs=[pl.BlockSpec((tm, D), lambda j, i: (i, 0)),
                  pl.BlockSpec((1, D, tn), lambda j, i: (l, 0, j + j0)),
                  pl.BlockSpec((1, 1, tn), lambda j, i: (l, 0, j + j0))],
        out_specs=pl.BlockSpec((tm, tn), lambda j, i: (i, j)),
        out_shape=jax.ShapeDtypeStruct((rows, ncols), BF16),
        scratch_shapes=[pltpu.VMEM((D, tn), BF16)],
        compiler_params=_cparams("arbitrary", "arbitrary"), name="proj",
    )(h, w, b)


def _attn_kernel(qn_ref, qr_ref, kr_ref, v_ref, kc_ref, vc_ref, fp_ref, o_ref, *, rows):
    gq = ROW_GROUP * GRID_W
    nk = KEY_ROWS * GRID_W
    lane = lax.broadcasted_iota(jnp.int32, (1, LANES), 1)
    head_mask = (lane < HEAD_DIM, lane >= HEAD_DIM)
    key_row = lax.broadcasted_iota(jnp.int32, (1, nk), 1) // GRID_W
    kc = kc_ref[0]
    vc = [jnp.where(head_mask[hh], vc_ref[0], 1) for hh in range(2)]

    def group(gi, carry):
        r0 = gi * ROW_GROUP
        ws = jnp.clip(r0 - WIN_ROWS // 2, 0, rows - KEY_ROWS)
        tok0 = pl.multiple_of(gi * gq, gq)
        key0 = pl.multiple_of(ws * GRID_W, GRID_W)
        qr = qr_ref[0, pl.ds(tok0, gq), :]
        qn = qn_ref[0, pl.ds(tok0, gq), :]
        kw = kr_ref[0, pl.ds(key0, nk), :]
        vw = v_ref[0, pl.ds(key0, nk), :]
        scores = [(_dot_t(jnp.where(head_mask[hh], qr, 0), kw),
                   _dot_t(jnp.where(head_mask[hh], qn, 0), kc))
                  for hh in range(2)]
        outs = []
        for hh in range(2):
            s_loc, s_ctx = scores[hh]
            slabs = []
            for i in range(ROW_GROUP):
                n0 = ws - r0 - i + (WIN_ROWS - 1) + 8
                bias = jnp.concatenate([fp_ref[hh, n0 + 2 * jp] for jp in range(KEY_ROWS // 2)], axis=1)
                lo = jnp.clip(r0 + i - WIN_ROWS // 2, 0, rows - WIN_ROWS) - ws
                valid = (key_row >= lo) & (key_row < lo + WIN_ROWS)
                slabs.append(jnp.where(valid, s_loc[i * GRID_W:(i + 1) * GRID_W] + bias, NEG_INF))
            s_loc = jnp.concatenate(slabs, axis=0)
            m = jnp.maximum(jnp.max(s_loc, axis=-1, keepdims=True), jnp.max(s_ctx, axis=-1, keepdims=True))
            p_loc = jnp.exp2(s_loc - m).astype(BF16)
            p_ctx = jnp.exp2(s_ctx - m).astype(BF16)
            o = _dot(p_loc, jnp.where(head_mask[hh], vw, 1)) + _dot(p_ctx, vc[hh])
            outs.append(o / pltpu.roll(o, HEAD_DIM, 1))
        o_ref[0, pl.ds(tok0, gq), :] = jnp.where(head_mask[0], outs[0], outs[1]).astype(BF16)
        return carry

    lax.fori_loop(0, rows // ROW_GROUP, group, 0, unroll=2)


def _attn(qn, qr, kr, v, kc, vc, fp, B, S, C):
    rows = S // GRID_W
    tok = pl.BlockSpec((1, S, LANES), lambda b, p: (p, b, 0))
    ctx = pl.BlockSpec((1, C, LANES), lambda b, p: (p, b, 0))
    return pl.pallas_call(
        functools.partial(_attn_kernel, rows=rows),
        grid=(B, N_HEAD_PAIRS),
        in_specs=[tok, tok, tok, tok, ctx, ctx,
                  pl.BlockSpec((2, N_PAIR_BLOCKS, GRID_W, LANES), lambda b, p: (p, 0, 0, 0))],
        out_specs=tok,
        out_shape=jax.ShapeDtypeStruct((N_HEAD_PAIRS, B * S, LANES), BF16),
        compiler_params=_cparams("arbitrary", "arbitrary"), name="attn",
    )(qn, qr, kr, v, kc, vc, fp)


def _ctx_attn_kernel(q_ref, k_ref, v_ref, o_ref):
    lane = lax.broadcasted_iota(jnp.int32, (1, LANES), 1)
    head_mask = (lane < HEAD_DIM, lane >= HEAD_DIM)
    q, k, v = q_ref[0], k_ref[0], v_ref[0]
    outs = []
    for hh in range(2):
        s = _dot_t(jnp.where(head_mask[hh], q, 0), k)
        m = jnp.max(s, axis=-1, keepdims=True)
        p = jnp.exp2(s - m)
        outs.append(_dot(p.astype(BF16), v) / jnp.sum(p, axis=-1, keepdims=True))
    o_ref[0] = jnp.where(head_mask[0], outs[0], outs[1]).astype(BF16)


def _ctx_attn(q, k, v, B, C):
    spec = pl.BlockSpec((1, C, LANES), lambda b, p: (p, b, 0))
    return pl.pallas_call(
        _ctx_attn_kernel, grid=(B, N_HEAD_PAIRS),
        in_specs=[spec, spec, spec], out_specs=spec,
        out_shape=jax.ShapeDtypeStruct((N_HEAD_PAIRS, B * C, LANES), BF16),
        compiler_params=_cparams("arbitrary", "arbitrary"), name="ctx_attn",
    )(q, k, v)


def _branch_kernel(attn_ref, pr_ref, pcp_ref, pcn_ref, wa_ref, wc_ref, ws_ref,
                   dww_ref, dwb_ref, lng_ref, lnb_ref, scw_ref, o_ref, glu_ext, sc_ext, *, tm, seq_len, D):
    i = pl.program_id(0)
    pos0 = (i * tm) % seq_len
    keep_prev = (pos0 != 0).astype(F32)
    keep_next = (pos0 + tm != seq_len).astype(F32)
    cw, sw = CONV_WIDTH, SC_WIDTH

    def glu(blk):
        return blk[:, 0:cw].astype(F32) * _sigmoid(blk[:, cw:2 * cw].astype(F32))

    def gated_x(blk):
        o = 2 * cw + sw
        return blk[:, o:o + sw].astype(F32) * blk[:, o + sw:o + 2 * sw].astype(F32)

    pcw = 2 * cw + 3 * sw
    cur, prv, nxt = pr_ref[:, 0:pcw], pcp_ref[...], pcn_ref[...]
    glu_ext[0:HALO, :] = glu(prv) * keep_prev
    glu_ext[HALO:HALO + tm, :] = glu(cur)
    glu_ext[HALO + tm:2 * HALO + tm, :] = glu(nxt) * keep_next
    sc_ext[0:HALO, :] = gated_x(prv) * keep_prev
    sc_ext[HALO:HALO + tm, :] = gated_x(cur)
    sc_ext[HALO + tm:2 * HALO + tm, :] = gated_x(nxt) * keep_next

    chunk = 32
    window = chunk + 2 * HALO
    conv_rows = []
    for r in range(0, tm, chunk):
        win = glu_ext[r:r + window, :]
        acc = jnp.zeros((chunk, cw), F32)
        for res in range(SUBLANES):
            shifted = win if res == 0 else pltpu.roll(win, window - res, 0)
            for k in range(CONV_K):
                off = HALO - CONV_K // 2 + k
                if off % SUBLANES == res:
                    acc = acc + dww_ref[k:k + 1, :] * shifted[off - res:off - res + chunk, :]
        conv_rows.append(acc)
    u = jnp.concatenate(conv_rows, axis=0) + dwb_ref[...]
    mu = jnp.mean(u, axis=-1, keepdims=True)
    uc = u - mu
    var = jnp.mean(uc * uc, axis=-1, keepdims=True)
    u = _silu(uc * lax.rsqrt(var + EPS) * lng_ref[...] + lnb_ref[...])
    y_b = _dot(u.astype(BF16), wc_ref[0])

    c3 = jnp.zeros((tm, sw), F32)
    for k in range(SC_K):
        base = HALO - SC_K // 2 + k
        c3 = c3 + scw_ref[k:k + 1, :] * sc_ext[base:base + tm, :]
    sc_b = cur[:, 2 * cw:2 * cw + sw].astype(F32)
    y_c = _dot((sc_b * c3).astype(BF16), ws_ref[0])

    attn = jnp.concatenate([attn_ref[c] for c in range(N_HEAD_PAIRS)], axis=1)
    y_a = _dot(attn, wa_ref[0])

    g = pr_ref[:, pcw:pcw + 3 * D]
    merged = (_sigmoid(g[:, 0:D].astype(F32)) * y_a
              + _sigmoid(g[:, D:2 * D].astype(F32)) * y_b
              + _sigmoid(g[:, 2 * D:3 * D].astype(F32)) * y_c)
    o_ref[...] = merged.astype(BF16)


def _branch(attn, pr, wa, wc, ws, l, dww, dwb, lng, lnb, scw, seq_len, D):
    rows, prw = pr.shape
    tm = 256
    pcw = 2 * CONV_WIDTH + 3 * SC_WIDTH
    nhalo = rows // HALO
    const = lambda i: (0, 0)
    layer = lambda i: (l, 0, 0)
    return pl.pallas_call(
        functools.partial(_branch_kernel, tm=tm, seq_len=seq_len, D=D),
        grid=(rows // tm,),
        in_specs=[pl.BlockSpec((N_HEAD_PAIRS, tm, LANES), lambda i: (0, i, 0)),
                  pl.BlockSpec((tm, prw), lambda i: (i, 0)),
                  pl.BlockSpec((HALO, pcw), lambda i: (jnp.maximum(i * (tm // HALO) - 1, 0), 0)),
                  pl.BlockSpec((HALO, pcw), lambda i: (jnp.minimum((i + 1) * (tm // HALO), nhalo - 1), 0)),
                  pl.BlockSpec((1, ATTN_WIDTH, D), layer),
                  pl.BlockSpec((1, CONV_WIDTH, D), layer),
                  pl.BlockSpec((1, SC_WIDTH, D), layer),
                  pl.BlockSpec((32, CONV_WIDTH), const),
                  pl.BlockSpec((1, CONV_WIDTH), const),
                  pl.BlockSpec((1, CONV_WIDTH), const),
                  pl.BlockSpec((1, CONV_WIDTH), const),
                  pl.BlockSpec((8, SC_WIDTH), const)],
        out_specs=pl.BlockSpec((tm, D), lambda i: (i, 0)),
        out_shape=jax.ShapeDtypeStruct((rows, D), BF16),
        scratch_shapes=[pltpu.VMEM((tm + 2 * HALO, CONV_WIDTH), F32),
                        pltpu.VMEM((tm + 2 * HALO, SC_WIDTH), F32)],
        compiler_params=_cparams("arbitrary"), name="branch",
    )(attn, pr, pr, pr, wa, wc, ws, dww, dwb, lng, lnb, scw)


def _outproj_kernel(m_ref, x_ref, wo_ref, gate_ref, g2_ref, sh_ref, sc_ref, wr_ref, xmid_ref, h2_ref, aff_ref):
    x = x_ref[...] + gate_ref[0] * _dot(m_ref[...], wo_ref[0])
    xmid_ref[...] = x
    ms = jnp.mean(x * x, axis=-1, keepdims=True)
    y = x * lax.rsqrt(ms + EPS) * g2_ref[...]
    h2 = y * (1.0 + sc_ref[0]) + sh_ref[0]
    h2_ref[...] = h2
    logits = _dot(h2.astype(BF16), wr_ref[...])
    lane = lax.broadcasted_iota(jnp.int32, logits.shape, 1)
    logits = jnp.where(lane < N_EXPERTS, logits, NEG_INF)
    e = jnp.exp(logits - jnp.max(logits, axis=-1, keepdims=True))
    aff_ref[...] = e / jnp.sum(e, axis=-1, keepdims=True)


def _outproj(merged, x, wo, l, gate, g2, shift, scale, wr, seq_len):
    rows, D = x.shape
    tm = 256
    mspec = _mod_spec(gate.shape[0], seq_len // tm, D)
    row = lambda i: (i, 0)
    const = lambda i: (0, 0)
    return pl.pallas_call(
        _outproj_kernel, grid=(rows // tm,),
        in_specs=[pl.BlockSpec((tm, D), row), pl.BlockSpec((tm, D), row),
                  pl.BlockSpec((1, D, D), lambda i: (l, 0, 0)), mspec,
                  pl.BlockSpec((1, D), const), mspec, mspec,
                  pl.BlockSpec((D, LANES), const)],
        out_specs=[pl.BlockSpec((tm, D), row), pl.BlockSpec((tm, D), row), pl.BlockSpec((tm, LANES), row)],
        out_shape=[jax.ShapeDtypeStruct((rows, D), F32), jax.ShapeDtypeStruct((rows, D), F32),
                   jax.ShapeDtypeStruct((rows, LANES), F32)],
        compiler_params=_cparams("arbitrary"), name="outproj",
    )(merged, x, wo, gate, g2, shift, scale, wr)


MAX_BISECT = 160
POS_LANE = 3 * N_EXPERTS


def _select_kernel(aff_ref, tri_ref, tv_ref, idx_ref, g_ref, s1_ref, r_ref, *, n, cap, blk, tchunk):
    E = N_EXPERTS
    aff = aff_ref[0]
    a = aff.T[0:E, :]
    capf = float(cap)

    def cond(st):
        it, _, _, ndone = st
        return jnp.logical_and(it < MAX_BISECT, ndone < float(E))

    def in_range(lo, hi):
        return jnp.logical_and(a >= lo, a < hi)

    def body(st):
        it, lo, hi, _ = st
        mid = 0.5 * (lo + hi)
        cnt = jnp.sum((a >= mid).astype(F32), axis=-1, keepdims=True)
        ge = cnt >= capf
        lo = jnp.where(ge, mid, lo)
        hi = jnp.where(ge, hi, mid)
        r = in_range(lo, hi)
        vmin = jnp.min(jnp.where(r, a, 4.0), axis=-1, keepdims=True)
        vmax = jnp.max(jnp.where(r, a, -1.0), axis=-1, keepdims=True)
        ndone = jnp.sum((vmin == vmax).astype(F32))
        return it + 1, lo, hi, ndone

    _, lo, hi, _ = lax.while_loop(cond, body, (jnp.int32(0), jnp.zeros((E, 1), F32), jnp.full((E, 1), 2.0, F32),
                                               jnp.float32(0.0)))
    thr = jnp.max(jnp.where(in_range(lo, hi), a, -1.0), axis=-1, keepdims=True)

    def cumsum(x):
        outs, carry = [], jnp.zeros((E, 1), F32)
        for c in range(n // blk):
            part = _dot(x[:, c * blk:(c + 1) * blk].astype(BF16), tri_ref[...]) + carry
            outs.append(part)
            carry = part[:, blk - 1:blk]
        return jnp.concatenate(outs, axis=1)

    gt = a > thr
    eq = (a == thr).astype(F32)
    need = capf - jnp.sum(gt.astype(F32), axis=-1, keepdims=True)
    eq_rank = cumsum(eq) - eq
    sel = jnp.logical_or(gt, jnp.logical_and(eq > 0.0, eq_rank < need)).astype(F32)
    s1_ref[...] = sel * cumsum(sel)

    a_hi = aff.astype(BF16).astype(F32)
    rest = aff - a_hi
    a_mid = rest.astype(BF16).astype(F32)
    a_lo = rest - a_mid
    r_ref[...] = (a_hi + pltpu.roll(a_mid, E, 1) + pltpu.roll(a_lo, 2 * E, 1) + tv_ref[...]).astype(BF16)

    slot_ids = lax.broadcasted_iota(jnp.int32, (cap, 1), 0).astype(F32) + 1.0
    lane = lax.broadcasted_iota(jnp.int32, (1, LANES), 1)

    def per_expert(e, carry):
        row = s1_ref[pl.ds(e, 1), :]
        acc = jnp.zeros((cap, LANES), F32)
        for c in range(n // tchunk):
            onehot = (row[:, c * tchunk:(c + 1) * tchunk] == slot_ids).astype(BF16)
            acc = acc + _dot(onehot, r_ref[c * tchunk:(c + 1) * tchunk, :])
        mine = jnp.logical_and(lane % E == e, lane < POS_LANE)
        g = jnp.sum(jnp.where(mine, acc, 0.0), axis=1, keepdims=True)
        tok = jnp.sum(jnp.where(lane == POS_LANE, acc * 64.0, jnp.where(lane == POS_LANE + 1, acc, 0.0)),
                      axis=1, keepdims=True)
        g_ref[0, e] = jnp.broadcast_to(g, (cap, LANES))
        idx_ref[0, e] = jnp.broadcast_to(tok, (cap, LANES)).astype(jnp.int32)
        return carry

    lax.fori_loop(0, E, per_expert, 0)


def _select(aff, B, n, cap):
    blk = min(512, n)
    tchunk = min(1024, n)
    tri = (np.arange(blk)[:, None] <= np.arange(blk)[None, :]).astype(np.float32)
    tv = np.zeros((n, LANES), np.float32)
    tv[:, POS_LANE] = np.arange(n) // 64
    tv[:, POS_LANE + 1] = np.arange(n) % 64
    out = pl.BlockSpec((1, N_EXPERTS, cap, LANES), lambda b: (b, 0, 0, 0))
    return pl.pallas_call(
        functools.partial(_select_kernel, n=n, cap=cap, blk=blk, tchunk=tchunk),
        grid=(B,),
        in_specs=[pl.BlockSpec((1, n, LANES), lambda b: (b, 0, 0)),
                  pl.BlockSpec((blk, blk), lambda b: (0, 0)),
                  pl.BlockSpec((n, LANES), lambda b: (0, 0))],
        out_specs=[out, out],
        out_shape=[jax.ShapeDtypeStruct((B, N_EXPERTS, cap, LANES), jnp.int32),
                   jax.ShapeDtypeStruct((B, N_EXPERTS, cap, LANES), F32)],
        scratch_shapes=[pltpu.VMEM((N_EXPERTS, n), F32), pltpu.VMEM((n, LANES), BF16)],
        compiler_params=_cparams("arbitrary"), name="select",
    )(aff.reshape(B, n, LANES), jnp.asarray(tri, BF16), jnp.asarray(tv))


DMA_UNROLL = 8

def _expert_kernel(idx_ref, g_ref, gate_ref, w1_ref, w3_ref, w2_ref, h_hbm, acc_in, acc_hbm,
                   xs_buf, acc_buf, sem, *, bb, cap, n, gate_rows):
    del acc_in
    e = pl.program_id(0)
    b0 = pl.program_id(1) * bb
    rows = bb * cap
    D = xs_buf.shape[1]

    def for_rows(fn):
        for bl in range(bb):
            b = b0 + bl
            base = (b * N_EXPERTS + e) * cap

            def body(s, carry, bl=bl, b=b, base=base):
                fn(bl * cap + s, b * n + idx_ref[base + s])
                return carry

            lax.fori_loop(0, cap, body, 0, unroll=DMA_UNROLL)

    def x_copy(j, r):
        return pltpu.make_async_copy(h_hbm.at[pl.ds(r, 1), :], xs_buf.at[pl.ds(j, 1), :], sem.at[0])

    def acc_load(j, r):
        return pltpu.make_async_copy(acc_hbm.at[pl.ds(r, 1), :], acc_buf.at[pl.ds(j, 1), :], sem.at[1])

    def acc_store(j, r):
        return pltpu.make_async_copy(acc_buf.at[pl.ds(j, 1), :], acc_hbm.at[pl.ds(r, 1), :], sem.at[2])

    def start_loads(j, r):
        x_copy(j, r).start()
        acc_load(j, r).start()

    def wait_loads(j, r):
        x_copy(j, r).wait()
        acc_load(j, r).wait()

    for_rows(start_loads)
    for_rows(wait_loads)

    x = xs_buf[...].astype(BF16)
    hidden = (_silu(_dot(x, w1_ref[0, 0])) * _dot(x, w3_ref[0, 0])).astype(BF16)
    y = _dot(hidden, w2_ref[0, 0])
    g = g_ref[...].reshape(rows, LANES)
    for bl in range(bb):
        rs = slice(bl * cap, (bl + 1) * cap)
        gate_row = gate_ref[bl if gate_rows > 1 else 0]
        for c in range(D // LANES):
            cs = slice(c * LANES, (c + 1) * LANES)
            acc_buf[rs, cs] += y[rs, cs] * g[rs] * gate_row[:, cs]

    for_rows(lambda j, r: acc_store(j, r).start())
    for_rows(lambda j, r: acc_store(j, r).wait())


def _experts(idx, g, gate, w1, w3, w2, l, h2, x_mid, B, n, cap, bb):
    _, E, D, FF = w1.shape
    rows = bb * cap
    gate_rows = bb if gate.shape[0] > 1 else 1
    gate_spec = (pl.BlockSpec((bb, 1, D), lambda e, b, idx: (b, 0, 0)) if gate.shape[0] > 1
                 else pl.BlockSpec((1, 1, D), lambda e, b, idx: (0, 0, 0)))
    grid_spec = pltpu.PrefetchScalarGridSpec(
        num_scalar_prefetch=1,
        grid=(E, B // bb),
        in_specs=[pl.BlockSpec((bb, 1, cap, LANES), lambda e, b, idx: (b, e, 0, 0)),
                  gate_spec,
                  pl.BlockSpec((1, 1, D, FF), lambda e, b, idx: (l, e, 0, 0)),
                  pl.BlockSpec((1, 1, D, FF), lambda e, b, idx: (l, e, 0, 0)),
                  pl.BlockSpec((1, 1, FF, D), lambda e, b, idx: (l, e, 0, 0)),
                  pl.BlockSpec(memory_space=pl.ANY),
                  pl.BlockSpec(memory_space=pl.ANY)],
        out_specs=pl.BlockSpec(memory_space=pl.ANY),
        scratch_shapes=[pltpu.VMEM((rows, D), F32), pltpu.VMEM((rows, D), F32),
                        pltpu.SemaphoreType.DMA((3,))])
    return pl.pallas_call(
        functools.partial(_expert_kernel, bb=bb, cap=cap, n=n, gate_rows=gate_rows),
        grid_spec=grid_spec,
        out_shape=jax.ShapeDtypeStruct(x_mid.shape, F32),
        input_output_aliases={7: 0},
        compiler_params=_cparams("arbitrary", "arbitrary"), name="experts",
    )(idx, g, gate, w1, w3, w2, h2, x_mid)


def _moe(h2, aff, x_mid, gate, w1, w3, w2, l, B, n, bb):
    cap = max(1, CAP_FACTOR * n // N_EXPERTS)
    idx, g = _select(aff, B, n, cap)
    return _experts(idx[..., 0].reshape(-1), g, gate, w1, w3, w2, l, h2, x_mid, B, n, cap, bb)


def _rope_tables(S):
    half, quarter = HEAD_DIM // 2, HEAD_DIM // 4
    t = jnp.arange(S)
    freqs = 1.0 / (ROPE_BASE ** (jnp.arange(quarter, dtype=F32) / quarter))
    d = np.arange(LANES) % HEAD_DIM
    pos = jnp.where(jnp.asarray(d < half)[None, :], (t // GRID_W)[:, None], (t % GRID_W)[:, None]).astype(F32)
    ang = pos * freqs[jnp.asarray(d % quarter)][None, :]
    first = jnp.asarray((d % half) < quarter)[None, :]
    sin = jnp.sin(ang)
    return jnp.cos(ang), jnp.where(first, -sin, 0.0), jnp.where(first, 0.0, sin)


def _bias_pair_blocks(rpb):
    cq = np.arange(GRID_W)
    ck = np.arange(GRID_W)
    cstart = np.clip(cq - WIN_COLS // 2, 0, GRID_W - WIN_COLS)
    valid = (ck[None, :] >= cstart[:, None]) & (ck[None, :] < cstart[:, None] + WIN_COLS)
    col_off = np.clip(ck[None, :] - cq[:, None] + WIN_COLS - 1, 0, 2 * WIN_COLS - 2)
    place = (col_off[None] == np.arange(2 * WIN_COLS - 1)[:, None, None]).astype(np.float32)
    ro = np.clip(np.arange(N_PAIR_BLOCKS + 1) - 8, 0, 2 * WIN_ROWS - 2)
    T = jnp.einsum('hrd,dck->hrck', rpb[:, ro], jnp.asarray(place), precision=lax.Precision.HIGHEST)
    T = jnp.where(jnp.asarray(valid)[None, None], T * LOG2E, NEG_INF)
    return jnp.concatenate([T[:, :-1], T[:, 1:]], axis=-1)


def _head_sum_tables():
    lane = np.arange(ATTN_WIDTH)
    bd = (lane[:, None] // HEAD_DIM == np.arange(LANES)[None, :]).astype(np.float32)
    return jnp.asarray(bd, BF16), jnp.asarray(bd.T, BF16)


def kernel(x, c, ctx, c_ctx, w_ada, b_ada, g_norm1, g_norm2, w_in, b_in, g_q, g_k, rpb, w_attn_o, conv_dw_w,
           conv_dw_b, conv_ln_g, conv_ln_b, w_conv_o, sc_w, w_sc_o, w_o, w_router, w_e1, w_e3, w_e2):
    B, S, D = x.shape
    C = ctx.shape[1]
    L = w_ada.shape[0]
    aw3 = 3 * ATTN_WIDTH
    prw = 2 * CONV_WIDTH + 3 * SC_WIDTH + 3 * D

    cc = jnp.zeros((8, D), F32).at[0:B].set(c).at[B].set(c_ctx)
    mod = _ada(cc, w_ada, b_ada)
    rope_tabs = _rope_tables(S)
    bd, ex = _head_sum_tables()

    xl = x.reshape(B * S, D)
    xc = ctx.reshape(B * C, D)
    b_in3 = b_in.reshape(L, 1, -1)
    w_qkv = w_in[:, :, :aw3].astype(BF16)
    wa, wc, ws, wo = (w.astype(BF16) for w in (w_attn_o, w_conv_o, w_sc_o, w_o))
    we = (w_e1.astype(BF16), w_e3.astype(BF16), w_e2.astype(BF16))

    for l in range(L):
        last = l == L - 1
        ml = lambda j: mod[l, 0:B, j * D:(j + 1) * D].reshape(B, 1, D)
        mc = lambda j: mod[l, B:B + 1, j * D:(j + 1) * D].reshape(1, 1, D)
        g1 = g_norm1[l].reshape(1, D)
        g2 = g_norm2[l].reshape(1, D)
        b_qkv = b_in[l][:aw3].reshape(1, aw3)
        gq = jnp.tile(g_q[l], N_HEADS).reshape(1, ATTN_WIDTH)
        gk = jnp.tile(g_k[l], N_HEADS).reshape(1, ATTN_WIDTH)
        branch_w = (wa, wc, ws, l,
                    jnp.zeros((32, CONV_WIDTH), F32).at[:CONV_K].set(conv_dw_w[l]),
                    conv_dw_b[l].reshape(1, -1), conv_ln_g[l].reshape(1, -1), conv_ln_b[l].reshape(1, -1),
                    jnp.zeros((8, SC_WIDTH), F32).at[:SC_K].set(sc_w[l]))
        wr = jnp.zeros((D, LANES), F32).at[:, :N_EXPERTS].set(w_router[l]).astype(BF16)

        hc = _normmod(xc, g1, mc(0), mc(1), C)
        qc, kc, vc = _qkv(hc, w_qkv, l, b_qkv, gq, gk, bd, ex, None, C)
        if not last:
            attn_c = _ctx_attn(qc, kc, vc, B, C)
            pr_c = _proj(hc, w_in, b_in3, l, aw3, prw)
            merged_c = _branch(attn_c, pr_c, *branch_w, C, D)
            xc_mid, hc2, aff_c = _outproj(merged_c, xc, wo, l, mc(2), g2, mc(3), mc(4), wr, C)
            xc = _moe(hc2, aff_c, xc_mid, mc(5), *we, l, B, C, B)

        h = _normmod(xl, g1, ml(0), ml(1), S)
        qn, kn, v, qr, kr = _qkv(h, w_qkv, l, b_qkv, gq, gk, bd, ex, rope_tabs, S)
        attn = _attn(qn, qr, kr, v, kc, vc, _bias_pair_blocks(rpb[l]), B, S, C)
        pr = _proj(h, w_in, b_in3, l, aw3, prw)
        merged = _branch(attn, pr, *branch_w, S, D)
        x_mid, h2, aff = _outproj(merged, xl, wo, l, ml(2), g2, ml(3), ml(4), wr, S)
        xl = _moe(h2, aff, x_mid, ml(5), *we, l, B, S, 1)

    return xl.reshape(B, S, D)
```

```python
import functools

import numpy as np
import jax
import jax.numpy as jnp
from jax import lax
from jax.experimental import pallas as pl
from jax.experimental.pallas import tpu as pltpu

F32 = jnp.float32
BF16 = jnp.bfloat16

GRID_W = 64
N_HEADS = 16
HEAD_DIM = 64
ATTN_WIDTH = N_HEADS * HEAD_DIM
N_HEAD_PAIRS = N_HEADS // 2
WIN_ROWS = 8
WIN_COLS = 16
ROPE_BASE = 10000.0
CONV_WIDTH = 512
CONV_K = 31
SC_WIDTH = 512
SC_K = 3
N_EXPERTS = 16
CAP_FACTOR = 2
EPS = 1e-6
NEG_INF = -1e30

LANES = 128
SUBLANES = 8
LOG2E = 1.4426950408889634
Q_SCALE = HEAD_DIM ** -0.5 * LOG2E
HALO = 16
ROW_GROUP = 4
KEY_ROWS = 12
N_PAIR_BLOCKS = 26
VMEM_LIMIT = 56 * 1024 * 1024


def _cparams(*sem):
    return pltpu.CompilerParams(dimension_semantics=sem, vmem_limit_bytes=VMEM_LIMIT)


def _dot(a, b):
    return jnp.dot(a, b, preferred_element_type=F32)


def _dot_t(a, b):
    return lax.dot_general(a, b, (((1,), (1,)), ((), ())), preferred_element_type=F32)


def _sigmoid(x):
    return 0.5 * jnp.tanh(0.5 * x) + 0.5


def _silu(x):
    return x * _sigmoid(x)


def _ada_kernel(cc_ref, w_ref, b_ref, o_ref):
    s = _silu(cc_ref[...]).astype(BF16)
    o_ref[0] = _dot(s, w_ref[0].astype(BF16)) + b_ref[0]


def _ada(cc, w_ada, b_ada):
    L, D, W = w_ada.shape
    tn = 1024
    return pl.pallas_call(
        _ada_kernel,
        grid=(L, W // tn),
        in_specs=[pl.BlockSpec((8, D), lambda l, j: (0, 0)),
                  pl.BlockSpec((1, D, tn), lambda l, j: (l, 0, j)),
                  pl.BlockSpec((1, 1, tn), lambda l, j: (l, 0, j))],
        out_specs=pl.BlockSpec((1, 8, tn), lambda l, j: (l, 0, j)),
        out_shape=jax.ShapeDtypeStruct((L, 8, W), F32),
        compiler_params=_cparams("arbitrary", "arbitrary"),
        name="ada",
    )(cc, w_ada, b_ada.reshape(L, 1, W))


def _mod_spec(nmod, tiles_per_mod, D):
    if nmod == 1:
        return pl.BlockSpec((1, 1, D), lambda i: (0, 0, 0))
    return pl.BlockSpec((1, 1, D), lambda i: (i // tiles_per_mod, 0, 0))


def _normmod_kernel(x_ref, g_ref, sh_ref, sc_ref, h_ref):
    x = x_ref[...]
    ms = jnp.mean(x * x, axis=-1, keepdims=True)
    y = x * lax.rsqrt(ms + EPS) * g_ref[...]
    h_ref[...] = (y * (1.0 + sc_ref[0]) + sh_ref[0]).astype(BF16)


def _normmod(x, g, shift, scale, seq_len):
    rows, D = x.shape
    tm = min(512, seq_len)
    mspec = _mod_spec(shift.shape[0], seq_len // tm, D)
    xspec = pl.BlockSpec((tm, D), lambda i: (i, 0))
    return pl.pallas_call(
        _normmod_kernel, grid=(rows // tm,),
        in_specs=[xspec, pl.BlockSpec((1, D), lambda i: (0, 0)), mspec, mspec],
        out_specs=xspec, out_shape=jax.ShapeDtypeStruct((rows, D), BF16),
        compiler_params=_cparams("arbitrary"), name="normmod",
    )(x, g, shift, scale)


def _qkv_kernel(*refs, rope):
    if rope:
        (h_ref, w_ref, b_ref, gq_ref, gk_ref, bd_ref, ex_ref, cos_ref, sa_ref, sb_ref,
         qn_ref, kn_ref, v_ref, qr_ref, kr_ref) = refs
    else:
        h_ref, w_ref, b_ref, gq_ref, gk_ref, bd_ref, ex_ref, qn_ref, kn_ref, v_ref = refs
    h = h_ref[...]
    aw = ATTN_WIDTH

    def proj(s):
        return _dot(h, w_ref[0, :, s * aw:(s + 1) * aw]) + b_ref[:, s * aw:(s + 1) * aw]

    def headnorm(a, g_ref):
        ss = _dot((a * a).astype(BF16), bd_ref[...])
        inv = lax.rsqrt(ss * (1.0 / HEAD_DIM) + EPS)
        inv_hi = inv.astype(BF16)
        inv_lo = (inv - inv_hi.astype(F32)).astype(BF16)
        full = _dot(inv_hi, ex_ref[...]) + _dot(inv_lo, ex_ref[...])
        return a * full * g_ref[...]

    def store_plain(xn, out_ref):
        for c in range(N_HEAD_PAIRS):
            out_ref[c] = xn[:, c * LANES:(c + 1) * LANES].astype(BF16)

    def store_rope(xn, out_ref):
        cos, sa, sb = cos_ref[...], sa_ref[...], sb_ref[...]
        for c in range(N_HEAD_PAIRS):
            xc = xn[:, c * LANES:(c + 1) * LANES]
            r = xc * cos + pltpu.roll(xc, LANES - 16, 1) * sa + pltpu.roll(xc, 16, 1) * sb
            out_ref[c] = r.astype(BF16)

    qn = headnorm(proj(0), gq_ref) * Q_SCALE
    store_plain(qn, qn_ref)
    if rope:
        store_rope(qn, qr_ref)
    kn = headnorm(proj(1), gk_ref)
    store_plain(kn, kn_ref)
    if rope:
        store_rope(kn, kr_ref)
    store_plain(proj(2), v_ref)


def _qkv(h, w, l, b, gq, gk, bd, ex, rope_tabs, seq_len):
    rows, D = h.shape
    tm = min(512, seq_len)
    rope = rope_tabs is not None
    const = lambda i: (0, 0)
    in_specs = [pl.BlockSpec((tm, D), lambda i: (i, 0)),
                pl.BlockSpec((1, D, 3 * ATTN_WIDTH), lambda i: (l, 0, 0), pipeline_mode=pl.Buffered(1)),
                pl.BlockSpec((1, 3 * ATTN_WIDTH), const),
                pl.BlockSpec((1, ATTN_WIDTH), const),
                pl.BlockSpec((1, ATTN_WIDTH), const),
                pl.BlockSpec((ATTN_WIDTH, LANES), const),
                pl.BlockSpec((LANES, ATTN_WIDTH), const)]
    ins = [h, w, b, gq, gk, bd, ex]
    n_out = 3
    if rope:
        tiles_per_seq = seq_len // tm
        tspec = pl.BlockSpec((tm, LANES), lambda i: (i % tiles_per_seq, 0))
        in_specs += [tspec, tspec, tspec]
        ins += list(rope_tabs)
        n_out = 5
    ospec = pl.BlockSpec((N_HEAD_PAIRS, tm, LANES), lambda i: (0, i, 0))
    return pl.pallas_call(
        functools.partial(_qkv_kernel, rope=rope),
        grid=(rows // tm,), in_specs=in_specs,
        out_specs=[ospec] * n_out,
        out_shape=[jax.ShapeDtypeStruct((N_HEAD_PAIRS, rows, LANES), BF16)] * n_out,
        compiler_params=_cparams("arbitrary"), name="qkv",
    )(*ins)


def _proj_kernel(h_ref, w_ref, b_ref, o_ref, wb_ref):
    @pl.when(pl.program_id(1) == 0)
    def _():
        wb_ref[...] = w_ref[0].astype(BF16)

    o_ref[...] = (_dot(h_ref[...], wb_ref[...]) + b_ref[0]).astype(o_ref.dtype)


def _proj(h, w, b, l, col0, ncols):
    rows, D = h.shape
    tm = min(2048, rows)
    tn = 512
    j0 = col0 // tn
    return pl.pallas_call(
        _proj_kernel, grid=(ncols // tn, rows // tm),
        in_specs=[pl.BlockSpec((tm, D), lambda j, i: (i, 0)),
                  pl.BlockSpec((1, D, tn), lambda j, i: (l, 0, j + j0)),
                  pl.BlockSpec((1, 1, tn), lambda j, i: (l, 0, j + j0))],
        out_specs=pl.BlockSpec((tm, tn), lambda j, i: (i, j)),
        out_shape=jax.ShapeDtypeStruct((rows, ncols), BF16),
        scratch_shapes=[pltpu.VMEM((D, tn), BF16)],
        compiler_params=_cparams("arbitrary", "arbitrary"), name="proj",
    )(h, w, b)


def _attn_kernel(qn_ref, qr_ref, kr_ref, v_ref, kc_ref, vc_ref, fp_ref, o_ref, *, rows):
    gq = ROW_GROUP * GRID_W
    nk = KEY_ROWS * GRID_W
    lane = lax.broadcasted_iota(jnp.int32, (1, LANES), 1)
    head_mask = (lane < HEAD_DIM, lane >= HEAD_DIM)
    key_row = lax.broadcasted_iota(jnp.int32, (1, nk), 1) // GRID_W
    kc = kc_ref[0]
    vc = [jnp.where(head_mask[hh], vc_ref[0], 1) for hh in range(2)]

    def group(gi, carry):
        r0 = gi * ROW_GROUP
        ws = jnp.clip(r0 - WIN_ROWS // 2, 0, rows - KEY_ROWS)
        tok0 = pl.multiple_of(gi * gq, gq)
        key0 = pl.multiple_of(ws * GRID_W, GRID_W)
        qr = qr_ref[0, pl.ds(tok0, gq), :]
        qn = qn_ref[0, pl.ds(tok0, gq), :]
        kw = kr_ref[0, pl.ds(key0, nk), :]
        vw = v_ref[0, pl.ds(key0, nk), :]
        scores = [(_dot_t(jnp.where(head_mask[hh], qr, 0), kw),
                   _dot_t(jnp.where(head_mask[hh], qn, 0), kc))
                  for hh in range(2)]
        outs = []
        for hh in range(2):
            s_loc, s_ctx = scores[hh]
            slabs = []
            for i in range(ROW_GROUP):
                n0 = ws - r0 - i + (WIN_ROWS - 1) + 8
                bias = jnp.concatenate([fp_ref[hh, n0 + 2 * jp] for jp in range(KEY_ROWS // 2)], axis=1)
                lo = jnp.clip(r0 + i - WIN_ROWS // 2, 0, rows - WIN_ROWS) - ws
                valid = (key_row >= lo) & (key_row < lo + WIN_ROWS)
                slabs.append(jnp.where(valid, s_loc[i * GRID_W:(i + 1) * GRID_W] + bias, NEG_INF))
            s_loc = jnp.concatenate(slabs, axis=0)
            m = jnp.maximum(jnp.max(s_loc, axis=-1, keepdims=True), jnp.max(s_ctx, axis=-1, keepdims=True))
            p_loc = jnp.exp2(s_loc - m).astype(BF16)
            p_ctx = jnp.exp2(s_ctx - m).astype(BF16)
            o = _dot(p_loc, jnp.where(head_mask[hh], vw, 1)) + _dot(p_ctx, vc[hh])
            outs.append(o / pltpu.roll(o, HEAD_DIM, 1))
        o_ref[0, pl.ds(tok0, gq), :] = jnp.where(head_mask[0], outs[0], outs[1]).astype(BF16)
        return carry

    lax.fori_loop(0, rows // ROW_GROUP, group, 0, unroll=2)


def _attn(qn, qr, kr, v, kc, vc, fp, B, S, C):
    rows = S // GRID_W
    tok = pl.BlockSpec((1, S, LANES), lambda b, p: (p, b, 0))
    ctx = pl.BlockSpec((1, C, LANES), lambda b, p: (p, b, 0))
    return pl.pallas_call(
        functools.partial(_attn_kernel, rows=rows),
        grid=(B, N_HEAD_PAIRS),
        in_specs=[tok, tok, tok, tok, ctx, ctx,
                  pl.BlockSpec((2, N_PAIR_BLOCKS, GRID_W, LANES), lambda b, p: (p, 0, 0, 0))],
        out_specs=tok,
        out_shape=jax.ShapeDtypeStruct((N_HEAD_PAIRS, B * S, LANES), BF16),
        compiler_params=_cparams("arbitrary", "arbitrary"), name="attn",
    )(qn, qr, kr, v, kc, vc, fp)


def _ctx_attn_kernel(q_ref, k_ref, v_ref, o_ref):
    lane = lax.broadcasted_iota(jnp.int32, (1, LANES), 1)
    head_mask = (lane < HEAD_DIM, lane >= HEAD_DIM)
    q, k, v = q_ref[0], k_ref[0], v_ref[0]
    outs = []
    for hh in range(2):
        s = _dot_t(jnp.where(head_mask[hh], q, 0), k)
        m = jnp.max(s, axis=-1, keepdims=True)
        p = jnp.exp2(s - m)
        outs.append(_dot(p.astype(BF16), v) / jnp.sum(p, axis=-1, keepdims=True))
    o_ref[0] = jnp.where(head_mask[0], outs[0], outs[1]).astype(BF16)


def _ctx_attn(q, k, v, B, C):
    spec = pl.BlockSpec((1, C, LANES), lambda b, p: (p, b, 0))
    return pl.pallas_call(
        _ctx_attn_kernel, grid=(B, N_HEAD_PAIRS),
        in_specs=[spec, spec, spec], out_specs=spec,
        out_shape=jax.ShapeDtypeStruct((N_HEAD_PAIRS, B * C, LANES), BF16),
        compiler_params=_cparams("arbitrary", "arbitrary"), name="ctx_attn",
    )(q, k, v)


def _branch_kernel(attn_ref, pr_ref, pcp_ref, pcn_ref, wa_ref, wc_ref, ws_ref,
                   dww_ref, dwb_ref, lng_ref, lnb_ref, scw_ref, o_ref, glu_ext, sc_ext, *, tm, seq_len, D):
    i = pl.program_id(0)
    pos0 = (i * tm) % seq_len
    keep_prev = (pos0 != 0).astype(F32)
    keep_next = (pos0 + tm != seq_len).astype(F32)
    cw, sw = CONV_WIDTH, SC_WIDTH

    def glu(blk):
        return blk[:, 0:cw].astype(F32) * _sigmoid(blk[:, cw:2 * cw].astype(F32))

    def gated_x(blk):
        o = 2 * cw + sw
        return blk[:, o:o + sw].astype(F32) * blk[:, o + sw:o + 2 * sw].astype(F32)

    pcw = 2 * cw + 3 * sw
    cur, prv, nxt = pr_ref[:, 0:pcw], pcp_ref[...], pcn_ref[...]
    glu_ext[0:HALO, :] = glu(prv) * keep_prev
    glu_ext[HALO:HALO + tm, :] = glu(cur)
    glu_ext[HALO + tm:2 * HALO + tm, :] = glu(nxt) * keep_next
    sc_ext[0:HALO, :] = gated_x(prv) * keep_prev
    sc_ext[HALO:HALO + tm, :] = gated_x(cur)
    sc_ext[HALO + tm:2 * HALO + tm, :] = gated_x(nxt) * keep_next

    chunk = 32
    window = chunk + 2 * HALO
    conv_rows = []
    for r in range(0, tm, chunk):
        win = glu_ext[r:r + window, :]
        acc = jnp.zeros((chunk, cw), F32)
        for res in range(SUBLANES):
            shifted = win if res == 0 else pltpu.roll(win, window - res, 0)
            for k in range(CONV_K):
                off = HALO - CONV_K // 2 + k
                if off % SUBLANES == res:
                    acc = acc + dww_ref[k:k + 1, :] * shifted[off - res:off - res + chunk, :]
        conv_rows.append(acc)
    u = jnp.concatenate(conv_rows, axis=0) + dwb_ref[...]
    mu = jnp.mean(u, axis=-1, keepdims=True)
    uc = u - mu
    var = jnp.mean(uc * uc, axis=-1, keepdims=True)
    u = _silu(uc * lax.rsqrt(var + EPS) * lng_ref[...] + lnb_ref[...])
    y_b = _dot(u.astype(BF16), wc_ref[0])

    c3 = jnp.zeros((tm, sw), F32)
    for k in range(SC_K):
        base = HALO - SC_K // 2 + k
        c3 = c3 + scw_ref[k:k + 1, :] * sc_ext[base:base + tm, :]
    sc_b = cur[:, 2 * cw:2 * cw + sw].astype(F32)
    y_c = _dot((sc_b * c3).astype(BF16), ws_ref[0])

    attn = jnp.concatenate([attn_ref[c] for c in range(N_HEAD_PAIRS)], axis=1)
    y_a = _dot(attn, wa_ref[0])

    g = pr_ref[:, pcw:pcw + 3 * D]
    merged = (_sigmoid(g[:, 0:D].astype(F32)) * y_a
              + _sigmoid(g[:, D:2 * D].astype(F32)) * y_b
              + _sigmoid(g[:, 2 * D:3 * D].astype(F32)) * y_c)
    o_ref[...] = merged.astype(BF16)


def _branch(attn, pr, wa, wc, ws, l, dww, dwb, lng, lnb, scw, seq_len, D):
    rows, prw = pr.shape
    tm = 256
    pcw = 2 * CONV_WIDTH + 3 * SC_WIDTH
    nhalo = rows // HALO
    const = lambda i: (0, 0)
    layer = lambda i: (l, 0, 0)
    return pl.pallas_call(
        functools.partial(_branch_kernel, tm=tm, seq_len=seq_len, D=D),
        grid=(rows // tm,),
        in_specs=[pl.BlockSpec((N_HEAD_PAIRS, tm, LANES), lambda i: (0, i, 0)),
                  pl.BlockSpec((tm, prw), lambda i: (i, 0)),
                  pl.BlockSpec((HALO, pcw), lambda i: (jnp.maximum(i * (tm // HALO) - 1, 0), 0)),
                  pl.BlockSpec((HALO, pcw), lambda i: (jnp.minimum((i + 1) * (tm // HALO), nhalo - 1), 0)),
                  pl.BlockSpec((1, ATTN_WIDTH, D), layer),
                  pl.BlockSpec((1, CONV_WIDTH, D), layer),
                  pl.BlockSpec((1, SC_WIDTH, D), layer),
                  pl.BlockSpec((32, CONV_WIDTH), const),
                  pl.BlockSpec((1, CONV_WIDTH), const),
                  pl.BlockSpec((1, CONV_WIDTH), const),
                  pl.BlockSpec((1, CONV_WIDTH), const),
                  pl.BlockSpec((8, SC_WIDTH), const)],
        out_specs=pl.BlockSpec((tm, D), lambda i: (i, 0)),
        out_shape=jax.ShapeDtypeStruct((rows, D), BF16),
        scratch_shapes=[pltpu.VMEM((tm + 2 * HALO, CONV_WIDTH), F32),
                        pltpu.VMEM((tm + 2 * HALO, SC_WIDTH), F32)],
        compiler_params=_cparams("arbitrary"), name="branch",
    )(attn, pr, pr, pr, wa, wc, ws, dww, dwb, lng, lnb, scw)


def _outproj_kernel(m_ref, x_ref, wo_ref, gate_ref, g2_ref, sh_ref, sc_ref, wr_ref, xmid_ref, h2_ref, aff_ref):
    x = x_ref[...] + gate_ref[0] * _dot(m_ref[...], wo_ref[0])
    xmid_ref[...] = x
    ms = jnp.mean(x * x, axis=-1, keepdims=True)
    y = x * lax.rsqrt(ms + EPS) * g2_ref[...]
    h2 = y * (1.0 + sc_ref[0]) + sh_ref[0]
    h2_ref[...] = h2
    logits = _dot(h2.astype(BF16), wr_ref[...])
    lane = lax.broadcasted_iota(jnp.int32, logits.shape, 1)
    logits = jnp.where(lane < N_EXPERTS, logits, NEG_INF)
    e = jnp.exp(logits - jnp.max(logits, axis=-1, keepdims=True))
    aff_ref[...] = e / jnp.sum(e, axis=-1, keepdims=True)


def _outproj(merged, x, wo, l, gate, g2, shift, scale, wr, seq_len):
    rows, D = x.shape
    tm = 256
    mspec = _mod_spec(gate.shape[0], seq_len // tm, D)
    row = lambda i: (i, 0)
    const = lambda i: (0, 0)
    return pl.pallas_call(
        _outproj_kernel, grid=(rows // tm,),
        in_specs=[pl.BlockSpec((tm, D), row), pl.BlockSpec((tm, D), row),
                  pl.BlockSpec((1, D, D), lambda i: (l, 0, 0)), mspec,
                  pl.BlockSpec((1, D), const), mspec, mspec,
                  pl.BlockSpec((D, LANES), const)],
        out_specs=[pl.BlockSpec((tm, D), row), pl.BlockSpec((tm, D), row), pl.BlockSpec((tm, LANES), row)],
        out_shape=[jax.ShapeDtypeStruct((rows, D), F32), jax.ShapeDtypeStruct((rows, D), F32),
                   jax.ShapeDtypeStruct((rows, LANES), F32)],
        compiler_params=_cparams("arbitrary"), name="outproj",
    )(merged, x, wo, gate, g2, shift, scale, wr)


MAX_BISECT = 160
POS_LANE = 3 * N_EXPERTS


def _select_kernel(aff_ref, tri_ref, tv_ref, idx_ref, g_ref, s1_ref, r_ref, *, n, cap, blk, tchunk):
    E = N_EXPERTS
    aff = aff_ref[0]
    a = aff.T[0:E, :]
    capf = float(cap)

    def cond(st):
        it, _, _, ndone = st
        return jnp.logical_and(it < MAX_BISECT, ndone < float(E))

    def in_range(lo, hi):
        return jnp.logical_and(a >= lo, a < hi)

    def body(st):
        it, lo, hi, _ = st
        mid = 0.5 * (lo + hi)
        cnt = jnp.sum((a >= mid).astype(F32), axis=-1, keepdims=True)
        ge = cnt >= capf
        lo = jnp.where(ge, mid, lo)
        hi = jnp.where(ge, hi, mid)
        r = in_range(lo, hi)
        vmin = jnp.min(jnp.where(r, a, 4.0), axis=-1, keepdims=True)
        vmax = jnp.max(jnp.where(r, a, -1.0), axis=-1, keepdims=True)
        ndone = jnp.sum((vmin == vmax).astype(F32))
        return it + 1, lo, hi, ndone

    _, lo, hi, _ = lax.while_loop(cond, body, (jnp.int32(0), jnp.zeros((E, 1), F32), jnp.full((E, 1), 2.0, F32),
                                               jnp.float32(0.0)))
    thr = jnp.max(jnp.where(in_range(lo, hi), a, -1.0), axis=-1, keepdims=True)

    def cumsum(x):
        outs, carry = [], jnp.zeros((E, 1), F32)
        for c in range(n // blk):
            part = _dot(x[:, c * blk:(c + 1) * blk].astype(BF16), tri_ref[...]) + carry
            outs.append(part)
            carry = part[:, blk - 1:blk]
        return jnp.concatenate(outs, axis=1)

    gt = a > thr
    eq = (a == thr).astype(F32)
    need = capf - jnp.sum(gt.astype(F32), axis=-1, keepdims=True)
    eq_rank = cumsum(eq) - eq
    sel = jnp.logical_or(gt, jnp.logical_and(eq > 0.0, eq_rank < need)).astype(F32)
    s1_ref[...] = sel * cumsum(sel)

    a_hi = aff.astype(BF16).astype(F32)
    rest = aff - a_hi
    a_mid = rest.astype(BF16).astype(F32)
    a_lo = rest - a_mid
    r_ref[...] = (a_hi + pltpu.roll(a_mid, E, 1) + pltpu.roll(a_lo, 2 * E, 1) + tv_ref[...]).astype(BF16)

    slot_ids = lax.broadcasted_iota(jnp.int32, (cap, 1), 0).astype(F32) + 1.0
    lane = lax.broadcasted_iota(jnp.int32, (1, LANES), 1)

    def per_expert(e, carry):
        row = s1_ref[pl.ds(e, 1), :]
        acc = jnp.zeros((cap, LANES), F32)
        for c in range(n // tchunk):
            onehot = (row[:, c * tchunk:(c + 1) * tchunk] == slot_ids).astype(BF16)
            acc = acc + _dot(onehot, r_ref[c * tchunk:(c + 1) * tchunk, :])
        mine = jnp.logical_and(lane % E == e, lane < POS_LANE)
        g = jnp.sum(jnp.where(mine, acc, 0.0), axis=1, keepdims=True)
        tok = jnp.sum(jnp.where(lane == POS_LANE, acc * 64.0, jnp.where(lane == POS_LANE + 1, acc, 0.0)),
                      axis=1, keepdims=True)
        g_ref[0, e] = jnp.broadcast_to(g, (cap, LANES))
        idx_ref[0, e] = jnp.broadcast_to(tok, (cap, LANES)).astype(jnp.int32)
        return carry

    lax.fori_loop(0, E, per_expert, 0)


def _select(aff, B, n, cap):
    blk = min(512, n)
    tchunk = min(1024, n)
    tri = (np.arange(blk)[:, None] <= np.arange(blk)[None, :]).astype(np.float32)
    tv = np.zeros((n, LANES), np.float32)
    tv[:, POS_LANE] = np.arange(n) // 64
    tv[:, POS_LANE + 1] = np.arange(n) % 64
    out = pl.BlockSpec((1, N_EXPERTS, cap, LANES), lambda b: (b, 0, 0, 0))
    return pl.pallas_call(
        functools.partial(_select_kernel, n=n, cap=cap, blk=blk, tchunk=tchunk),
        grid=(B,),
        in_specs=[pl.BlockSpec((1, n, LANES), lambda b: (b, 0, 0)),
                  pl.BlockSpec((blk, blk), lambda b: (0, 0)),
                  pl.BlockSpec((n, LANES), lambda b: (0, 0))],
        out_specs=[out, out],
        out_shape=[jax.ShapeDtypeStruct((B, N_EXPERTS, cap, LANES), jnp.int32),
                   jax.ShapeDtypeStruct((B, N_EXPERTS, cap, LANES), F32)],
        scratch_shapes=[pltpu.VMEM((N_EXPERTS, n), F32), pltpu.VMEM((n, LANES), BF16)],
        compiler_params=_cparams("arbitrary"), name="select",
    )(aff.reshape(B, n, LANES), jnp.asarray(tri, BF16), jnp.asarray(tv))


MIN_CHUNK_ROWS = 256


def _expert_kernel(idx_ref, g_ref, gate_ref, w1_ref, w3_ref, w2_ref, h_hbm, acc_in, acc_hbm,
                   xs_buf, acc_buf, sem, *, bb, cap, n, gate_rows, n_chunks):
    del acc_in
    e = pl.program_id(0)
    b0 = pl.program_id(1) * bb
    rows = bb * cap
    D = xs_buf.shape[2]

    def token_row(j):
        b = b0 + j // cap
        return b * n + idx_ref[(b * N_EXPERTS + e) * cap + j % cap]

    def loop_rows(fn, lo, hi):
        for bl in range(bb):
            seg_lo, seg_hi = max(lo, bl * cap), min(hi, (bl + 1) * cap)
            if seg_lo >= seg_hi:
                continue
            b = b0 + bl
            base = (b * N_EXPERTS + e) * cap - bl * cap

            def body(t, carry, b=b, base=base):
                for u in range(SUBLANES):
                    fn(t, u, b * n + idx_ref[base + t * SUBLANES + u])
                return carry

            lax.fori_loop(seg_lo // SUBLANES, seg_hi // SUBLANES, body, 0)

    def inline_rows(fn, lo, hi):
        for j in range(lo, hi):
            fn(j // SUBLANES, j % SUBLANES, token_row(j))

    def x_copy(t, u, r):
        return pltpu.make_async_copy(h_hbm.at[pl.ds(r, 1), :], xs_buf.at[t, pl.ds(u, 1), :], sem.at[0])

    def acc_load(t, u, r):
        return pltpu.make_async_copy(acc_hbm.at[pl.ds(r, 1), :], acc_buf.at[t, pl.ds(u, 1), :], sem.at[1])

    def acc_store(t, u, r):
        return pltpu.make_async_copy(acc_buf.at[t, pl.ds(u, 1), :], acc_hbm.at[pl.ds(r, 1), :], sem.at[2])

    def start_loads(t, u, r):
        x_copy(t, u, r).start()
        acc_load(t, u, r).start()

    def wait_loads(t, u, r):
        x_copy(t, u, r).wait()
        acc_load(t, u, r).wait()

    def start_store(t, u, r):
        acc_store(t, u, r).start()

    def wait_store(t, u, r):
        acc_store(t, u, r).wait()

    g = g_ref[...].reshape(rows, LANES)

    def compute(lo, hi):
        ts = slice(lo // SUBLANES, hi // SUBLANES)
        x = xs_buf[ts].reshape(hi - lo, D).astype(BF16)
        hidden = (_silu(_dot(x, w1_ref[0, 0])) * _dot(x, w3_ref[0, 0])).astype(BF16)
        y = _dot(hidden, w2_ref[0, 0]) * jnp.concatenate([g[lo:hi]] * (D // LANES), axis=1)
        for bl in range(bb):
            seg_lo, seg_hi = max(lo, bl * cap), min(hi, (bl + 1) * cap)
            if seg_lo >= seg_hi:
                continue
            gate_row = gate_ref[bl if gate_rows > 1 else 0]
            seg = slice(seg_lo // SUBLANES, seg_hi // SUBLANES)
            upd = y[seg_lo - lo:seg_hi - lo] * gate_row
            acc_buf[seg] += upd.reshape((seg_hi - seg_lo) // SUBLANES, SUBLANES, D)

    bounds = [(k * rows // n_chunks, (k + 1) * rows // n_chunks) for k in range(n_chunks)]
    loop_rows(start_loads, *bounds[0])
    loop_rows(wait_loads, *bounds[0])
    for k in range(n_chunks):
        if k + 1 < n_chunks:
            inline_rows(start_loads, *bounds[k + 1])
        if k >= 1:
            inline_rows(start_store, *bounds[k - 1])
        compute(*bounds[k])
        if k + 1 < n_chunks:
            loop_rows(wait_loads, *bounds[k + 1])
    loop_rows(start_store, *bounds[-1])
    loop_rows(wait_store, 0, rows)


def _experts(idx, g, gate, w1, w3, w2, l, h2, x_mid, B, n, cap, bb):
    _, E, D, FF = w1.shape
    rows = bb * cap
    gate_rows = bb if gate.shape[0] > 1 else 1
    gate_spec = (pl.BlockSpec((bb, 1, D), lambda e, b, idx: (b, 0, 0)) if gate.shape[0] > 1
                 else pl.BlockSpec((1, 1, D), lambda e, b, idx: (0, 0, 0)))
    grid_spec = pltpu.PrefetchScalarGridSpec(
        num_scalar_prefetch=1,
        grid=(E, B // bb),
        in_specs=[pl.BlockSpec((bb, 1, cap, LANES), lambda e, b, idx: (b, e, 0, 0)),
                  gate_spec,
                  pl.BlockSpec((1, 1, D, FF), lambda e, b, idx: (l, e, 0, 0)),
                  pl.BlockSpec((1, 1, D, FF), lambda e, b, idx: (l, e, 0, 0)),
                  pl.BlockSpec((1, 1, FF, D), lambda e, b, idx: (l, e, 0, 0)),
                  pl.BlockSpec(memory_space=pl.ANY),
                  pl.BlockSpec(memory_space=pl.ANY)],
        out_specs=pl.BlockSpec(memory_space=pl.ANY),
        scratch_shapes=[pltpu.VMEM((rows // SUBLANES, SUBLANES, D), F32),
                        pltpu.VMEM((rows // SUBLANES, SUBLANES, D), F32),
                        pltpu.SemaphoreType.DMA((3,))])
    return pl.pallas_call(
        functools.partial(_expert_kernel, bb=bb, cap=cap, n=n, gate_rows=gate_rows,
                          n_chunks=2 if rows >= 2 * MIN_CHUNK_ROWS else 1),
        grid_spec=grid_spec,
        out_shape=jax.ShapeDtypeStruct(x_mid.shape, F32),
        input_output_aliases={7: 0},
        compiler_params=_cparams("arbitrary", "arbitrary"), name="experts",
    )(idx, g, gate, w1, w3, w2, h2, x_mid)


def _moe(h2, aff, x_mid, gate, w1, w3, w2, l, B, n, bb):
    cap = max(1, CAP_FACTOR * n // N_EXPERTS)
    idx, g = _select(aff, B, n, cap)
    return _experts(idx[..., 0].reshape(-1), g, gate, w1, w3, w2, l, h2, x_mid, B, n, cap, bb)


def _rope_tables(S):
    half, quarter = HEAD_DIM // 2, HEAD_DIM // 4
    t = jnp.arange(S)
    freqs = 1.0 / (ROPE_BASE ** (jnp.arange(quarter, dtype=F32) / quarter))
    d = np.arange(LANES) % HEAD_DIM
    pos = jnp.where(jnp.asarray(d < half)[None, :], (t // GRID_W)[:, None], (t % GRID_W)[:, None]).astype(F32)
    ang = pos * freqs[jnp.asarray(d % quarter)][None, :]
    first = jnp.asarray((d % half) < quarter)[None, :]
    sin = jnp.sin(ang)
    return jnp.cos(ang), jnp.where(first, -sin, 0.0), jnp.where(first, 0.0, sin)


def _bias_pair_blocks(rpb):
    cq = np.arange(GRID_W)
    ck = np.arange(GRID_W)
    cstart = np.clip(cq - WIN_COLS // 2, 0, GRID_W - WIN_COLS)
    valid = (ck[None, :] >= cstart[:, None]) & (ck[None, :] < cstart[:, None] + WIN_COLS)
    col_off = np.clip(ck[None, :] - cq[:, None] + WIN_COLS - 1, 0, 2 * WIN_COLS - 2)
    place = (col_off[None] == np.arange(2 * WIN_COLS - 1)[:, None, None]).astype(np.float32)
    ro = np.clip(np.arange(N_PAIR_BLOCKS + 1) - 8, 0, 2 * WIN_ROWS - 2)
    T = jnp.einsum('hrd,dck->hrck', rpb[:, ro], jnp.asarray(place), precision=lax.Precision.HIGHEST)
    T = jnp.where(jnp.asarray(valid)[None, None], T * LOG2E, NEG_INF)
    return jnp.concatenate([T[:, :-1], T[:, 1:]], axis=-1)


def _head_sum_tables():
    lane = np.arange(ATTN_WIDTH)
    bd = (lane[:, None] // HEAD_DIM == np.arange(LANES)[None, :]).astype(np.float32)
    return jnp.asarray(bd, BF16), jnp.asarray(bd.T, BF16)


def kernel(x, c, ctx, c_ctx, w_ada, b_ada, g_norm1, g_norm2, w_in, b_in, g_q, g_k, rpb, w_attn_o, conv_dw_w,
           conv_dw_b, conv_ln_g, conv_ln_b, w_conv_o, sc_w, w_sc_o, w_o, w_router, w_e1, w_e3, w_e2):
    B, S, D = x.shape
    C = ctx.shape[1]
    L = w_ada.shape[0]
    aw3 = 3 * ATTN_WIDTH
    prw = 2 * CONV_WIDTH + 3 * SC_WIDTH + 3 * D

    cc = jnp.zeros((8, D), F32).at[0:B].set(c).at[B].set(c_ctx)
    mod = _ada(cc, w_ada, b_ada)
    rope_tabs = _rope_tables(S)
    bd, ex = _head_sum_tables()

    xl = x.reshape(B * S, D)
    xc = ctx.reshape(B * C, D)
    b_in3 = b_in.reshape(L, 1, -1)
    w_qkv = w_in[:, :, :aw3].astype(BF16)
    wa, wc, ws, wo = (w.astype(BF16) for w in (w_attn_o, w_conv_o, w_sc_o, w_o))
    we = (w_e1.astype(BF16), w_e3.astype(BF16), w_e2.astype(BF16))

    for l in range(L):
        last = l == L - 1
        ml = lambda j: mod[l, 0:B, j * D:(j + 1) * D].reshape(B, 1, D)
        mc = lambda j: mod[l, B:B + 1, j * D:(j + 1) * D].reshape(1, 1, D)
        g1 = g_norm1[l].reshape(1, D)
        g2 = g_norm2[l].reshape(1, D)
        b_qkv = b_in[l][:aw3].reshape(1, aw3)
        gq = jnp.tile(g_q[l], N_HEADS).reshape(1, ATTN_WIDTH)
        gk = jnp.tile(g_k[l], N_HEADS).reshape(1, ATTN_WIDTH)
        branch_w = (wa, wc, ws, l,
                    jnp.zeros((32, CONV_WIDTH), F32).at[:CONV_K].set(conv_dw_w[l]),
                    conv_dw_b[l].reshape(1, -1), conv_ln_g[l].reshape(1, -1), conv_ln_b[l].reshape(1, -1),
                    jnp.zeros((8, SC_WIDTH), F32).at[:SC_K].set(sc_w[l]))
        wr = jnp.zeros((D, LANES), F32).at[:, :N_EXPERTS].set(w_router[l]).astype(BF16)

        hc = _normmod(xc, g1, mc(0), mc(1), C)
        qc, kc, vc = _qkv(hc, w_qkv, l, b_qkv, gq, gk, bd, ex, None, C)
        if not last:
            attn_c = _ctx_attn(qc, kc, vc, B, C)
            pr_c = _proj(hc, w_in, b_in3, l, aw3, prw)
            merged_c = _branch(attn_c, pr_c, *branch_w, C, D)
            xc_mid, hc2, aff_c = _outproj(merged_c, xc, wo, l, mc(2), g2, mc(3), mc(4), wr, C)
            xc = _moe(hc2, aff_c, xc_mid, mc(5), *we, l, B, C, B)

        h = _normmod(xl, g1, ml(0), ml(1), S)
        qn, kn, v, qr, kr = _qkv(h, w_qkv, l, b_qkv, gq, gk, bd, ex, rope_tabs, S)
        attn = _attn(qn, qr, kr, v, kc, vc, _bias_pair_blocks(rpb[l]), B, S, C)
        pr = _proj(h, w_in, b_in3, l, aw3, prw)
        merged = _branch(attn, pr, *branch_w, S, D)
        x_mid, h2, aff = _outproj(merged, xl, wo, l, ml(2), g2, ml(3), ml(4), wr, S)
        xl = _moe(h2, aff, x_mid, ml(5), *we, l, B, S, 1)

    return xl.reshape(B, S, D)
```

```python
import functools

import numpy as np
import jax
import jax.numpy as jnp
from jax import lax
from jax.experimental import pallas as pl
from jax.experimental.pallas import tpu as pltpu

F32 = jnp.float32
BF16 = jnp.bfloat16

GRID_W = 64
N_HEADS = 16
HEAD_DIM = 64
ATTN_WIDTH = N_HEADS * HEAD_DIM
N_HEAD_PAIRS = N_HEADS // 2
WIN_ROWS = 8
WIN_COLS = 16
ROPE_BASE = 10000.0
CONV_WIDTH = 512
CONV_K = 31
SC_WIDTH = 512
SC_K = 3
N_EXPERTS = 16
CAP_FACTOR = 2
EPS = 1e-6
NEG_INF = -1e30

LANES = 128
SUBLANES = 8
LOG2E = 1.4426950408889634
Q_SCALE = HEAD_DIM ** -0.5 * LOG2E
HALO = 16
ROW_GROUP = 4
KEY_ROWS = 12
N_PAIR_BLOCKS = 26
VMEM_LIMIT = 56 * 1024 * 1024


def _cparams(*sem):
    return pltpu.CompilerParams(dimension_semantics=sem, vmem_limit_bytes=VMEM_LIMIT)


def _dot(a, b):
    return jnp.dot(a, b, preferred_element_type=F32)


def _dot_t(a, b):
    return lax.dot_general(a, b, (((1,), (1,)), ((), ())), preferred_element_type=F32)


def _sigmoid(x):
    return 0.5 * jnp.tanh(0.5 * x) + 0.5


def _silu(x):
    return x * _sigmoid(x)


def _ada_kernel(cc_ref, w_ref, b_ref, o_ref):
    s = _silu(cc_ref[...]).astype(BF16)
    o_ref[0] = _dot(s, w_ref[0].astype(BF16)) + b_ref[0]


def _ada(cc, w_ada, b_ada):
    L, D, W = w_ada.shape
    tn = 1024
    return pl.pallas_call(
        _ada_kernel,
        grid=(L, W // tn),
        in_specs=[pl.BlockSpec((8, D), lambda l, j: (0, 0)),
                  pl.BlockSpec((1, D, tn), lambda l, j: (l, 0, j)),
                  pl.BlockSpec((1, 1, tn), lambda l, j: (l, 0, j))],
        out_specs=pl.BlockSpec((1, 8, tn), lambda l, j: (l, 0, j)),
        out_shape=jax.ShapeDtypeStruct((L, 8, W), F32),
        compiler_params=_cparams("arbitrary", "arbitrary"),
        name="ada",
    )(cc, w_ada, b_ada.reshape(L, 1, W))


def _mod_spec(nmod, tiles_per_mod, D):
    if nmod == 1:
        return pl.BlockSpec((1, 1, D), lambda i: (0, 0, 0))
    return pl.BlockSpec((1, 1, D), lambda i: (i // tiles_per_mod, 0, 0))


def _normmod_kernel(x_ref, g_ref, sh_ref, sc_ref, h_ref):
    x = x_ref[...]
    ms = jnp.mean(x * x, axis=-1, keepdims=True)
    y = x * lax.rsqrt(ms + EPS) * g_ref[...]
    h_ref[...] = (y * (1.0 + sc_ref[0]) + sh_ref[0]).astype(BF16)


def _normmod(x, g, shift, scale, seq_len):
    rows, D = x.shape
    tm = min(512, seq_len)
    mspec = _mod_spec(shift.shape[0], seq_len // tm, D)
    xspec = pl.BlockSpec((tm, D), lambda i: (i, 0))
    return pl.pallas_call(
        _normmod_kernel, grid=(rows // tm,),
        in_specs=[xspec, pl.BlockSpec((1, D), lambda i: (0, 0)), mspec, mspec],
        out_specs=xspec, out_shape=jax.ShapeDtypeStruct((rows, D), BF16),
        compiler_params=_cparams("arbitrary"), name="normmod",
    )(x, g, shift, scale)


def _qkv_kernel(*refs, rope):
    if rope:
        (h_ref, w_ref, b_ref, gq_ref, gk_ref, bd_ref, ex_ref, cos_ref, sa_ref, sb_ref,
         qn_ref, kn_ref, v_ref, qr_ref, kr_ref) = refs
    else:
        h_ref, w_ref, b_ref, gq_ref, gk_ref, bd_ref, ex_ref, qn_ref, kn_ref, v_ref = refs
    h = h_ref[...]
    aw = ATTN_WIDTH

    def proj(s):
        return _dot(h, w_ref[0, :, s * aw:(s + 1) * aw]) + b_ref[:, s * aw:(s + 1) * aw]

    def headnorm(a, g_ref):
        ss = _dot((a * a).astype(BF16), bd_ref[...])
        inv = lax.rsqrt(ss * (1.0 / HEAD_DIM) + EPS)
        inv_hi = inv.astype(BF16)
        inv_lo = (inv - inv_hi.astype(F32)).astype(BF16)
        full = _dot(inv_hi, ex_ref[...]) + _dot(inv_lo, ex_ref[...])
        return a * full * g_ref[...]

    def store_plain(xn, out_ref):
        for c in range(N_HEAD_PAIRS):
            out_ref[c] = xn[:, c * LANES:(c + 1) * LANES].astype(BF16)

    def store_rope(xn, out_ref):
        cos, sa, sb = cos_ref[...], sa_ref[...], sb_ref[...]
        for c in range(N_HEAD_PAIRS):
            xc = xn[:, c * LANES:(c + 1) * LANES]
            r = xc * cos + pltpu.roll(xc, LANES - 16, 1) * sa + pltpu.roll(xc, 16, 1) * sb
            out_ref[c] = r.astype(BF16)

    qn = headnorm(proj(0), gq_ref) * Q_SCALE
    store_plain(qn, qn_ref)
    if rope:
        store_rope(qn, qr_ref)
    kn = headnorm(proj(1), gk_ref)
    store_plain(kn, kn_ref)
    if rope:
        store_rope(kn, kr_ref)
    store_plain(proj(2), v_ref)


def _qkv(h, w, l, b, gq, gk, bd, ex, rope_tabs, seq_len):
    rows, D = h.shape
    tm = min(512, seq_len)
    rope = rope_tabs is not None
    const = lambda i: (0, 0)
    in_specs = [pl.BlockSpec((tm, D), lambda i: (i, 0)),
                pl.BlockSpec((1, D, 3 * ATTN_WIDTH), lambda i: (l, 0, 0), pipeline_mode=pl.Buffered(1)),
                pl.BlockSpec((1, 3 * ATTN_WIDTH), const),
                pl.BlockSpec((1, ATTN_WIDTH), const),
                pl.BlockSpec((1, ATTN_WIDTH), const),
                pl.BlockSpec((ATTN_WIDTH, LANES), const),
                pl.BlockSpec((LANES, ATTN_WIDTH), const)]
    ins = [h, w, b, gq, gk, bd, ex]
    n_out = 3
    if rope:
        tiles_per_seq = seq_len // tm
        tspec = pl.BlockSpec((tm, LANES), lambda i: (i % tiles_per_seq, 0))
        in_specs += [tspec, tspec, tspec]
        ins += list(rope_tabs)
        n_out = 5
    ospec = pl.BlockSpec((N_HEAD_PAIRS, tm, LANES), lambda i: (0, i, 0))
    return pl.pallas_call(
        functools.partial(_qkv_kernel, rope=rope),
        grid=(rows // tm,), in_specs=in_specs,
        out_specs=[ospec] * n_out,
        out_shape=[jax.ShapeDtypeStruct((N_HEAD_PAIRS, rows, LANES), BF16)] * n_out,
        compiler_params=_cparams("arbitrary"), name="qkv",
    )(*ins)


def _proj_kernel(h_ref, w_ref, b_ref, o_ref, wb_ref):
    @pl.when(pl.program_id(1) == 0)
    def _():
        wb_ref[...] = w_ref[0].astype(BF16)

    o_ref[...] = (_dot(h_ref[...], wb_ref[...]) + b_ref[0]).astype(o_ref.dtype)


def _proj(h, w, b, l, col0, ncols):
    rows, D = h.shape
    tm = min(2048, rows)
    tn = 512
    j0 = col0 // tn
    return pl.pallas_call(
        _proj_kernel, grid=(ncols // tn, rows // tm),
        in_specs=[pl.BlockSpec((tm, D), lambda j, i: (i, 0)),
                  pl.BlockSpec((1, D, tn), lambda j, i: (l, 0, j + j0)),
                  pl.BlockSpec((1, 1, tn), lambda j, i: (l, 0, j + j0))],
        out_specs=pl.BlockSpec((tm, tn), lambda j, i: (i, j)),
        out_shape=jax.ShapeDtypeStruct((rows, ncols), BF16),
        scratch_shapes=[pltpu.VMEM((D, tn), BF16)],
        compiler_params=_cparams("arbitrary", "arbitrary"), name="proj",
    )(h, w, b)


def _attn_kernel(qn_ref, qr_ref, kr_ref, v_ref, kc_ref, vc_ref, fp_ref, o_ref, *, rows):
    gq = ROW_GROUP * GRID_W
    nk = KEY_ROWS * GRID_W
    lane = lax.broadcasted_iota(jnp.int32, (1, LANES), 1)
    head_mask = (lane < HEAD_DIM, lane >= HEAD_DIM)
    key_row = lax.broadcasted_iota(jnp.int32, (1, nk), 1) // GRID_W
    kc = kc_ref[0]
    vc = [jnp.where(head_mask[hh], vc_ref[0], 1) for hh in range(2)]

    def group(gi, carry):
        r0 = gi * ROW_GROUP
        ws = jnp.clip(r0 - WIN_ROWS // 2, 0, rows - KEY_ROWS)
        tok0 = pl.multiple_of(gi * gq, gq)
        key0 = pl.multiple_of(ws * GRID_W, GRID_W)
        qr = qr_ref[0, pl.ds(tok0, gq), :]
        qn = qn_ref[0, pl.ds(tok0, gq), :]
        kw = kr_ref[0, pl.ds(key0, nk), :]
        vw = v_ref[0, pl.ds(key0, nk), :]
        scores = [(_dot_t(jnp.where(head_mask[hh], qr, 0), kw),
                   _dot_t(jnp.where(head_mask[hh], qn, 0), kc))
                  for hh in range(2)]
        outs = []
        for hh in range(2):
            s_loc, s_ctx = scores[hh]
            slabs = []
            for i in range(ROW_GROUP):
                n0 = ws - r0 - i + (WIN_ROWS - 1) + 8
                bias = jnp.concatenate([fp_ref[hh, n0 + 2 * jp] for jp in range(KEY_ROWS // 2)], axis=1)
                lo = jnp.clip(r0 + i - WIN_ROWS // 2, 0, rows - WIN_ROWS) - ws
                valid = (key_row >= lo) & (key_row < lo + WIN_ROWS)
                slabs.append(jnp.where(valid, s_loc[i * GRID_W:(i + 1) * GRID_W] + bias, NEG_INF))
            s_loc = jnp.concatenate(slabs, axis=0)
            m = jnp.maximum(jnp.max(s_loc, axis=-1, keepdims=True), jnp.max(s_ctx, axis=-1, keepdims=True))
            p_loc = jnp.exp2(s_loc - m).astype(BF16)
            p_ctx = jnp.exp2(s_ctx - m).astype(BF16)
            o = _dot(p_loc, jnp.where(head_mask[hh], vw, 1)) + _dot(p_ctx, vc[hh])
            outs.append(o / pltpu.roll(o, HEAD_DIM, 1))
        o_ref[0, pl.ds(tok0, gq), :] = jnp.where(head_mask[0], outs[0], outs[1]).astype(BF16)
        return carry

    lax.fori_loop(0, rows // ROW_GROUP, group, 0, unroll=2)


def _attn(qn, qr, kr, v, kc, vc, fp, B, S, C):
    rows = S // GRID_W
    tok = pl.BlockSpec((1, S, LANES), lambda b, p: (p, b, 0))
    ctx = pl.BlockSpec((1, C, LANES), lambda b, p: (p, b, 0))
    return pl.pallas_call(
        functools.partial(_attn_kernel, rows=rows),
        grid=(B, N_HEAD_PAIRS),
        in_specs=[tok, tok, tok, tok, ctx, ctx,
                  pl.BlockSpec((2, N_PAIR_BLOCKS, GRID_W, LANES), lambda b, p: (p, 0, 0, 0))],
        out_specs=tok,
        out_shape=jax.ShapeDtypeStruct((N_HEAD_PAIRS, B * S, LANES), BF16),
        compiler_params=_cparams("arbitrary", "arbitrary"), name="attn",
    )(qn, qr, kr, v, kc, vc, fp)


def _ctx_attn_kernel(q_ref, k_ref, v_ref, o_ref):
    lane = lax.broadcasted_iota(jnp.int32, (1, LANES), 1)
    head_mask = (lane < HEAD_DIM, lane >= HEAD_DIM)
    q, k, v = q_ref[0], k_ref[0], v_ref[0]
    outs = []
    for hh in range(2):
        s = _dot_t(jnp.where(head_mask[hh], q, 0), k)
        m = jnp.max(s, axis=-1, keepdims=True)
        p = jnp.exp2(s - m)
        outs.append(_dot(p.astype(BF16), v) / jnp.sum(p, axis=-1, keepdims=True))
    o_ref[0] = jnp.where(head_mask[0], outs[0], outs[1]).astype(BF16)


def _ctx_attn(q, k, v, B, C):
    spec = pl.BlockSpec((1, C, LANES), lambda b, p: (p, b, 0))
    return pl.pallas_call(
        _ctx_attn_kernel, grid=(B, N_HEAD_PAIRS),
        in_specs=[spec, spec, spec], out_specs=spec,
        out_shape=jax.ShapeDtypeStruct((N_HEAD_PAIRS, B * C, LANES), BF16),
        compiler_params=_cparams("arbitrary", "arbitrary"), name="ctx_attn",
    )(q, k, v)


def _branch_kernel(attn_ref, pr_ref, pcp_ref, pcn_ref, wa_ref, wc_ref, ws_ref,
                   dww_ref, dwb_ref, lng_ref, lnb_ref, scw_ref, o_ref, glu_ext, sc_ext, *, tm, seq_len, D):
    i = pl.program_id(0)
    pos0 = (i * tm) % seq_len
    keep_prev = (pos0 != 0).astype(F32)
    keep_next = (pos0 + tm != seq_len).astype(F32)
    cw, sw = CONV_WIDTH, SC_WIDTH

    def glu(blk):
        return blk[:, 0:cw].astype(F32) * _sigmoid(blk[:, cw:2 * cw].astype(F32))

    def gated_x(blk):
        o = 2 * cw + sw
        return blk[:, o:o + sw].astype(F32) * blk[:, o + sw:o + 2 * sw].astype(F32)

    pcw = 2 * cw + 3 * sw
    cur, prv, nxt = pr_ref[:, 0:pcw], pcp_ref[...], pcn_ref[...]
    glu_ext[0:HALO, :] = glu(prv) * keep_prev
    glu_ext[HALO:HALO + tm, :] = glu(cur)
    glu_ext[HALO + tm:2 * HALO + tm, :] = glu(nxt) * keep_next
    sc_ext[0:HALO, :] = gated_x(prv) * keep_prev
    sc_ext[HALO:HALO + tm, :] = gated_x(cur)
    sc_ext[HALO + tm:2 * HALO + tm, :] = gated_x(nxt) * keep_next

    chunk = 32
    window = chunk + 2 * HALO
    conv_rows = []
    for r in range(0, tm, chunk):
        win = glu_ext[r:r + window, :]
        acc = jnp.zeros((chunk, cw), F32)
        for res in range(SUBLANES):
            shifted = win if res == 0 else pltpu.roll(win, window - res, 0)
            for k in range(CONV_K):
                off = HALO - CONV_K // 2 + k
                if off % SUBLANES == res:
                    acc = acc + dww_ref[k:k + 1, :] * shifted[off - res:off - res + chunk, :]
        conv_rows.append(acc)
    u = jnp.concatenate(conv_rows, axis=0) + dwb_ref[...]
    mu = jnp.mean(u, axis=-1, keepdims=True)
    uc = u - mu
    var = jnp.mean(uc * uc, axis=-1, keepdims=True)
    u = _silu(uc * lax.rsqrt(var + EPS) * lng_ref[...] + lnb_ref[...])
    y_b = _dot(u.astype(BF16), wc_ref[0])

    c3 = jnp.zeros((tm, sw), F32)
    for k in range(SC_K):
        base = HALO - SC_K // 2 + k
        c3 = c3 + scw_ref[k:k + 1, :] * sc_ext[base:base + tm, :]
    sc_b = cur[:, 2 * cw:2 * cw + sw].astype(F32)
    y_c = _dot((sc_b * c3).astype(BF16), ws_ref[0])

    attn = jnp.concatenate([attn_ref[c] for c in range(N_HEAD_PAIRS)], axis=1)
    y_a = _dot(attn, wa_ref[0])

    g = pr_ref[:, pcw:pcw + 3 * D]
    merged = (_sigmoid(g[:, 0:D].astype(F32)) * y_a
              + _sigmoid(g[:, D:2 * D].astype(F32)) * y_b
              + _sigmoid(g[:, 2 * D:3 * D].astype(F32)) * y_c)
    o_ref[...] = merged.astype(BF16)


def _branch(attn, pr, wa, wc, ws, l, dww, dwb, lng, lnb, scw, seq_len, D):
    rows, prw = pr.shape
    tm = 256
    pcw = 2 * CONV_WIDTH + 3 * SC_WIDTH
    nhalo = rows // HALO
    const = lambda i: (0, 0)
    layer = lambda i: (l, 0, 0)
    return pl.pallas_call(
        functools.partial(_branch_kernel, tm=tm, seq_len=seq_len, D=D),
        grid=(rows // tm,),
        in_specs=[pl.BlockSpec((N_HEAD_PAIRS, tm, LANES), lambda i: (0, i, 0)),
                  pl.BlockSpec((tm, prw), lambda i: (i, 0)),
                  pl.BlockSpec((HALO, pcw), lambda i: (jnp.maximum(i * (tm // HALO) - 1, 0), 0)),
                  pl.BlockSpec((HALO, pcw), lambda i: (jnp.minimum((i + 1) * (tm // HALO), nhalo - 1), 0)),
                  pl.BlockSpec((1, ATTN_WIDTH, D), layer),
                  pl.BlockSpec((1, CONV_WIDTH, D), layer),
                  pl.BlockSpec((1, SC_WIDTH, D), layer),
                  pl.BlockSpec((32, CONV_WIDTH), const),
                  pl.BlockSpec((1, CONV_WIDTH), const),
                  pl.BlockSpec((1, CONV_WIDTH), const),
                  pl.BlockSpec((1, CONV_WIDTH), const),
                  pl.BlockSpec((8, SC_WIDTH), const)],
        out_specs=pl.BlockSpec((tm, D), lambda i: (i, 0)),
        out_shape=jax.ShapeDtypeStruct((rows, D), BF16),
        scratch_shapes=[pltpu.VMEM((tm + 2 * HALO, CONV_WIDTH), F32),
                        pltpu.VMEM((tm + 2 * HALO, SC_WIDTH), F32)],
        compiler_params=_cparams("arbitrary"), name="branch",
    )(attn, pr, pr, pr, wa, wc, ws, dww, dwb, lng, lnb, scw)


def _pack_halves(x):
    half = x.shape[1] // 2
    bits = lax.bitcast_convert_type(x.astype(F32), jnp.uint32)
    return (bits[:, :half] >> 16) | (bits[:, half:] & jnp.uint32(0xFFFF0000))


def _unpack_halves(words):
    lo = lax.bitcast_convert_type(words << 16, F32).astype(BF16)
    hi = lax.bitcast_convert_type(words & jnp.uint32(0xFFFF0000), F32).astype(BF16)
    return jnp.concatenate([lo, hi], axis=1)


def _outproj_kernel(m_ref, x_ref, wo_ref, gate_ref, g2_ref, sh_ref, sc_ref, wr_ref, xmid_ref, h2_ref, aff_ref):
    x = x_ref[...] + gate_ref[0] * _dot(m_ref[...], wo_ref[0])
    xmid_ref[...] = x
    ms = jnp.mean(x * x, axis=-1, keepdims=True)
    y = x * lax.rsqrt(ms + EPS) * g2_ref[...]
    h2 = (y * (1.0 + sc_ref[0]) + sh_ref[0]).astype(BF16)
    h2_ref[...] = _pack_halves(h2)
    logits = _dot(h2, wr_ref[...])
    lane = lax.broadcasted_iota(jnp.int32, logits.shape, 1)
    logits = jnp.where(lane < N_EXPERTS, logits, NEG_INF)
    e = jnp.exp(logits - jnp.max(logits, axis=-1, keepdims=True))
    aff_ref[...] = e / jnp.sum(e, axis=-1, keepdims=True)


def _outproj(merged, x, wo, l, gate, g2, shift, scale, wr, seq_len):
    rows, D = x.shape
    tm = 256
    mspec = _mod_spec(gate.shape[0], seq_len // tm, D)
    row = lambda i: (i, 0)
    const = lambda i: (0, 0)
    return pl.pallas_call(
        _outproj_kernel, grid=(rows // tm,),
        in_specs=[pl.BlockSpec((tm, D), row), pl.BlockSpec((tm, D), row),
                  pl.BlockSpec((1, D, D), lambda i: (l, 0, 0)), mspec,
                  pl.BlockSpec((1, D), const), mspec, mspec,
                  pl.BlockSpec((D, LANES), const)],
        out_specs=[pl.BlockSpec((tm, D), row), pl.BlockSpec((tm, D // 2), row), pl.BlockSpec((tm, LANES), row)],
        out_shape=[jax.ShapeDtypeStruct((rows, D), F32), jax.ShapeDtypeStruct((rows, D // 2), jnp.uint32),
                   jax.ShapeDtypeStruct((rows, LANES), F32)],
        compiler_params=_cparams("arbitrary"), name="outproj",
    )(merged, x, wo, gate, g2, shift, scale, wr)


MAX_BISECT = 160
POS_LANE = 3 * N_EXPERTS


def _select_kernel(aff_ref, tri_ref, tv_ref, idx_ref, g_ref, s1_ref, r_ref, *, n, cap, blk, tchunk):
    E = N_EXPERTS
    aff = aff_ref[0]
    a = aff.T[0:E, :]
    capf = float(cap)

    def cond(st):
        it, _, _, ndone = st
        return jnp.logical_and(it < MAX_BISECT, ndone < float(E))

    def in_range(lo, hi):
        return jnp.logical_and(a >= lo, a < hi)

    def body(st):
        it, lo, hi, _ = st
        mid = 0.5 * (lo + hi)
        cnt = jnp.sum((a >= mid).astype(F32), axis=-1, keepdims=True)
        ge = cnt >= capf
        lo = jnp.where(ge, mid, lo)
        hi = jnp.where(ge, hi, mid)
        r = in_range(lo, hi)
        vmin = jnp.min(jnp.where(r, a, 4.0), axis=-1, keepdims=True)
        vmax = jnp.max(jnp.where(r, a, -1.0), axis=-1, keepdims=True)
        ndone = jnp.sum((vmin == vmax).astype(F32))
        return it + 1, lo, hi, ndone

    _, lo, hi, _ = lax.while_loop(cond, body, (jnp.int32(0), jnp.zeros((E, 1), F32), jnp.full((E, 1), 2.0, F32),
                                               jnp.float32(0.0)))
    thr = jnp.max(jnp.where(in_range(lo, hi), a, -1.0), axis=-1, keepdims=True)

    def cumsum(x):
        outs, carry = [], jnp.zeros((E, 1), F32)
        for c in range(n // blk):
            part = _dot(x[:, c * blk:(c + 1) * blk].astype(BF16), tri_ref[...]) + carry
            outs.append(part)
            carry = part[:, blk - 1:blk]
        return jnp.concatenate(outs, axis=1)

    gt = a > thr
    eq = (a == thr).astype(F32)
    need = capf - jnp.sum(gt.astype(F32), axis=-1, keepdims=True)
    eq_rank = cumsum(eq) - eq
    sel = jnp.logical_or(gt, jnp.logical_and(eq > 0.0, eq_rank < need)).astype(F32)
    s1_ref[...] = sel * cumsum(sel)

    a_hi = aff.astype(BF16).astype(F32)
    rest = aff - a_hi
    a_mid = rest.astype(BF16).astype(F32)
    a_lo = rest - a_mid
    r_ref[...] = (a_hi + pltpu.roll(a_mid, E, 1) + pltpu.roll(a_lo, 2 * E, 1) + tv_ref[...]).astype(BF16)

    slot_ids = lax.broadcasted_iota(jnp.int32, (cap, 1), 0).astype(F32) + 1.0
    lane = lax.broadcasted_iota(jnp.int32, (1, LANES), 1)

    def per_expert(e, carry):
        row = s1_ref[pl.ds(e, 1), :]
        acc = jnp.zeros((cap, LANES), F32)
        for c in range(n // tchunk):
            onehot = (row[:, c * tchunk:(c + 1) * tchunk] == slot_ids).astype(BF16)
            acc = acc + _dot(onehot, r_ref[c * tchunk:(c + 1) * tchunk, :])
        mine = jnp.logical_and(lane % E == e, lane < POS_LANE)
        g = jnp.sum(jnp.where(mine, acc, 0.0), axis=1, keepdims=True)
        tok = jnp.sum(jnp.where(lane == POS_LANE, acc * 64.0, jnp.where(lane == POS_LANE + 1, acc, 0.0)),
                      axis=1, keepdims=True)
        g_ref[0, e] = jnp.broadcast_to(g, (cap, LANES))
        idx_ref[0, e] = jnp.broadcast_to(tok, (cap, LANES)).astype(jnp.int32)
        return carry

    lax.fori_loop(0, E, per_expert, 0)


def _select(aff, B, n, cap):
    blk = min(512, n)
    tchunk = min(1024, n)
    tri = (np.arange(blk)[:, None] <= np.arange(blk)[None, :]).astype(np.float32)
    tv = np.zeros((n, LANES), np.float32)
    tv[:, POS_LANE] = np.arange(n) // 64
    tv[:, POS_LANE + 1] = np.arange(n) % 64
    out = pl.BlockSpec((1, N_EXPERTS, cap, LANES), lambda b: (b, 0, 0, 0))
    return pl.pallas_call(
        functools.partial(_select_kernel, n=n, cap=cap, blk=blk, tchunk=tchunk),
        grid=(B,),
        in_specs=[pl.BlockSpec((1, n, LANES), lambda b: (b, 0, 0)),
                  pl.BlockSpec((blk, blk), lambda b: (0, 0)),
                  pl.BlockSpec((n, LANES), lambda b: (0, 0))],
        out_specs=[out, out],
        out_shape=[jax.ShapeDtypeStruct((B, N_EXPERTS, cap, LANES), jnp.int32),
                   jax.ShapeDtypeStruct((B, N_EXPERTS, cap, LANES), F32)],
        scratch_shapes=[pltpu.VMEM((N_EXPERTS, n), F32), pltpu.VMEM((n, LANES), BF16)],
        compiler_params=_cparams("arbitrary"), name="select",
    )(aff.reshape(B, n, LANES), jnp.asarray(tri, BF16), jnp.asarray(tv))


MIN_CHUNK_ROWS = 256


def _expert_kernel(idx_ref, g_ref, gate_ref, w1_ref, w3_ref, w2_ref, h_hbm, acc_in, acc_hbm,
                   xs_buf, acc_buf, sem, *, bb, cap, n, gate_rows, n_chunks):
    del acc_in
    e = pl.program_id(0)
    b0 = pl.program_id(1) * bb
    rows = bb * cap
    D = acc_buf.shape[2]

    def token_row(j):
        b = b0 + j // cap
        return b * n + idx_ref[(b * N_EXPERTS + e) * cap + j % cap]

    def loop_rows(fn, lo, hi):
        for bl in range(bb):
            seg_lo, seg_hi = max(lo, bl * cap), min(hi, (bl + 1) * cap)
            if seg_lo >= seg_hi:
                continue
            b = b0 + bl
            base = (b * N_EXPERTS + e) * cap - bl * cap

            def body(t, carry, b=b, base=base):
                for u in range(SUBLANES):
                    fn(t, u, b * n + idx_ref[base + t * SUBLANES + u])
                return carry

            lax.fori_loop(seg_lo // SUBLANES, seg_hi // SUBLANES, body, 0)

    def inline_rows(fn, lo, hi):
        for j in range(lo, hi):
            fn(j // SUBLANES, j % SUBLANES, token_row(j))

    def x_copy(t, u, r):
        return pltpu.make_async_copy(h_hbm.at[pl.ds(r, 1), :], xs_buf.at[t, pl.ds(u, 1), :], sem.at[0])

    def acc_load(t, u, r):
        return pltpu.make_async_copy(acc_hbm.at[pl.ds(r, 1), :], acc_buf.at[t, pl.ds(u, 1), :], sem.at[1])

    def acc_store(t, u, r):
        return pltpu.make_async_copy(acc_buf.at[t, pl.ds(u, 1), :], acc_hbm.at[pl.ds(r, 1), :], sem.at[2])

    def start_loads(t, u, r):
        x_copy(t, u, r).start()
        acc_load(t, u, r).start()

    def wait_loads(t, u, r):
        x_copy(t, u, r).wait()
        acc_load(t, u, r).wait()

    def start_store(t, u, r):
        acc_store(t, u, r).start()

    def wait_store(t, u, r):
        acc_store(t, u, r).wait()

    g = g_ref[...].reshape(rows, LANES)

    def compute(lo, hi):
        ts = slice(lo // SUBLANES, hi // SUBLANES)
        x = _unpack_halves(xs_buf[ts].reshape(hi - lo, D // 2))
        hidden = (_silu(_dot(x, w1_ref[0, 0])) * _dot(x, w3_ref[0, 0])).astype(BF16)
        y = _dot(hidden, w2_ref[0, 0]) * jnp.concatenate([g[lo:hi]] * (D // LANES), axis=1)
        for bl in range(bb):
            seg_lo, seg_hi = max(lo, bl * cap), min(hi, (bl + 1) * cap)
            if seg_lo >= seg_hi:
                continue
            gate_row = gate_ref[bl if gate_rows > 1 else 0]
            seg = slice(seg_lo // SUBLANES, seg_hi // SUBLANES)
            upd = y[seg_lo - lo:seg_hi - lo] * gate_row
            acc_buf[seg] += upd.reshape((seg_hi - seg_lo) // SUBLANES, SUBLANES, D)

    bounds = [(k * rows // n_chunks, (k + 1) * rows // n_chunks) for k in range(n_chunks)]
    loop_rows(start_loads, *bounds[0])
    loop_rows(wait_loads, *bounds[0])
    for k in range(n_chunks):
        if k + 1 < n_chunks:
            inline_rows(start_loads, *bounds[k + 1])
        if k >= 1:
            inline_rows(start_store, *bounds[k - 1])
        compute(*bounds[k])
        if k + 1 < n_chunks:
            loop_rows(wait_loads, *bounds[k + 1])
    loop_rows(start_store, *bounds[-1])
    loop_rows(wait_store, 0, rows)


def _experts(idx, g, gate, w1, w3, w2, l, h2, x_mid, B, n, cap, bb):
    _, E, D, FF = w1.shape
    rows = bb * cap
    gate_rows = bb if gate.shape[0] > 1 else 1
    gate_spec = (pl.BlockSpec((bb, 1, D), lambda e, b, idx: (b, 0, 0)) if gate.shape[0] > 1
                 else pl.BlockSpec((1, 1, D), lambda e, b, idx: (0, 0, 0)))
    grid_spec = pltpu.PrefetchScalarGridSpec(
        num_scalar_prefetch=1,
        grid=(E, B // bb),
        in_specs=[pl.BlockSpec((bb, 1, cap, LANES), lambda e, b, idx: (b, e, 0, 0)),
                  gate_spec,
                  pl.BlockSpec((1, 1, D, FF), lambda e, b, idx: (l, e, 0, 0)),
                  pl.BlockSpec((1, 1, D, FF), lambda e, b, idx: (l, e, 0, 0)),
                  pl.BlockSpec((1, 1, FF, D), lambda e, b, idx: (l, e, 0, 0)),
                  pl.BlockSpec(memory_space=pl.ANY),
                  pl.BlockSpec(memory_space=pl.ANY)],
        out_specs=pl.BlockSpec(memory_space=pl.ANY),
        scratch_shapes=[pltpu.VMEM((rows // SUBLANES, SUBLANES, D // 2), jnp.uint32),
                        pltpu.VMEM((rows // SUBLANES, SUBLANES, D), F32),
                        pltpu.SemaphoreType.DMA((3,))])
    return pl.pallas_call(
        functools.partial(_expert_kernel, bb=bb, cap=cap, n=n, gate_rows=gate_rows,
                          n_chunks=2 if rows >= 2 * MIN_CHUNK_ROWS else 1),
        grid_spec=grid_spec,
        out_shape=jax.ShapeDtypeStruct(x_mid.shape, F32),
        input_output_aliases={7: 0},
        compiler_params=_cparams("arbitrary", "arbitrary"), name="experts",
    )(idx, g, gate, w1, w3, w2, h2, x_mid)


def _moe(h2, aff, x_mid, gate, w1, w3, w2, l, B, n, bb):
    cap = max(1, CAP_FACTOR * n // N_EXPERTS)
    idx, g = _select(aff, B, n, cap)
    return _experts(idx[..., 0].reshape(-1), g, gate, w1, w3, w2, l, h2, x_mid, B, n, cap, bb)


def _rope_tables(S):
    half, quarter = HEAD_DIM // 2, HEAD_DIM // 4
    t = jnp.arange(S)
    freqs = 1.0 / (ROPE_BASE ** (jnp.arange(quarter, dtype=F32) / quarter))
    d = np.arange(LANES) % HEAD_DIM
    pos = jnp.where(jnp.asarray(d < half)[None, :], (t // GRID_W)[:, None], (t % GRID_W)[:, None]).astype(F32)
    ang = pos * freqs[jnp.asarray(d % quarter)][None, :]
    first = jnp.asarray((d % half) < quarter)[None, :]
    sin = jnp.sin(ang)
    return jnp.cos(ang), jnp.where(first, -sin, 0.0), jnp.where(first, 0.0, sin)


def _bias_pair_blocks(rpb):
    cq = np.arange(GRID_W)
    ck = np.arange(GRID_W)
    cstart = np.clip(cq - WIN_COLS // 2, 0, GRID_W - WIN_COLS)
    valid = (ck[None, :] >= cstart[:, None]) & (ck[None, :] < cstart[:, None] + WIN_COLS)
    col_off = np.clip(ck[None, :] - cq[:, None] + WIN_COLS - 1, 0, 2 * WIN_COLS - 2)
    place = (col_off[None] == np.arange(2 * WIN_COLS - 1)[:, None, None]).astype(np.float32)
    ro = np.clip(np.arange(N_PAIR_BLOCKS + 1) - 8, 0, 2 * WIN_ROWS - 2)
    T = jnp.einsum('hrd,dck->hrck', rpb[:, ro], jnp.asarray(place), precision=lax.Precision.HIGHEST)
    T = jnp.where(jnp.asarray(valid)[None, None], T * LOG2E, NEG_INF)
    return jnp.concatenate([T[:, :-1], T[:, 1:]], axis=-1)


def _head_sum_tables():
    lane = np.arange(ATTN_WIDTH)
    bd = (lane[:, None] // HEAD_DIM == np.arange(LANES)[None, :]).astype(np.float32)
    return jnp.asarray(bd, BF16), jnp.asarray(bd.T, BF16)


def kernel(x, c, ctx, c_ctx, w_ada, b_ada, g_norm1, g_norm2, w_in, b_in, g_q, g_k, rpb, w_attn_o, conv_dw_w,
           conv_dw_b, conv_ln_g, conv_ln_b, w_conv_o, sc_w, w_sc_o, w_o, w_router, w_e1, w_e3, w_e2):
    B, S, D = x.shape
    C = ctx.shape[1]
    L = w_ada.shape[0]
    aw3 = 3 * ATTN_WIDTH
    prw = 2 * CONV_WIDTH + 3 * SC_WIDTH + 3 * D

    cc = jnp.zeros((8, D), F32).at[0:B].set(c).at[B].set(c_ctx)
    mod = _ada(cc, w_ada, b_ada)
    rope_tabs = _rope_tables(S)
    bd, ex = _head_sum_tables()

    xl = x.reshape(B * S, D)
    xc = ctx.reshape(B * C, D)
    b_in3 = b_in.reshape(L, 1, -1)
    w_qkv = w_in[:, :, :aw3].astype(BF16)
    wa, wc, ws, wo = (w.astype(BF16) for w in (w_attn_o, w_conv_o, w_sc_o, w_o))
    we = (w_e1.astype(BF16), w_e3.astype(BF16), w_e2.astype(BF16))

    for l in range(L):
        last = l == L - 1
        ml = lambda j: mod[l, 0:B, j * D:(j + 1) * D].reshape(B, 1, D)
        mc = lambda j: mod[l, B:B + 1, j * D:(j + 1) * D].reshape(1, 1, D)
        g1 = g_norm1[l].reshape(1, D)
        g2 = g_norm2[l].reshape(1, D)
        b_qkv = b_in[l][:aw3].reshape(1, aw3)
        gq = jnp.tile(g_q[l], N_HEADS).reshape(1, ATTN_WIDTH)
        gk = jnp.tile(g_k[l], N_HEADS).reshape(1, ATTN_WIDTH)
        branch_w = (wa, wc, ws, l,
                    jnp.zeros((32, CONV_WIDTH), F32).at[:CONV_K].set(conv_dw_w[l]),
                    conv_dw_b[l].reshape(1, -1), conv_ln_g[l].reshape(1, -1), conv_ln_b[l].reshape(1, -1),
                    jnp.zeros((8, SC_WIDTH), F32).at[:SC_K].set(sc_w[l]))
        wr = jnp.zeros((D, LANES), F32).at[:, :N_EXPERTS].set(w_router[l]).astype(BF16)

        hc = _normmod(xc, g1, mc(0), mc(1), C)
        qc, kc, vc = _qkv(hc, w_qkv, l, b_qkv, gq, gk, bd, ex, None, C)
        if not last:
            attn_c = _ctx_attn(qc, kc, vc, B, C)
            pr_c = _proj(hc, w_in, b_in3, l, aw3, prw)
            merged_c = _branch(attn_c, pr_c, *branch_w, C, D)
            xc_mid, hc2, aff_c = _outproj(merged_c, xc, wo, l, mc(2), g2, mc(3), mc(4), wr, C)
            xc = _moe(hc2, aff_c, xc_mid, mc(5), *we, l, B, C, B)

        h = _normmod(xl, g1, ml(0), ml(1), S)
        qn, kn, v, qr, kr = _qkv(h, w_qkv, l, b_qkv, gq, gk, bd, ex, rope_tabs, S)
        attn = _attn(qn, qr, kr, v, kc, vc, _bias_pair_blocks(rpb[l]), B, S, C)
        pr = _proj(h, w_in, b_in3, l, aw3, prw)
        merged = _branch(attn, pr, *branch_w, S, D)
        x_mid, h2, aff = _outproj(merged, xl, wo, l, ml(2), g2, ml(3), ml(4), wr, S)
        xl = _moe(h2, aff, x_mid, ml(5), *we, l, B, S, 1)

    return xl.reshape(B, S, D)
```

```python
import functools

import numpy as np
import jax
import jax.numpy as jnp
from jax import lax
from jax.experimental import pallas as pl
from jax.experimental.pallas import tpu as pltpu

F32 = jnp.float32
BF16 = jnp.bfloat16

GRID_W = 64
N_HEADS = 16
HEAD_DIM = 64
ATTN_WIDTH = N_HEADS * HEAD_DIM
N_HEAD_PAIRS = N_HEADS // 2
WIN_ROWS = 8
WIN_COLS = 16
ROPE_BASE = 10000.0
CONV_WIDTH = 512
CONV_K = 31
SC_WIDTH = 512
SC_K = 3
N_EXPERTS = 16
CAP_FACTOR = 2
EPS = 1e-6
NEG_INF = -1e30

LANES = 128
SUBLANES = 8
LOG2E = 1.4426950408889634
Q_SCALE = HEAD_DIM ** -0.5 * LOG2E
HALO = 16
ROW_GROUP = 4
KEY_ROWS = 12
N_PAIR_BLOCKS = 26
VMEM_LIMIT = 56 * 1024 * 1024


def _cparams(*sem):
    return pltpu.CompilerParams(dimension_semantics=sem, vmem_limit_bytes=VMEM_LIMIT)


def _dot(a, b):
    return jnp.dot(a, b, preferred_element_type=F32)


def _dot_t(a, b):
    return lax.dot_general(a, b, (((1,), (1,)), ((), ())), preferred_element_type=F32)


def _sigmoid(x):
    return 0.5 * jnp.tanh(0.5 * x) + 0.5


def _silu(x):
    return x * _sigmoid(x)


def _ada_kernel(cc_ref, w_ref, b_ref, o_ref):
    s = _silu(cc_ref[...]).astype(BF16)
    o_ref[0] = _dot(s, w_ref[0].astype(BF16)) + b_ref[0]


def _ada(cc, w_ada, b_ada):
    L, D, W = w_ada.shape
    tn = 1024
    return pl.pallas_call(
        _ada_kernel,
        grid=(L, W // tn),
        in_specs=[pl.BlockSpec((8, D), lambda l, j: (0, 0)),
                  pl.BlockSpec((1, D, tn), lambda l, j: (l, 0, j)),
                  pl.BlockSpec((1, 1, tn), lambda l, j: (l, 0, j))],
        out_specs=pl.BlockSpec((1, 8, tn), lambda l, j: (l, 0, j)),
        out_shape=jax.ShapeDtypeStruct((L, 8, W), F32),
        compiler_params=_cparams("arbitrary", "arbitrary"),
        name="ada",
    )(cc, w_ada, b_ada.reshape(L, 1, W))


def _mod_spec(nmod, tiles_per_mod, D):
    if nmod == 1:
        return pl.BlockSpec((1, 1, D), lambda i: (0, 0, 0))
    return pl.BlockSpec((1, 1, D), lambda i: (i // tiles_per_mod, 0, 0))


def _normmod_kernel(x_ref, g_ref, sh_ref, sc_ref, h_ref):
    x = x_ref[...]
    ms = jnp.mean(x * x, axis=-1, keepdims=True)
    y = x * lax.rsqrt(ms + EPS) * g_ref[...]
    h_ref[...] = (y * (1.0 + sc_ref[0]) + sh_ref[0]).astype(BF16)


def _normmod(x, g, shift, scale, seq_len):
    rows, D = x.shape[0], g.shape[1]
    tm = min(512, seq_len)
    mspec = _mod_spec(shift.shape[0], seq_len // tm, D)
    xspec = pl.BlockSpec((tm, D), lambda i: (i, 0))
    return pl.pallas_call(
        _normmod_kernel, grid=(rows // tm,),
        in_specs=[xspec, pl.BlockSpec((1, D), lambda i: (0, 0)), mspec, mspec],
        out_specs=xspec, out_shape=jax.ShapeDtypeStruct((rows, D), BF16),
        compiler_params=_cparams("arbitrary"), name="normmod",
    )(x, g, shift, scale)


def _qkv_kernel(*refs, rope):
    if rope:
        (h_ref, w_ref, b_ref, gq_ref, gk_ref, bd_ref, ex_ref, cos_ref, sa_ref, sb_ref,
         qn_ref, kn_ref, v_ref, qr_ref, kr_ref) = refs
    else:
        h_ref, w_ref, b_ref, gq_ref, gk_ref, bd_ref, ex_ref, qn_ref, kn_ref, v_ref = refs
    h = h_ref[...]
    aw = ATTN_WIDTH

    def proj(s):
        return _dot(h, w_ref[0, :, s * aw:(s + 1) * aw]) + b_ref[:, s * aw:(s + 1) * aw]

    def headnorm(a, g_ref):
        ss = _dot((a * a).astype(BF16), bd_ref[...])
        inv = lax.rsqrt(ss * (1.0 / HEAD_DIM) + EPS)
        inv_hi = inv.astype(BF16)
        inv_lo = (inv - inv_hi.astype(F32)).astype(BF16)
        full = _dot(inv_hi, ex_ref[...]) + _dot(inv_lo, ex_ref[...])
        return a * full * g_ref[...]

    def store_plain(xn, out_ref):
        for c in range(N_HEAD_PAIRS):
            out_ref[c] = xn[:, c * LANES:(c + 1) * LANES].astype(BF16)

    def store_rope(xn, out_ref):
        cos, sa, sb = cos_ref[...], sa_ref[...], sb_ref[...]
        for c in range(N_HEAD_PAIRS):
            xc = xn[:, c * LANES:(c + 1) * LANES]
            r = xc * cos + pltpu.roll(xc, LANES - 16, 1) * sa + pltpu.roll(xc, 16, 1) * sb
            out_ref[c] = r.astype(BF16)

    qn = headnorm(proj(0), gq_ref) * Q_SCALE
    store_plain(qn, qn_ref)
    if rope:
        store_rope(qn, qr_ref)
    kn = headnorm(proj(1), gk_ref)
    store_plain(kn, kn_ref)
    if rope:
        store_rope(kn, kr_ref)
    store_plain(proj(2), v_ref)


def _qkv(h, w, l, b, gq, gk, bd, ex, rope_tabs, seq_len):
    rows, D = h.shape
    tm = min(512, seq_len)
    rope = rope_tabs is not None
    const = lambda i: (0, 0)
    in_specs = [pl.BlockSpec((tm, D), lambda i: (i, 0)),
                pl.BlockSpec((1, D, 3 * ATTN_WIDTH), lambda i: (l, 0, 0), pipeline_mode=pl.Buffered(1)),
                pl.BlockSpec((1, 3 * ATTN_WIDTH), const),
                pl.BlockSpec((1, ATTN_WIDTH), const),
                pl.BlockSpec((1, ATTN_WIDTH), const),
                pl.BlockSpec((ATTN_WIDTH, LANES), const),
                pl.BlockSpec((LANES, ATTN_WIDTH), const)]
    ins = [h, w, b, gq, gk, bd, ex]
    n_out = 3
    if rope:
        tiles_per_seq = seq_len // tm
        tspec = pl.BlockSpec((tm, LANES), lambda i: (i % tiles_per_seq, 0))
        in_specs += [tspec, tspec, tspec]
        ins += list(rope_tabs)
        n_out = 5
    ospec = pl.BlockSpec((N_HEAD_PAIRS, tm, LANES), lambda i: (0, i, 0))
    return pl.pallas_call(
        functools.partial(_qkv_kernel, rope=rope),
        grid=(rows // tm,), in_specs=in_specs,
        out_specs=[ospec] * n_out,
        out_shape=[jax.ShapeDtypeStruct((N_HEAD_PAIRS, rows, LANES), BF16)] * n_out,
        compiler_params=_cparams("arbitrary"), name="qkv",
    )(*ins)


def _proj_kernel(h_ref, w_ref, b_ref, o_ref, wb_ref):
    @pl.when(pl.program_id(1) == 0)
    def _():
        wb_ref[...] = w_ref[0].astype(BF16)

    o_ref[...] = (_dot(h_ref[...], wb_ref[...]) + b_ref[0]).astype(o_ref.dtype)


def _proj(h, w, b, l, col0, ncols):
    rows, D = h.shape
    tm = min(2048, rows)
    tn = 512
    j0 = col0 // tn
    return pl.pallas_call(
        _proj_kernel, grid=(ncols // tn, rows // tm),
        in_specs=[pl.BlockSpec((tm, D), lambda j, i: (i, 0)),
                  pl.BlockSpec((1, D, tn), lambda j, i: (l, 0, j + j0)),
                  pl.BlockSpec((1, 1, tn), lambda j, i: (l, 0, j + j0))],
        out_specs=pl.BlockSpec((tm, tn), lambda j, i: (i, j)),
        out_shape=jax.ShapeDtypeStruct((rows, ncols), BF16),
        scratch_shapes=[pltpu.VMEM((D, tn), BF16)],
        compiler_params=_cparams("arbitrary", "arbitrary"), name="proj",
    )(h, w, b)


def _attn_kernel(qn_ref, qr_ref, kr_ref, v_ref, kc_ref, vc_ref, fp_ref, o_ref, *, rows):
    gq = ROW_GROUP * GRID_W
    nk = KEY_ROWS * GRID_W
    lane = lax.broadcasted_iota(jnp.int32, (1, LANES), 1)
    head_mask = (lane < HEAD_DIM, lane >= HEAD_DIM)
    key_row = lax.broadcasted_iota(jnp.int32, (1, nk), 1) // GRID_W
    kc = kc_ref[0]
    vc = [jnp.where(head_mask[hh], vc_ref[0], 1) for hh in range(2)]

    def group(gi, carry):
        r0 = gi * ROW_GROUP
        ws = jnp.clip(r0 - WIN_ROWS // 2, 0, rows - KEY_ROWS)
        tok0 = pl.multiple_of(gi * gq, gq)
        key0 = pl.multiple_of(ws * GRID_W, GRID_W)
        qr = qr_ref[0, pl.ds(tok0, gq), :]
        qn = qn_ref[0, pl.ds(tok0, gq), :]
        kw = kr_ref[0, pl.ds(key0, nk), :]
        vw = v_ref[0, pl.ds(key0, nk), :]
        scores = [(_dot_t(jnp.where(head_mask[hh], qr, 0), kw),
                   _dot_t(jnp.where(head_mask[hh], qn, 0), kc))
                  for hh in range(2)]
        outs = []
        for hh in range(2):
            s_loc, s_ctx = scores[hh]
            slabs = []
            for i in range(ROW_GROUP):
                n0 = ws - r0 - i + (WIN_ROWS - 1) + 8
                bias = jnp.concatenate([fp_ref[hh, n0 + 2 * jp] for jp in range(KEY_ROWS // 2)], axis=1)
                lo = jnp.clip(r0 + i - WIN_ROWS // 2, 0, rows - WIN_ROWS) - ws
                valid = (key_row >= lo) & (key_row < lo + WIN_ROWS)
                slabs.append(jnp.where(valid, s_loc[i * GRID_W:(i + 1) * GRID_W] + bias, NEG_INF))
            s_loc = jnp.concatenate(slabs, axis=0)
            m = jnp.maximum(jnp.max(s_loc, axis=-1, keepdims=True), jnp.max(s_ctx, axis=-1, keepdims=True))
            p_loc = jnp.exp2(s_loc - m).astype(BF16)
            p_ctx = jnp.exp2(s_ctx - m).astype(BF16)
            o = _dot(p_loc, jnp.where(head_mask[hh], vw, 1)) + _dot(p_ctx, vc[hh])
            outs.append(o / pltpu.roll(o, HEAD_DIM, 1))
        o_ref[0, pl.ds(tok0, gq), :] = jnp.where(head_mask[0], outs[0], outs[1]).astype(BF16)
        return carry

    lax.fori_loop(0, rows // ROW_GROUP, group, 0, unroll=2)


def _attn(qn, qr, kr, v, kc, vc, fp, B, S, C):
    rows = S // GRID_W
    tok = pl.BlockSpec((1, S, LANES), lambda b, p: (p, b, 0))
    ctx = pl.BlockSpec((1, C, LANES), lambda b, p: (p, b, 0))
    return pl.pallas_call(
        functools.partial(_attn_kernel, rows=rows),
        grid=(B, N_HEAD_PAIRS),
        in_specs=[tok, tok, tok, tok, ctx, ctx,
                  pl.BlockSpec((2, N_PAIR_BLOCKS, GRID_W, LANES), lambda b, p: (p, 0, 0, 0))],
        out_specs=tok,
        out_shape=jax.ShapeDtypeStruct((N_HEAD_PAIRS, B * S, LANES), BF16),
        compiler_params=_cparams("arbitrary", "arbitrary"), name="attn",
    )(qn, qr, kr, v, kc, vc, fp)


def _ctx_attn_kernel(q_ref, k_ref, v_ref, o_ref):
    lane = lax.broadcasted_iota(jnp.int32, (1, LANES), 1)
    head_mask = (lane < HEAD_DIM, lane >= HEAD_DIM)
    q, k, v = q_ref[0], k_ref[0], v_ref[0]
    outs = []
    for hh in range(2):
        s = _dot_t(jnp.where(head_mask[hh], q, 0), k)
        m = jnp.max(s, axis=-1, keepdims=True)
        p = jnp.exp2(s - m)
        outs.append(_dot(p.astype(BF16), v) / jnp.sum(p, axis=-1, keepdims=True))
    o_ref[0] = jnp.where(head_mask[0], outs[0], outs[1]).astype(BF16)


def _ctx_attn(q, k, v, B, C):
    spec = pl.BlockSpec((1, C, LANES), lambda b, p: (p, b, 0))
    return pl.pallas_call(
        _ctx_attn_kernel, grid=(B, N_HEAD_PAIRS),
        in_specs=[spec, spec, spec], out_specs=spec,
        out_shape=jax.ShapeDtypeStruct((N_HEAD_PAIRS, B * C, LANES), BF16),
        compiler_params=_cparams("arbitrary", "arbitrary"), name="ctx_attn",
    )(q, k, v)


def _branch_kernel(attn_ref, pr_ref, pcp_ref, pcn_ref, wa_ref, wc_ref, ws_ref,
                   dww_ref, dwb_ref, lng_ref, lnb_ref, scw_ref, o_ref, glu_ext, sc_ext, *, tm, seq_len, D):
    i = pl.program_id(0)
    pos0 = (i * tm) % seq_len
    keep_prev = (pos0 != 0).astype(F32)
    keep_next = (pos0 + tm != seq_len).astype(F32)
    cw, sw = CONV_WIDTH, SC_WIDTH

    def glu(blk):
        return blk[:, 0:cw].astype(F32) * _sigmoid(blk[:, cw:2 * cw].astype(F32))

    def gated_x(blk):
        o = 2 * cw + sw
        return blk[:, o:o + sw].astype(F32) * blk[:, o + sw:o + 2 * sw].astype(F32)

    pcw = 2 * cw + 3 * sw
    cur, prv, nxt = pr_ref[:, 0:pcw], pcp_ref[...], pcn_ref[...]
    glu_ext[0:HALO, :] = glu(prv) * keep_prev
    glu_ext[HALO:HALO + tm, :] = glu(cur)
    glu_ext[HALO + tm:2 * HALO + tm, :] = glu(nxt) * keep_next
    sc_ext[0:HALO, :] = gated_x(prv) * keep_prev
    sc_ext[HALO:HALO + tm, :] = gated_x(cur)
    sc_ext[HALO + tm:2 * HALO + tm, :] = gated_x(nxt) * keep_next

    chunk = 32
    window = chunk + 2 * HALO
    conv_rows = []
    for r in range(0, tm, chunk):
        win = glu_ext[r:r + window, :]
        acc = jnp.zeros((chunk, cw), F32)
        for res in range(SUBLANES):
            shifted = win if res == 0 else pltpu.roll(win, window - res, 0)
            for k in range(CONV_K):
                off = HALO - CONV_K // 2 + k
                if off % SUBLANES == res:
                    acc = acc + dww_ref[k:k + 1, :] * shifted[off - res:off - res + chunk, :]
        conv_rows.append(acc)
    u = jnp.concatenate(conv_rows, axis=0) + dwb_ref[...]
    mu = jnp.mean(u, axis=-1, keepdims=True)
    uc = u - mu
    var = jnp.mean(uc * uc, axis=-1, keepdims=True)
    u = _silu(uc * lax.rsqrt(var + EPS) * lng_ref[...] + lnb_ref[...])
    y_b = _dot(u.astype(BF16), wc_ref[0])

    c3 = jnp.zeros((tm, sw), F32)
    for k in range(SC_K):
        base = HALO - SC_K // 2 + k
        c3 = c3 + scw_ref[k:k + 1, :] * sc_ext[base:base + tm, :]
    sc_b = cur[:, 2 * cw:2 * cw + sw].astype(F32)
    y_c = _dot((sc_b * c3).astype(BF16), ws_ref[0])

    attn = jnp.concatenate([attn_ref[c] for c in range(N_HEAD_PAIRS)], axis=1)
    y_a = _dot(attn, wa_ref[0])

    g = pr_ref[:, pcw:pcw + 3 * D]
    merged = (_sigmoid(g[:, 0:D].astype(F32)) * y_a
              + _sigmoid(g[:, D:2 * D].astype(F32)) * y_b
              + _sigmoid(g[:, 2 * D:3 * D].astype(F32)) * y_c)
    o_ref[...] = merged.astype(BF16)


def _branch(attn, pr, wa, wc, ws, l, dww, dwb, lng, lnb, scw, seq_len, D):
    rows, prw = pr.shape
    tm = 256
    pcw = 2 * CONV_WIDTH + 3 * SC_WIDTH
    nhalo = rows // HALO
    const = lambda i: (0, 0)
    layer = lambda i: (l, 0, 0)
    return pl.pallas_call(
        functools.partial(_branch_kernel, tm=tm, seq_len=seq_len, D=D),
        grid=(rows // tm,),
        in_specs=[pl.BlockSpec((N_HEAD_PAIRS, tm, LANES), lambda i: (0, i, 0)),
                  pl.BlockSpec((tm, prw), lambda i: (i, 0)),
                  pl.BlockSpec((HALO, pcw), lambda i: (jnp.maximum(i * (tm // HALO) - 1, 0), 0)),
                  pl.BlockSpec((HALO, pcw), lambda i: (jnp.minimum((i + 1) * (tm // HALO), nhalo - 1), 0)),
                  pl.BlockSpec((1, ATTN_WIDTH, D), layer),
                  pl.BlockSpec((1, CONV_WIDTH, D), layer),
                  pl.BlockSpec((1, SC_WIDTH, D), layer),
                  pl.BlockSpec((32, CONV_WIDTH), const),
                  pl.BlockSpec((1, CONV_WIDTH), const),
                  pl.BlockSpec((1, CONV_WIDTH), const),
                  pl.BlockSpec((1, CONV_WIDTH), const),
                  pl.BlockSpec((8, SC_WIDTH), const)],
        out_specs=pl.BlockSpec((tm, D), lambda i: (i, 0)),
        out_shape=jax.ShapeDtypeStruct((rows, D), BF16),
        scratch_shapes=[pltpu.VMEM((tm + 2 * HALO, CONV_WIDTH), F32),
                        pltpu.VMEM((tm + 2 * HALO, SC_WIDTH), F32)],
        compiler_params=_cparams("arbitrary"), name="branch",
    )(attn, pr, pr, pr, wa, wc, ws, dww, dwb, lng, lnb, scw)


def _pack_halves(x):
    half = x.shape[1] // 2
    bits = lax.bitcast_convert_type(x.astype(F32), jnp.uint32)
    return (bits[:, :half] >> 16) | (bits[:, half:] & jnp.uint32(0xFFFF0000))


def _unpack_halves(words):
    lo = lax.bitcast_convert_type(words << 16, F32).astype(BF16)
    hi = lax.bitcast_convert_type(words & jnp.uint32(0xFFFF0000), F32).astype(BF16)
    return jnp.concatenate([lo, hi], axis=1)


def _outproj_kernel(m_ref, x_ref, wo_ref, gate_ref, g2_ref, sh_ref, sc_ref, wr_ref, xh_ref, aff_ref):
    D = x_ref.shape[1]
    x = x_ref[...] + gate_ref[0] * _dot(m_ref[...], wo_ref[0])
    xh_ref[:, 0:D] = x
    ms = jnp.mean(x * x, axis=-1, keepdims=True)
    y = x * lax.rsqrt(ms + EPS) * g2_ref[...]
    h2 = (y * (1.0 + sc_ref[0]) + sh_ref[0]).astype(BF16)
    xh_ref[:, D:] = lax.bitcast_convert_type(_pack_halves(h2), F32)
    logits = _dot(h2, wr_ref[...])
    lane = lax.broadcasted_iota(jnp.int32, logits.shape, 1)
    logits = jnp.where(lane < N_EXPERTS, logits, NEG_INF)
    e = jnp.exp(logits - jnp.max(logits, axis=-1, keepdims=True))
    aff_ref[...] = e / jnp.sum(e, axis=-1, keepdims=True)


def _outproj(merged, x, wo, l, gate, g2, shift, scale, wr, seq_len):
    rows, D = merged.shape
    tm = 256
    mspec = _mod_spec(gate.shape[0], seq_len // tm, D)
    row = lambda i: (i, 0)
    const = lambda i: (0, 0)
    return pl.pallas_call(
        _outproj_kernel, grid=(rows // tm,),
        in_specs=[pl.BlockSpec((tm, D), row), pl.BlockSpec((tm, D), row),
                  pl.BlockSpec((1, D, D), lambda i: (l, 0, 0)), mspec,
                  pl.BlockSpec((1, D), const), mspec, mspec,
                  pl.BlockSpec((D, LANES), const)],
        out_specs=[pl.BlockSpec((tm, D + D // 2), row), pl.BlockSpec((tm, LANES), row)],
        out_shape=[jax.ShapeDtypeStruct((rows, D + D // 2), F32),
                   jax.ShapeDtypeStruct((rows, LANES), F32)],
        compiler_params=_cparams("arbitrary"), name="outproj",
    )(merged, x, wo, gate, g2, shift, scale, wr)


MAX_BISECT = 160
POS_LANE = 3 * N_EXPERTS


def _select_kernel(aff_ref, tri_ref, tv_ref, idx_ref, g_ref, s1_ref, r_ref, *, n, cap, blk, tchunk):
    E = N_EXPERTS
    aff = aff_ref[0]
    a = aff.T[0:E, :]
    capf = float(cap)

    def cond(st):
        it, _, _, ndone = st
        return jnp.logical_and(it < MAX_BISECT, ndone < float(E))

    def in_range(lo, hi):
        return jnp.logical_and(a >= lo, a < hi)

    def body(st):
        it, lo, hi, _ = st
        mid = 0.5 * (lo + hi)
        cnt = jnp.sum((a >= mid).astype(F32), axis=-1, keepdims=True)
        ge = cnt >= capf
        lo = jnp.where(ge, mid, lo)
        hi = jnp.where(ge, hi, mid)
        r = in_range(lo, hi)
        vmin = jnp.min(jnp.where(r, a, 4.0), axis=-1, keepdims=True)
        vmax = jnp.max(jnp.where(r, a, -1.0), axis=-1, keepdims=True)
        ndone = jnp.sum((vmin == vmax).astype(F32))
        return it + 1, lo, hi, ndone

    _, lo, hi, _ = lax.while_loop(cond, body, (jnp.int32(0), jnp.zeros((E, 1), F32), jnp.full((E, 1), 2.0, F32),
                                               jnp.float32(0.0)))
    thr = jnp.max(jnp.where(in_range(lo, hi), a, -1.0), axis=-1, keepdims=True)

    def cumsum(x):
        outs, carry = [], jnp.zeros((E, 1), F32)
        for c in range(n // blk):
            part = _dot(x[:, c * blk:(c + 1) * blk].astype(BF16), tri_ref[...]) + carry
            outs.append(part)
            carry = part[:, blk - 1:blk]
        return jnp.concatenate(outs, axis=1)

    gt = a > thr
    eq = (a == thr).astype(F32)
    need = capf - jnp.sum(gt.astype(F32), axis=-1, keepdims=True)
    eq_rank = cumsum(eq) - eq
    sel = jnp.logical_or(gt, jnp.logical_and(eq > 0.0, eq_rank < need)).astype(F32)
    s1_ref[...] = sel * cumsum(sel)

    a_hi = aff.astype(BF16).astype(F32)
    rest = aff - a_hi
    a_mid = rest.astype(BF16).astype(F32)
    a_lo = rest - a_mid
    r_ref[...] = (a_hi + pltpu.roll(a_mid, E, 1) + pltpu.roll(a_lo, 2 * E, 1) + tv_ref[...]).astype(BF16)

    slot_ids = lax.broadcasted_iota(jnp.int32, (cap, 1), 0).astype(F32) + 1.0
    lane = lax.broadcasted_iota(jnp.int32, (1, LANES), 1)

    def per_expert(e, carry):
        row = s1_ref[pl.ds(e, 1), :]
        acc = jnp.zeros((cap, LANES), F32)
        for c in range(n // tchunk):
            onehot = (row[:, c * tchunk:(c + 1) * tchunk] == slot_ids).astype(BF16)
            acc = acc + _dot(onehot, r_ref[c * tchunk:(c + 1) * tchunk, :])
        mine = jnp.logical_and(lane % E == e, lane < POS_LANE)
        g = jnp.sum(jnp.where(mine, acc, 0.0), axis=1, keepdims=True)
        tok = jnp.sum(jnp.where(lane == POS_LANE, acc * 64.0, jnp.where(lane == POS_LANE + 1, acc, 0.0)),
                      axis=1, keepdims=True)
        g_ref[0, e] = jnp.broadcast_to(g, (cap, LANES))
        idx_ref[0, e] = jnp.broadcast_to(tok, (cap, LANES)).astype(jnp.int32)
        return carry

    lax.fori_loop(0, E, per_expert, 0)


def _select(aff, B, n, cap):
    blk = min(512, n)
    tchunk = min(1024, n)
    tri = (np.arange(blk)[:, None] <= np.arange(blk)[None, :]).astype(np.float32)
    tv = np.zeros((n, LANES), np.float32)
    tv[:, POS_LANE] = np.arange(n) // 64
    tv[:, POS_LANE + 1] = np.arange(n) % 64
    out = pl.BlockSpec((1, N_EXPERTS, cap, LANES), lambda b: (b, 0, 0, 0))
    return pl.pallas_call(
        functools.partial(_select_kernel, n=n, cap=cap, blk=blk, tchunk=tchunk),
        grid=(B,),
        in_specs=[pl.BlockSpec((1, n, LANES), lambda b: (b, 0, 0)),
                  pl.BlockSpec((blk, blk), lambda b: (0, 0)),
                  pl.BlockSpec((n, LANES), lambda b: (0, 0))],
        out_specs=[out, out],
        out_shape=[jax.ShapeDtypeStruct((B, N_EXPERTS, cap, LANES), jnp.int32),
                   jax.ShapeDtypeStruct((B, N_EXPERTS, cap, LANES), F32)],
        scratch_shapes=[pltpu.VMEM((N_EXPERTS, n), F32), pltpu.VMEM((n, LANES), BF16)],
        compiler_params=_cparams("arbitrary"), name="select",
    )(aff.reshape(B, n, LANES), jnp.asarray(tri, BF16), jnp.asarray(tv))


MIN_CHUNK_ROWS = 256


def _expert_kernel(idx_ref, g_ref, gate_ref, w1_ref, w3_ref, w2_ref, xh_in, xh_hbm,
                   buf, sem, *, bb, cap, n, D, gate_rows, n_chunks):
    del xh_in
    e = pl.program_id(0)
    b0 = pl.program_id(1) * bb
    rows = bb * cap

    def token_row(j):
        b = b0 + j // cap
        return b * n + idx_ref[(b * N_EXPERTS + e) * cap + j % cap]

    def loop_rows(fn, lo, hi):
        for bl in range(bb):
            seg_lo, seg_hi = max(lo, bl * cap), min(hi, (bl + 1) * cap)
            if seg_lo >= seg_hi:
                continue
            b = b0 + bl
            base = (b * N_EXPERTS + e) * cap - bl * cap

            def body(t, carry, b=b, base=base):
                for u in range(SUBLANES):
                    fn(t, u, b * n + idx_ref[base + t * SUBLANES + u])
                return carry

            lax.fori_loop(seg_lo // SUBLANES, seg_hi // SUBLANES, body, 0)

    def inline_rows(fn, lo, hi):
        for j in range(lo, hi):
            fn(j // SUBLANES, j % SUBLANES, token_row(j))

    def row_load(t, u, r):
        return pltpu.make_async_copy(xh_hbm.at[pl.ds(r, 1), :], buf.at[t, pl.ds(u, 1), :], sem.at[0])

    def row_store(t, u, r):
        return pltpu.make_async_copy(buf.at[t, pl.ds(u, 1), pl.ds(0, D)], xh_hbm.at[pl.ds(r, 1), pl.ds(0, D)],
                                     sem.at[1])

    def start_loads(t, u, r):
        row_load(t, u, r).start()

    def wait_loads(t, u, r):
        row_load(t, u, r).wait()

    def start_store(t, u, r):
        row_store(t, u, r).start()

    def wait_store(t, u, r):
        row_store(t, u, r).wait()

    g = g_ref[...].reshape(rows, LANES)

    def compute(lo, hi):
        ts = slice(lo // SUBLANES, hi // SUBLANES)
        x = _unpack_halves(lax.bitcast_convert_type(buf[ts, :, D:], jnp.uint32).reshape(hi - lo, D // 2))
        hidden = (_silu(_dot(x, w1_ref[0, 0])) * _dot(x, w3_ref[0, 0])).astype(BF16)
        y = _dot(hidden, w2_ref[0, 0]) * jnp.concatenate([g[lo:hi]] * (D // LANES), axis=1)
        for bl in range(bb):
            seg_lo, seg_hi = max(lo, bl * cap), min(hi, (bl + 1) * cap)
            if seg_lo >= seg_hi:
                continue
            gate_row = gate_ref[bl if gate_rows > 1 else 0]
            seg = slice(seg_lo // SUBLANES, seg_hi // SUBLANES)
            upd = y[seg_lo - lo:seg_hi - lo] * gate_row
            buf[seg, :, 0:D] += upd.reshape((seg_hi - seg_lo) // SUBLANES, SUBLANES, D)

    bounds = [(k * rows // n_chunks, (k + 1) * rows // n_chunks) for k in range(n_chunks)]
    loop_rows(start_loads, *bounds[0])
    loop_rows(wait_loads, *bounds[0])
    for k in range(n_chunks):
        if k + 1 < n_chunks:
            inline_rows(start_loads, *bounds[k + 1])
        if k >= 1:
            inline_rows(start_store, *bounds[k - 1])
        compute(*bounds[k])
        if k + 1 < n_chunks:
            loop_rows(wait_loads, *bounds[k + 1])
    loop_rows(start_store, *bounds[-1])
    loop_rows(wait_store, 0, rows)


def _experts(idx, g, gate, w1, w3, w2, l, xh, B, n, cap, bb):
    _, E, D, FF = w1.shape
    rows = bb * cap
    gate_rows = bb if gate.shape[0] > 1 else 1
    gate_spec = (pl.BlockSpec((bb, 1, D), lambda e, b, idx: (b, 0, 0)) if gate.shape[0] > 1
                 else pl.BlockSpec((1, 1, D), lambda e, b, idx: (0, 0, 0)))
    grid_spec = pltpu.PrefetchScalarGridSpec(
        num_scalar_prefetch=1,
        grid=(E, B // bb),
        in_specs=[pl.BlockSpec((bb, 1, cap, LANES), lambda e, b, idx: (b, e, 0, 0)),
                  gate_spec,
                  pl.BlockSpec((1, 1, D, FF), lambda e, b, idx: (l, e, 0, 0)),
                  pl.BlockSpec((1, 1, D, FF), lambda e, b, idx: (l, e, 0, 0)),
                  pl.BlockSpec((1, 1, FF, D), lambda e, b, idx: (l, e, 0, 0)),
                  pl.BlockSpec(memory_space=pl.ANY)],
        out_specs=pl.BlockSpec(memory_space=pl.ANY),
        scratch_shapes=[pltpu.VMEM((rows // SUBLANES, SUBLANES, xh.shape[1]), F32),
                        pltpu.SemaphoreType.DMA((2,))])
    return pl.pallas_call(
        functools.partial(_expert_kernel, bb=bb, cap=cap, n=n, D=D, gate_rows=gate_rows,
                          n_chunks=2 if rows >= 2 * MIN_CHUNK_ROWS else 1),
        grid_spec=grid_spec,
        out_shape=jax.ShapeDtypeStruct(xh.shape, F32),
        input_output_aliases={6: 0},
        compiler_params=_cparams("arbitrary", "arbitrary"), name="experts",
    )(idx, g, gate, w1, w3, w2, xh)


def _moe(xh, aff, gate, w1, w3, w2, l, B, n, bb):
    cap = max(1, CAP_FACTOR * n // N_EXPERTS)
    idx, g = _select(aff, B, n, cap)
    return _experts(idx[..., 0].reshape(-1), g, gate, w1, w3, w2, l, xh, B, n, cap, bb)


def _rope_tables(S):
    half, quarter = HEAD_DIM // 2, HEAD_DIM // 4
    t = jnp.arange(S)
    freqs = 1.0 / (ROPE_BASE ** (jnp.arange(quarter, dtype=F32) / quarter))
    d = np.arange(LANES) % HEAD_DIM
    pos = jnp.where(jnp.asarray(d < half)[None, :], (t // GRID_W)[:, None], (t % GRID_W)[:, None]).astype(F32)
    ang = pos * freqs[jnp.asarray(d % quarter)][None, :]
    first = jnp.asarray((d % half) < quarter)[None, :]
    sin = jnp.sin(ang)
    return jnp.cos(ang), jnp.where(first, -sin, 0.0), jnp.where(first, 0.0, sin)


def _bias_pair_blocks(rpb):
    cq = np.arange(GRID_W)
    ck = np.arange(GRID_W)
    cstart = np.clip(cq - WIN_COLS // 2, 0, GRID_W - WIN_COLS)
    valid = (ck[None, :] >= cstart[:, None]) & (ck[None, :] < cstart[:, None] + WIN_COLS)
    col_off = np.clip(ck[None, :] - cq[:, None] + WIN_COLS - 1, 0, 2 * WIN_COLS - 2)
    place = (col_off[None] == np.arange(2 * WIN_COLS - 1)[:, None, None]).astype(np.float32)
    ro = np.clip(np.arange(N_PAIR_BLOCKS + 1) - 8, 0, 2 * WIN_ROWS - 2)
    T = jnp.einsum('hrd,dck->hrck', rpb[:, ro], jnp.asarray(place), precision=lax.Precision.HIGHEST)
    T = jnp.where(jnp.asarray(valid)[None, None], T * LOG2E, NEG_INF)
    return jnp.concatenate([T[:, :-1], T[:, 1:]], axis=-1)


def _head_sum_tables():
    lane = np.arange(ATTN_WIDTH)
    bd = (lane[:, None] // HEAD_DIM == np.arange(LANES)[None, :]).astype(np.float32)
    return jnp.asarray(bd, BF16), jnp.asarray(bd.T, BF16)


def kernel(x, c, ctx, c_ctx, w_ada, b_ada, g_norm1, g_norm2, w_in, b_in, g_q, g_k, rpb, w_attn_o, conv_dw_w,
           conv_dw_b, conv_ln_g, conv_ln_b, w_conv_o, sc_w, w_sc_o, w_o, w_router, w_e1, w_e3, w_e2):
    B, S, D = x.shape
    C = ctx.shape[1]
    L = w_ada.shape[0]
    aw3 = 3 * ATTN_WIDTH
    prw = 2 * CONV_WIDTH + 3 * SC_WIDTH + 3 * D

    cc = jnp.zeros((8, D), F32).at[0:B].set(c).at[B].set(c_ctx)
    mod = _ada(cc, w_ada, b_ada)
    rope_tabs = _rope_tables(S)
    bd, ex = _head_sum_tables()

    xl = x.reshape(B * S, D)
    xc = ctx.reshape(B * C, D)
    b_in3 = b_in.reshape(L, 1, -1)
    w_qkv = w_in[:, :, :aw3].astype(BF16)
    wa, wc, ws, wo = (w.astype(BF16) for w in (w_attn_o, w_conv_o, w_sc_o, w_o))
    we = (w_e1.astype(BF16), w_e3.astype(BF16), w_e2.astype(BF16))

    for l in range(L):
        last = l == L - 1
        ml = lambda j: mod[l, 0:B, j * D:(j + 1) * D].reshape(B, 1, D)
        mc = lambda j: mod[l, B:B + 1, j * D:(j + 1) * D].reshape(1, 1, D)
        g1 = g_norm1[l].reshape(1, D)
        g2 = g_norm2[l].reshape(1, D)
        b_qkv = b_in[l][:aw3].reshape(1, aw3)
        gq = jnp.tile(g_q[l], N_HEADS).reshape(1, ATTN_WIDTH)
        gk = jnp.tile(g_k[l], N_HEADS).reshape(1, ATTN_WIDTH)
        branch_w = (wa, wc, ws, l,
                    jnp.zeros((32, CONV_WIDTH), F32).at[:CONV_K].set(conv_dw_w[l]),
                    conv_dw_b[l].reshape(1, -1), conv_ln_g[l].reshape(1, -1), conv_ln_b[l].reshape(1, -1),
                    jnp.zeros((8, SC_WIDTH), F32).at[:SC_K].set(sc_w[l]))
        wr = jnp.zeros((D, LANES), F32).at[:, :N_EXPERTS].set(w_router[l]).astype(BF16)

        hc = _normmod(xc, g1, mc(0), mc(1), C)
        qc, kc, vc = _qkv(hc, w_qkv, l, b_qkv, gq, gk, bd, ex, None, C)
        if not last:
            attn_c = _ctx_attn(qc, kc, vc, B, C)
            pr_c = _proj(hc, w_in, b_in3, l, aw3, prw)
            merged_c = _branch(attn_c, pr_c, *branch_w, C, D)
            xh_c, aff_c = _outproj(merged_c, xc, wo, l, mc(2), g2, mc(3), mc(4), wr, C)
            xc = _moe(xh_c, aff_c, mc(5), *we, l, B, C, B)

        h = _normmod(xl, g1, ml(0), ml(1), S)
        qn, kn, v, qr, kr = _qkv(h, w_qkv, l, b_qkv, gq, gk, bd, ex, rope_tabs, S)
        attn = _attn(qn, qr, kr, v, kc, vc, _bias_pair_blocks(rpb[l]), B, S, C)
        pr = _proj(h, w_in, b_in3, l, aw3, prw)
        merged = _branch(attn, pr, *branch_w, S, D)
        xh, aff = _outproj(merged, xl, wo, l, ml(2), g2, ml(3), ml(4), wr, S)
        xl = _moe(xh, aff, ml(5), *we, l, B, S, 1)

    return xl[:, :D].reshape(B, S, D)
```

```python
import functools

import numpy as np
import jax
import jax.numpy as jnp
from jax import lax
from jax.experimental import pallas as pl
from jax.experimental.pallas import tpu as pltpu

F32 = jnp.float32
BF16 = jnp.bfloat16

GRID_W = 64
N_HEADS = 16
HEAD_DIM = 64
ATTN_WIDTH = N_HEADS * HEAD_DIM
N_HEAD_PAIRS = N_HEADS // 2
WIN_ROWS = 8
WIN_COLS = 16
ROPE_BASE = 10000.0
CONV_WIDTH = 512
CONV_K = 31
SC_WIDTH = 512
SC_K = 3
N_EXPERTS = 16
CAP_FACTOR = 2
EPS = 1e-6
NEG_INF = -1e30

LANES = 128
SUBLANES = 8
LOG2E = 1.4426950408889634
Q_SCALE = HEAD_DIM ** -0.5 * LOG2E
HALO = 16
ROW_GROUP = 4
KEY_ROWS = ROW_GROUP + WIN_ROWS
BIAS_PAD = WIN_ROWS
N_PAIR_BLOCKS = WIN_ROWS - 1 + BIAS_PAD + KEY_ROWS - 1
VMEM_LIMIT = 56 * 1024 * 1024


def _cparams(*sem):
    return pltpu.CompilerParams(dimension_semantics=sem, vmem_limit_bytes=VMEM_LIMIT)


def _dot(a, b):
    return jnp.dot(a, b, preferred_element_type=F32)


def _dot_t(a, b):
    return lax.dot_general(a, b, (((1,), (1,)), ((), ())), preferred_element_type=F32)


def _sigmoid(x):
    return 0.5 * jnp.tanh(0.5 * x) + 0.5


def _silu(x):
    return x * _sigmoid(x)


def _ada_kernel(cc_ref, w_ref, b_ref, o_ref):
    s = _silu(cc_ref[...]).astype(BF16)
    o_ref[0] = _dot(s, w_ref[0].astype(BF16)) + b_ref[0]


def _ada(cc, w_ada, b_ada):
    L, D, W = w_ada.shape
    tn = 1024
    return pl.pallas_call(
        _ada_kernel,
        grid=(L, W // tn),
        in_specs=[pl.BlockSpec((8, D), lambda l, j: (0, 0)),
                  pl.BlockSpec((1, D, tn), lambda l, j: (l, 0, j)),
                  pl.BlockSpec((1, 1, tn), lambda l, j: (l, 0, j))],
        out_specs=pl.BlockSpec((1, 8, tn), lambda l, j: (l, 0, j)),
        out_shape=jax.ShapeDtypeStruct((L, 8, W), F32),
        compiler_params=_cparams("arbitrary", "arbitrary"),
        name="ada",
    )(cc, w_ada, b_ada.reshape(L, 1, W))


def _mod_spec(nmod, tiles_per_mod, D):
    if nmod == 1:
        return pl.BlockSpec((1, 1, D), lambda i: (0, 0, 0))
    return pl.BlockSpec((1, 1, D), lambda i: (i // tiles_per_mod, 0, 0))


def _normmod_kernel(x_ref, g_ref, sh_ref, sc_ref, h_ref):
    x = x_ref[...]
    ms = jnp.mean(x * x, axis=-1, keepdims=True)
    y = x * lax.rsqrt(ms + EPS) * g_ref[...]
    h_ref[...] = (y * (1.0 + sc_ref[0]) + sh_ref[0]).astype(BF16)


def _normmod(x, g, shift, scale, seq_len):
    rows, D = x.shape[0], g.shape[1]
    tm = min(1024, seq_len)
    mspec = _mod_spec(shift.shape[0], seq_len // tm, D)
    xspec = pl.BlockSpec((tm, D), lambda i: (i, 0))
    return pl.pallas_call(
        _normmod_kernel, grid=(rows // tm,),
        in_specs=[xspec, pl.BlockSpec((1, D), lambda i: (0, 0)), mspec, mspec],
        out_specs=xspec, out_shape=jax.ShapeDtypeStruct((rows, D), BF16),
        compiler_params=_cparams("arbitrary"), name="normmod",
    )(x, g, shift, scale)


def _qkv_kernel(*refs, rope):
    if rope:
        (h_ref, w_ref, b_ref, gq_ref, gk_ref, bd_ref, ex_ref, cos_ref, sa_ref, sb_ref,
         qn_ref, kn_ref, v_ref, qr_ref, kr_ref) = refs
    else:
        h_ref, w_ref, b_ref, gq_ref, gk_ref, bd_ref, ex_ref, qn_ref, kn_ref, v_ref = refs
    h = h_ref[...]
    aw = ATTN_WIDTH

    def proj(s):
        return _dot(h, w_ref[0, :, s * aw:(s + 1) * aw]) + b_ref[:, s * aw:(s + 1) * aw]

    def headnorm(a, g_ref):
        ss = _dot((a * a).astype(BF16), bd_ref[...])
        inv = lax.rsqrt(ss * (1.0 / HEAD_DIM) + EPS)
        inv_hi = inv.astype(BF16)
        inv_lo = (inv - inv_hi.astype(F32)).astype(BF16)
        full = _dot(inv_hi, ex_ref[...]) + _dot(inv_lo, ex_ref[...])
        return a * full * g_ref[...]

    def store_plain(xn, out_ref):
        for c in range(N_HEAD_PAIRS):
            out_ref[c] = xn[:, c * LANES:(c + 1) * LANES].astype(BF16)

    def store_rope(xn, out_ref):
        cos, sa, sb = cos_ref[...], sa_ref[...], sb_ref[...]
        for c in range(N_HEAD_PAIRS):
            xc = xn[:, c * LANES:(c + 1) * LANES]
            r = xc * cos + pltpu.roll(xc, LANES - 16, 1) * sa + pltpu.roll(xc, 16, 1) * sb
            out_ref[c] = r.astype(BF16)

    qn = headnorm(proj(0), gq_ref) * Q_SCALE
    store_plain(qn, qn_ref)
    if rope:
        store_rope(qn, qr_ref)
    kn = headnorm(proj(1), gk_ref)
    store_plain(kn, kn_ref)
    if rope:
        store_rope(kn, kr_ref)
    store_plain(proj(2), v_ref)


def _qkv(h, w, l, b, gq, gk, bd, ex, rope_tabs, seq_len):
    rows, D = h.shape
    tm = min(512, seq_len)
    rope = rope_tabs is not None
    const = lambda i: (0, 0)
    in_specs = [pl.BlockSpec((tm, D), lambda i: (i, 0)),
                pl.BlockSpec((1, D, 3 * ATTN_WIDTH), lambda i: (l, 0, 0), pipeline_mode=pl.Buffered(1)),
                pl.BlockSpec((1, 3 * ATTN_WIDTH), const),
                pl.BlockSpec((1, ATTN_WIDTH), const),
                pl.BlockSpec((1, ATTN_WIDTH), const),
                pl.BlockSpec((ATTN_WIDTH, LANES), const),
                pl.BlockSpec((LANES, ATTN_WIDTH), const)]
    ins = [h, w, b, gq, gk, bd, ex]
    n_out = 3
    if rope:
        tiles_per_seq = seq_len // tm
        tspec = pl.BlockSpec((tm, LANES), lambda i: (i % tiles_per_seq, 0))
        in_specs += [tspec, tspec, tspec]
        ins += list(rope_tabs)
        n_out = 5
    ospec = pl.BlockSpec((N_HEAD_PAIRS, tm, LANES), lambda i: (0, i, 0))
    return pl.pallas_call(
        functools.partial(_qkv_kernel, rope=rope),
        grid=(rows // tm,), in_specs=in_specs,
        out_specs=[ospec] * n_out,
        out_shape=[jax.ShapeDtypeStruct((N_HEAD_PAIRS, rows, LANES), BF16)] * n_out,
        compiler_params=_cparams("arbitrary"), name="qkv",
    )(*ins)


def _proj_kernel(h_ref, w_ref, b_ref, o_ref, wb_ref):
    @pl.when(pl.program_id(1) == 0)
    def _():
        wb_ref[...] = w_ref[0].astype(BF16)

    o_ref[...] = (_dot(h_ref[...], wb_ref[...]) + b_ref[0]).astype(o_ref.dtype)


def _proj(h, w, b, l, col0, ncols):
    rows, D = h.shape
    tm = min(2048, rows)
    tn = 512
    j0 = col0 // tn
    return pl.pallas_call(
        _proj_kernel, grid=(ncols // tn, rows // tm),
        in_specs=[pl.BlockSpec((tm, D), lambda j, i: (i, 0)),
                  pl.BlockSpec((1, D, tn), lambda j, i: (l, 0, j + j0)),
                  pl.BlockSpec((1, 1, tn), lambda j, i: (l, 0, j + j0))],
        out_specs=pl.BlockSpec((tm, tn), lambda j, i: (i, j)),
        out_shape=jax.ShapeDtypeStruct((rows, ncols), BF16),
        scratch_shapes=[pltpu.VMEM((D, tn), BF16)],
        compiler_params=_cparams("arbitrary", "arbitrary"), name="proj",
    )(h, w, b)


def _attn_kernel(qn_ref, qr_ref, kr_ref, v_ref, kc_ref, vc_ref, fp_ref, o_ref, *, rows):
    gq = ROW_GROUP * GRID_W
    nk = KEY_ROWS * GRID_W
    lane = lax.broadcasted_iota(jnp.int32, (1, LANES), 1)
    head_mask = (lane < HEAD_DIM, lane >= HEAD_DIM)
    key_row = lax.broadcasted_iota(jnp.int32, (1, nk), 1) // GRID_W
    kc = kc_ref[0]
    vc = [jnp.where(head_mask[hh], vc_ref[0], 1) for hh in range(2)]

    def group(gi, carry):
        r0 = gi * ROW_GROUP
        ws = jnp.clip(r0 - WIN_ROWS // 2, 0, rows - KEY_ROWS)
        tok0 = pl.multiple_of(gi * gq, gq)
        key0 = pl.multiple_of(ws * GRID_W, GRID_W)
        qr = qr_ref[0, pl.ds(tok0, gq), :]
        qn = qn_ref[0, pl.ds(tok0, gq), :]
        kw = kr_ref[0, pl.ds(key0, nk), :]
        vw = v_ref[0, pl.ds(key0, nk), :]
        scores = [(_dot_t(jnp.where(head_mask[hh], qr, 0), kw),
                   _dot_t(jnp.where(head_mask[hh], qn, 0), kc))
                  for hh in range(2)]
        outs = []
        for hh in range(2):
            s_loc, s_ctx = scores[hh]
            slabs = []
            for i in range(ROW_GROUP):
                n0 = ws - r0 - i + (WIN_ROWS - 1) + BIAS_PAD
                bias = jnp.concatenate([fp_ref[hh, n0 + 2 * jp] for jp in range(KEY_ROWS // 2)], axis=1)
                lo = jnp.clip(r0 + i - WIN_ROWS // 2, 0, rows - WIN_ROWS) - ws
                valid = (key_row >= lo) & (key_row < lo + WIN_ROWS)
                slabs.append(jnp.where(valid, s_loc[i * GRID_W:(i + 1) * GRID_W] + bias, NEG_INF))
            s_loc = jnp.concatenate(slabs, axis=0)
            m = jnp.maximum(jnp.max(s_loc, axis=-1, keepdims=True), jnp.max(s_ctx, axis=-1, keepdims=True))
            p_loc = jnp.exp2(s_loc - m).astype(BF16)
            p_ctx = jnp.exp2(s_ctx - m).astype(BF16)
            o = _dot(p_loc, jnp.where(head_mask[hh], vw, 1)) + _dot(p_ctx, vc[hh])
            outs.append(o / pltpu.roll(o, HEAD_DIM, 1))
        o_ref[0, pl.ds(tok0, gq), :] = jnp.where(head_mask[0], outs[0], outs[1]).astype(BF16)
        return carry

    lax.fori_loop(0, rows // ROW_GROUP, group, 0, unroll=2)


def _attn(qn, qr, kr, v, kc, vc, fp, B, S, C):
    rows = S // GRID_W
    tok = pl.BlockSpec((1, S, LANES), lambda b, p: (p, b, 0))
    ctx = pl.BlockSpec((1, C, LANES), lambda b, p: (p, b, 0))
    return pl.pallas_call(
        functools.partial(_attn_kernel, rows=rows),
        grid=(B, N_HEAD_PAIRS),
        in_specs=[tok, tok, tok, tok, ctx, ctx,
                  pl.BlockSpec((2, N_PAIR_BLOCKS, GRID_W, LANES), lambda b, p: (p, 0, 0, 0))],
        out_specs=tok,
        out_shape=jax.ShapeDtypeStruct((N_HEAD_PAIRS, B * S, LANES), BF16),
        compiler_params=_cparams("arbitrary", "arbitrary"), name="attn",
    )(qn, qr, kr, v, kc, vc, fp)


def _ctx_attn_kernel(q_ref, k_ref, v_ref, o_ref):
    lane = lax.broadcasted_iota(jnp.int32, (1, LANES), 1)
    head_mask = (lane < HEAD_DIM, lane >= HEAD_DIM)
    q, k, v = q_ref[0], k_ref[0], v_ref[0]
    outs = []
    for hh in range(2):
        s = _dot_t(jnp.where(head_mask[hh], q, 0), k)
        m = jnp.max(s, axis=-1, keepdims=True)
        p = jnp.exp2(s - m)
        outs.append(_dot(p.astype(BF16), v) / jnp.sum(p, axis=-1, keepdims=True))
    o_ref[0] = jnp.where(head_mask[0], outs[0], outs[1]).astype(BF16)


def _ctx_attn(q, k, v, B, C):
    spec = pl.BlockSpec((1, C, LANES), lambda b, p: (p, b, 0))
    return pl.pallas_call(
        _ctx_attn_kernel, grid=(B, N_HEAD_PAIRS),
        in_specs=[spec, spec, spec], out_specs=spec,
        out_shape=jax.ShapeDtypeStruct((N_HEAD_PAIRS, B * C, LANES), BF16),
        compiler_params=_cparams("arbitrary", "arbitrary"), name="ctx_attn",
    )(q, k, v)


def _branch_kernel(attn_ref, pr_ref, pcp_ref, pcn_ref, wa_ref, wc_ref, ws_ref,
                   dww_ref, dwb_ref, lng_ref, lnb_ref, scw_ref, o_ref, glu_ext, sc_ext, *, tm, seq_len, D):
    i = pl.program_id(0)
    pos0 = (i * tm) % seq_len
    keep_prev = (pos0 != 0).astype(F32)
    keep_next = (pos0 + tm != seq_len).astype(F32)
    cw, sw = CONV_WIDTH, SC_WIDTH

    def glu(blk):
        return blk[:, 0:cw].astype(F32) * _sigmoid(blk[:, cw:2 * cw].astype(F32))

    def gated_x(blk):
        o = 2 * cw + sw
        return blk[:, o:o + sw].astype(F32) * blk[:, o + sw:o + 2 * sw].astype(F32)

    pcw = 2 * cw + 3 * sw
    cur, prv, nxt = pr_ref[:, 0:pcw], pcp_ref[...], pcn_ref[...]
    glu_ext[0:HALO, :] = glu(prv) * keep_prev
    glu_ext[HALO:HALO + tm, :] = glu(cur)
    glu_ext[HALO + tm:2 * HALO + tm, :] = glu(nxt) * keep_next
    sc_ext[0:HALO, :] = gated_x(prv) * keep_prev
    sc_ext[HALO:HALO + tm, :] = gated_x(cur)
    sc_ext[HALO + tm:2 * HALO + tm, :] = gated_x(nxt) * keep_next

    chunk = 32
    window = chunk + 2 * HALO
    conv_rows = []
    for r in range(0, tm, chunk):
        win = glu_ext[r:r + window, :]
        acc = jnp.zeros((chunk, cw), F32)
        for res in range(SUBLANES):
            shifted = win if res == 0 else pltpu.roll(win, window - res, 0)
            for k in range(CONV_K):
                off = HALO - CONV_K // 2 + k
                if off % SUBLANES == res:
                    acc = acc + dww_ref[k:k + 1, :] * shifted[off - res:off - res + chunk, :]
        conv_rows.append(acc)
    u = jnp.concatenate(conv_rows, axis=0) + dwb_ref[...]
    mu = jnp.mean(u, axis=-1, keepdims=True)
    uc = u - mu
    var = jnp.mean(uc * uc, axis=-1, keepdims=True)
    u = _silu(uc * lax.rsqrt(var + EPS) * lng_ref[...] + lnb_ref[...])
    y_b = _dot(u.astype(BF16), wc_ref[0])

    c3 = jnp.zeros((tm, sw), F32)
    for k in range(SC_K):
        base = HALO - SC_K // 2 + k
        c3 = c3 + scw_ref[k:k + 1, :] * sc_ext[base:base + tm, :]
    sc_b = cur[:, 2 * cw:2 * cw + sw].astype(F32)
    y_c = _dot((sc_b * c3).astype(BF16), ws_ref[0])

    attn = jnp.concatenate([attn_ref[c] for c in range(N_HEAD_PAIRS)], axis=1)
    y_a = _dot(attn, wa_ref[0])

    g = pr_ref[:, pcw:pcw + 3 * D]
    merged = (_sigmoid(g[:, 0:D].astype(F32)) * y_a
              + _sigmoid(g[:, D:2 * D].astype(F32)) * y_b
              + _sigmoid(g[:, 2 * D:3 * D].astype(F32)) * y_c)
    o_ref[...] = merged.astype(BF16)


def _branch(attn, pr, wa, wc, ws, l, dww, dwb, lng, lnb, scw, seq_len, D):
    rows, prw = pr.shape
    tm = 256
    pcw = 2 * CONV_WIDTH + 3 * SC_WIDTH
    nhalo = rows // HALO
    const = lambda i: (0, 0)
    layer = lambda i: (l, 0, 0)
    return pl.pallas_call(
        functools.partial(_branch_kernel, tm=tm, seq_len=seq_len, D=D),
        grid=(rows // tm,),
        in_specs=[pl.BlockSpec((N_HEAD_PAIRS, tm, LANES), lambda i: (0, i, 0)),
                  pl.BlockSpec((tm, prw), lambda i: (i, 0)),
                  pl.BlockSpec((HALO, pcw), lambda i: (jnp.maximum(i * (tm // HALO) - 1, 0), 0)),
                  pl.BlockSpec((HALO, pcw), lambda i: (jnp.minimum((i + 1) * (tm // HALO), nhalo - 1), 0)),
                  pl.BlockSpec((1, ATTN_WIDTH, D), layer),
                  pl.BlockSpec((1, CONV_WIDTH, D), layer),
                  pl.BlockSpec((1, SC_WIDTH, D), layer),
                  pl.BlockSpec((32, CONV_WIDTH), const),
                  pl.BlockSpec((1, CONV_WIDTH), const),
                  pl.BlockSpec((1, CONV_WIDTH), const),
                  pl.BlockSpec((1, CONV_WIDTH), const),
                  pl.BlockSpec((8, SC_WIDTH), const)],
        out_specs=pl.BlockSpec((tm, D), lambda i: (i, 0)),
        out_shape=jax.ShapeDtypeStruct((rows, D), BF16),
        scratch_shapes=[pltpu.VMEM((tm + 2 * HALO, CONV_WIDTH), F32),
                        pltpu.VMEM((tm + 2 * HALO, SC_WIDTH), F32)],
        compiler_params=_cparams("arbitrary"), name="branch",
    )(attn, pr, pr, pr, wa, wc, ws, dww, dwb, lng, lnb, scw)


def _pack_halves(x):
    half = x.shape[1] // 2
    bits = lax.bitcast_convert_type(x.astype(F32), jnp.uint32)
    return (bits[:, :half] >> 16) | (bits[:, half:] & jnp.uint32(0xFFFF0000))


def _unpack_halves(words):
    lo = lax.bitcast_convert_type(words << 16, F32).astype(BF16)
    hi = lax.bitcast_convert_type(words & jnp.uint32(0xFFFF0000), F32).astype(BF16)
    return jnp.concatenate([lo, hi], axis=1)


def _outproj_kernel(m_ref, x_ref, wo_ref, gate_ref, g2_ref, sh_ref, sc_ref, wr_ref, xh_ref, aff_ref):
    D = x_ref.shape[1]
    x = x_ref[...] + gate_ref[0] * _dot(m_ref[...], wo_ref[0])
    xh_ref[:, 0:D] = x
    ms = jnp.mean(x * x, axis=-1, keepdims=True)
    y = x * lax.rsqrt(ms + EPS) * g2_ref[...]
    h2 = (y * (1.0 + sc_ref[0]) + sh_ref[0]).astype(BF16)
    xh_ref[:, D:] = lax.bitcast_convert_type(_pack_halves(h2), F32)
    logits = _dot(h2, wr_ref[...])
    lane = lax.broadcasted_iota(jnp.int32, logits.shape, 1)
    logits = jnp.where(lane < N_EXPERTS, logits, NEG_INF)
    e = jnp.exp(logits - jnp.max(logits, axis=-1, keepdims=True))
    aff_ref[...] = e / jnp.sum(e, axis=-1, keepdims=True)


def _outproj(merged, x, wo, l, gate, g2, shift, scale, wr, seq_len):
    rows, D = merged.shape
    tm = 256
    mspec = _mod_spec(gate.shape[0], seq_len // tm, D)
    row = lambda i: (i, 0)
    const = lambda i: (0, 0)
    return pl.pallas_call(
        _outproj_kernel, grid=(rows // tm,),
        in_specs=[pl.BlockSpec((tm, D), row), pl.BlockSpec((tm, D), row),
                  pl.BlockSpec((1, D, D), lambda i: (l, 0, 0)), mspec,
                  pl.BlockSpec((1, D), const), mspec, mspec,
                  pl.BlockSpec((D, LANES), const)],
        out_specs=[pl.BlockSpec((tm, D + D // 2), row), pl.BlockSpec((tm, LANES), row)],
        out_shape=[jax.ShapeDtypeStruct((rows, D + D // 2), F32),
                   jax.ShapeDtypeStruct((rows, LANES), F32)],
        compiler_params=_cparams("arbitrary"), name="outproj",
    )(merged, x, wo, gate, g2, shift, scale, wr)


MAX_BISECT = 160
BISECT_PER_CHECK = 4
POS_LANE = 3 * N_EXPERTS
TOK_SPLIT = 64


def _select_kernel(aff_ref, tri_ref, tv_ref, idx_ref, g_ref, s1_ref, r_ref, *, n, cap, blk, tchunk):
    E = N_EXPERTS
    aff = aff_ref[0]
    a = aff.T[0:E, :]
    capf = float(cap)

    def cond(st):
        it, _, _, ndone = st
        return jnp.logical_and(it < MAX_BISECT, ndone < float(E))

    def in_range(lo, hi):
        return jnp.logical_and(a >= lo, a < hi)

    def body(st):
        it, lo, hi, _ = st
        for _ in range(BISECT_PER_CHECK):
            mid = 0.5 * (lo + hi)
            cnt = jnp.sum((a >= mid).astype(F32), axis=-1, keepdims=True)
            ge = cnt >= capf
            lo = jnp.where(ge, mid, lo)
            hi = jnp.where(ge, hi, mid)
        r = in_range(lo, hi)
        vmin = jnp.min(jnp.where(r, a, 4.0), axis=-1, keepdims=True)
        vmax = jnp.max(jnp.where(r, a, -1.0), axis=-1, keepdims=True)
        ndone = jnp.sum((vmin == vmax).astype(F32))
        return it + BISECT_PER_CHECK, lo, hi, ndone

    _, lo, hi, _ = lax.while_loop(cond, body, (jnp.int32(0), jnp.zeros((E, 1), F32), jnp.full((E, 1), 2.0, F32),
                                               jnp.float32(0.0)))
    thr = jnp.max(jnp.where(in_range(lo, hi), a, -1.0), axis=-1, keepdims=True)

    def cumsum(x):
        outs, carry = [], jnp.zeros((E, 1), F32)
        for c in range(n // blk):
            part = _dot(x[:, c * blk:(c + 1) * blk].astype(BF16), tri_ref[...]) + carry
            outs.append(part)
            carry = part[:, blk - 1:blk]
        return jnp.concatenate(outs, axis=1)

    gt = a > thr
    eq = (a == thr).astype(F32)
    need = capf - jnp.sum(gt.astype(F32), axis=-1, keepdims=True)
    eq_rank = cumsum(eq) - eq
    sel = jnp.logical_or(gt, jnp.logical_and(eq > 0.0, eq_rank < need)).astype(F32)
    s1_ref[...] = sel * cumsum(sel)

    a_hi = aff.astype(BF16).astype(F32)
    rest = aff - a_hi
    a_mid = rest.astype(BF16).astype(F32)
    a_lo = rest - a_mid
    r_ref[...] = (a_hi + pltpu.roll(a_mid, E, 1) + pltpu.roll(a_lo, 2 * E, 1) + tv_ref[...]).astype(BF16)

    slot_ids = lax.broadcasted_iota(jnp.int32, (cap, 1), 0).astype(F32) + 1.0
    lane = lax.broadcasted_iota(jnp.int32, (1, LANES), 1)

    def per_expert(e, carry):
        row = s1_ref[pl.ds(e, 1), :]
        acc = jnp.zeros((cap, LANES), F32)
        for c in range(n // tchunk):
            onehot = (row[:, c * tchunk:(c + 1) * tchunk] == slot_ids).astype(BF16)
            acc = acc + _dot(onehot, r_ref[c * tchunk:(c + 1) * tchunk, :])
        mine = jnp.logical_and(lane % E == e, lane < POS_LANE)
        g = jnp.sum(jnp.where(mine, acc, 0.0), axis=1, keepdims=True)
        tok = jnp.sum(jnp.where(lane == POS_LANE, acc * float(TOK_SPLIT), jnp.where(lane == POS_LANE + 1, acc, 0.0)),
                      axis=1, keepdims=True)
        g_ref[0, e] = jnp.broadcast_to(g, (cap, LANES))
        idx_ref[0, e] = jnp.broadcast_to(tok, (cap, LANES)).astype(jnp.int32)
        return carry

    lax.fori_loop(0, E, per_expert, 0)


def _select(aff, B, n, cap):
    blk = min(512, n)
    tchunk = min(1024, n)
    tri = (np.arange(blk)[:, None] <= np.arange(blk)[None, :]).astype(np.float32)
    tv = np.zeros((n, LANES), np.float32)
    tv[:, POS_LANE] = np.arange(n) // TOK_SPLIT
    tv[:, POS_LANE + 1] = np.arange(n) % TOK_SPLIT
    out = pl.BlockSpec((1, N_EXPERTS, cap, LANES), lambda b: (b, 0, 0, 0))
    return pl.pallas_call(
        functools.partial(_select_kernel, n=n, cap=cap, blk=blk, tchunk=tchunk),
        grid=(B,),
        in_specs=[pl.BlockSpec((1, n, LANES), lambda b: (b, 0, 0)),
                  pl.BlockSpec((blk, blk), lambda b: (0, 0)),
                  pl.BlockSpec((n, LANES), lambda b: (0, 0))],
        out_specs=[out, out],
        out_shape=[jax.ShapeDtypeStruct((B, N_EXPERTS, cap, LANES), jnp.int32),
                   jax.ShapeDtypeStruct((B, N_EXPERTS, cap, LANES), F32)],
        scratch_shapes=[pltpu.VMEM((N_EXPERTS, n), F32), pltpu.VMEM((n, LANES), BF16)],
        compiler_params=_cparams("arbitrary"), name="select",
    )(aff.reshape(B, n, LANES), jnp.asarray(tri, BF16), jnp.asarray(tv))


MIN_CHUNK_ROWS = 256


def _expert_kernel(idx_ref, g_ref, gate_ref, w1_ref, w3_ref, w2_ref, xh_in, xh_hbm,
                   buf, sem, *, bb, cap, n, D, gate_rows, n_chunks):
    del xh_in
    e = pl.program_id(0)
    b0 = pl.program_id(1) * bb
    rows = bb * cap

    def token_row(j):
        b = b0 + j // cap
        return b * n + idx_ref[(b * N_EXPERTS + e) * cap + j % cap]

    def loop_rows(fn, lo, hi):
        for bl in range(bb):
            seg_lo, seg_hi = max(lo, bl * cap), min(hi, (bl + 1) * cap)
            if seg_lo >= seg_hi:
                continue
            b = b0 + bl
            base = (b * N_EXPERTS + e) * cap - bl * cap

            def body(t, carry, b=b, base=base):
                for u in range(SUBLANES):
                    fn(t, u, b * n + idx_ref[base + t * SUBLANES + u])
                return carry

            lax.fori_loop(seg_lo // SUBLANES, seg_hi // SUBLANES, body, 0)

    def inline_rows(fn, lo, hi):
        for j in range(lo, hi):
            fn(j // SUBLANES, j % SUBLANES, token_row(j))

    def row_load(t, u, r):
        return pltpu.make_async_copy(xh_hbm.at[pl.ds(r, 1), :], buf.at[t, pl.ds(u, 1), :], sem.at[0])

    def row_store(t, u, r):
        return pltpu.make_async_copy(buf.at[t, pl.ds(u, 1), pl.ds(0, D)], xh_hbm.at[pl.ds(r, 1), pl.ds(0, D)],
                                     sem.at[1])

    def start_loads(t, u, r):
        row_load(t, u, r).start()

    def wait_loads(t, u, r):
        row_load(t, u, r).wait()

    def start_store(t, u, r):
        row_store(t, u, r).start()

    def wait_store(t, u, r):
        row_store(t, u, r).wait()

    g = g_ref[...].reshape(rows, LANES)

    def compute(lo, hi):
        ts = slice(lo // SUBLANES, hi // SUBLANES)
        x = _unpack_halves(lax.bitcast_convert_type(buf[ts, :, D:], jnp.uint32).reshape(hi - lo, D // 2))
        hidden = (_silu(_dot(x, w1_ref[0, 0])) * _dot(x, w3_ref[0, 0])).astype(BF16)
        y = _dot(hidden, w2_ref[0, 0]) * jnp.concatenate([g[lo:hi]] * (D // LANES), axis=1)
        for bl in range(bb):
            seg_lo, seg_hi = max(lo, bl * cap), min(hi, (bl + 1) * cap)
            if seg_lo >= seg_hi:
                continue
            gate_row = gate_ref[bl if gate_rows > 1 else 0]
            seg = slice(seg_lo // SUBLANES, seg_hi // SUBLANES)
            upd = y[seg_lo - lo:seg_hi - lo] * gate_row
            buf[seg, :, 0:D] += upd.reshape((seg_hi - seg_lo) // SUBLANES, SUBLANES, D)

    bounds = [(k * rows // n_chunks, (k + 1) * rows // n_chunks) for k in range(n_chunks)]
    loop_rows(start_loads, *bounds[0])
    loop_rows(wait_loads, *bounds[0])
    for k in range(n_chunks):
        if k + 1 < n_chunks:
            inline_rows(start_loads, *bounds[k + 1])
        if k >= 1:
            inline_rows(start_store, *bounds[k - 1])
        compute(*bounds[k])
        if k + 1 < n_chunks:
            loop_rows(wait_loads, *bounds[k + 1])
    loop_rows(start_store, *bounds[-1])
    loop_rows(wait_store, 0, rows)


def _experts(idx, g, gate, w1, w3, w2, l, xh, B, n, cap, bb):
    _, E, D, FF = w1.shape
    rows = bb * cap
    gate_rows = bb if gate.shape[0] > 1 else 1
    gate_spec = (pl.BlockSpec((bb, 1, D), lambda e, b, idx: (b, 0, 0)) if gate.shape[0] > 1
                 else pl.BlockSpec((1, 1, D), lambda e, b, idx: (0, 0, 0)))
    grid_spec = pltpu.PrefetchScalarGridSpec(
        num_scalar_prefetch=1,
        grid=(E, B // bb),
        in_specs=[pl.BlockSpec((bb, 1, cap, LANES), lambda e, b, idx: (b, e, 0, 0)),
                  gate_spec,
                  pl.BlockSpec((1, 1, D, FF), lambda e, b, idx: (l, e, 0, 0)),
                  pl.BlockSpec((1, 1, D, FF), lambda e, b, idx: (l, e, 0, 0)),
                  pl.BlockSpec((1, 1, FF, D), lambda e, b, idx: (l, e, 0, 0)),
                  pl.BlockSpec(memory_space=pl.ANY)],
        out_specs=pl.BlockSpec(memory_space=pl.ANY),
        scratch_shapes=[pltpu.VMEM((rows // SUBLANES, SUBLANES, xh.shape[1]), F32),
                        pltpu.SemaphoreType.DMA((2,))])
    return pl.pallas_call(
        functools.partial(_expert_kernel, bb=bb, cap=cap, n=n, D=D, gate_rows=gate_rows,
                          n_chunks=2 if rows >= 2 * MIN_CHUNK_ROWS else 1),
        grid_spec=grid_spec,
        out_shape=jax.ShapeDtypeStruct(xh.shape, F32),
        input_output_aliases={6: 0},
        compiler_params=_cparams("arbitrary", "arbitrary"), name="experts",
    )(idx, g, gate, w1, w3, w2, xh)


def _moe(xh, aff, gate, w1, w3, w2, l, B, n, bb):
    cap = max(1, CAP_FACTOR * n // N_EXPERTS)
    idx, g = _select(aff, B, n, cap)
    return _experts(idx[..., 0].reshape(-1), g, gate, w1, w3, w2, l, xh, B, n, cap, bb)


def _rope_tables(S):
    half, quarter = HEAD_DIM // 2, HEAD_DIM // 4
    t = jnp.arange(S)
    freqs = 1.0 / (ROPE_BASE ** (jnp.arange(quarter, dtype=F32) / quarter))
    d = np.arange(LANES) % HEAD_DIM
    pos = jnp.where(jnp.asarray(d < half)[None, :], (t // GRID_W)[:, None], (t % GRID_W)[:, None]).astype(F32)
    ang = pos * freqs[jnp.asarray(d % quarter)][None, :]
    first = jnp.asarray((d % half) < quarter)[None, :]
    sin = jnp.sin(ang)
    return jnp.cos(ang), jnp.where(first, -sin, 0.0), jnp.where(first, 0.0, sin)


def _bias_pair_blocks(rpb):
    cq = np.arange(GRID_W)
    ck = np.arange(GRID_W)
    cstart = np.clip(cq - WIN_COLS // 2, 0, GRID_W - WIN_COLS)
    valid = (ck[None, :] >= cstart[:, None]) & (ck[None, :] < cstart[:, None] + WIN_COLS)
    col_off = np.clip(ck[None, :] - cq[:, None] + WIN_COLS - 1, 0, 2 * WIN_COLS - 2)
    place = (col_off[None] == np.arange(2 * WIN_COLS - 1)[:, None, None]).astype(np.float32)
    ro = np.clip(np.arange(N_PAIR_BLOCKS + 1) - BIAS_PAD, 0, 2 * WIN_ROWS - 2)
    T = jnp.einsum('hrd,dck->hrck', rpb[:, ro], jnp.asarray(place), precision=lax.Precision.HIGHEST)
    T = jnp.where(jnp.asarray(valid)[None, None], T * LOG2E, NEG_INF)
    return jnp.concatenate([T[:, :-1], T[:, 1:]], axis=-1)


def _head_sum_tables():
    lane = np.arange(ATTN_WIDTH)
    bd = (lane[:, None] // HEAD_DIM == np.arange(LANES)[None, :]).astype(np.float32)
    return jnp.asarray(bd, BF16), jnp.asarray(bd.T, BF16)


def kernel(x, c, ctx, c_ctx, w_ada, b_ada, g_norm1, g_norm2, w_in, b_in, g_q, g_k, rpb, w_attn_o, conv_dw_w,
           conv_dw_b, conv_ln_g, conv_ln_b, w_conv_o, sc_w, w_sc_o, w_o, w_router, w_e1, w_e3, w_e2):
    B, S, D = x.shape
    C = ctx.shape[1]
    L = w_ada.shape[0]
    aw3 = 3 * ATTN_WIDTH
    prw = 2 * CONV_WIDTH + 3 * SC_WIDTH + 3 * D

    cc = jnp.zeros((8, D), F32).at[0:B].set(c).at[B].set(c_ctx)
    mod = _ada(cc, w_ada, b_ada)
    rope_tabs = _rope_tables(S)
    bd, ex = _head_sum_tables()

    xl = x.reshape(B * S, D)
    xc = ctx.reshape(B * C, D)
    b_in3 = b_in.reshape(L, 1, -1)
    w_qkv = w_in[:, :, :aw3].astype(BF16)
    wa, wc, ws, wo = (w.astype(BF16) for w in (w_attn_o, w_conv_o, w_sc_o, w_o))
    we = (w_e1.astype(BF16), w_e3.astype(BF16), w_e2.astype(BF16))

    for l in range(L):
        last = l == L - 1
        ml = lambda j: mod[l, 0:B, j * D:(j + 1) * D].reshape(B, 1, D)
        mc = lambda j: mod[l, B:B + 1, j * D:(j + 1) * D].reshape(1, 1, D)
        g1 = g_norm1[l].reshape(1, D)
        g2 = g_norm2[l].reshape(1, D)
        b_qkv = b_in[l][:aw3].reshape(1, aw3)
        gq = jnp.tile(g_q[l], N_HEADS).reshape(1, ATTN_WIDTH)
        gk = jnp.tile(g_k[l], N_HEADS).reshape(1, ATTN_WIDTH)
        branch_w = (wa, wc, ws, l,
                    jnp.zeros((32, CONV_WIDTH), F32).at[:CONV_K].set(conv_dw_w[l]),
                    conv_dw_b[l].reshape(1, -1), conv_ln_g[l].reshape(1, -1), conv_ln_b[l].reshape(1, -1),
                    jnp.zeros((8, SC_WIDTH), F32).at[:SC_K].set(sc_w[l]))
        wr = jnp.zeros((D, LANES), F32).at[:, :N_EXPERTS].set(w_router[l]).astype(BF16)

        hc = _normmod(xc, g1, mc(0), mc(1), C)
        qc, kc, vc = _qkv(hc, w_qkv, l, b_qkv, gq, gk, bd, ex, None, C)
        if not last:
            attn_c = _ctx_attn(qc, kc, vc, B, C)
            pr_c = _proj(hc, w_in, b_in3, l, aw3, prw)
            merged_c = _branch(attn_c, pr_c, *branch_w, C, D)
            xh_c, aff_c = _outproj(merged_c, xc, wo, l, mc(2), g2, mc(3), mc(4), wr, C)
            xc = _moe(xh_c, aff_c, mc(5), *we, l, B, C, B)

        h = _normmod(xl, g1, ml(0), ml(1), S)
        qn, kn, v, qr, kr = _qkv(h, w_qkv, l, b_qkv, gq, gk, bd, ex, rope_tabs, S)
        attn = _attn(qn, qr, kr, v, kc, vc, _bias_pair_blocks(rpb[l]), B, S, C)
        pr = _proj(h, w_in, b_in3, l, aw3, prw)
        merged = _branch(attn, pr, *branch_w, S, D)
        xh, aff = _outproj(merged, xl, wo, l, ml(2), g2, ml(3), ml(4), wr, S)
        xl = _moe(xh, aff, ml(5), *we, l, B, S, 1)

    return xl[:, :D].reshape(B, S, D)
```

```python
import functools

import numpy as np
import jax
import jax.numpy as jnp
from jax import lax
from jax.experimental import pallas as pl
from jax.experimental.pallas import tpu as pltpu

F32 = jnp.float32
BF16 = jnp.bfloat16

GRID_W = 64
N_HEADS = 16
HEAD_DIM = 64
ATTN_WIDTH = N_HEADS * HEAD_DIM
N_HEAD_PAIRS = N_HEADS // 2
WIN_ROWS = 8
WIN_COLS = 16
ROPE_BASE = 10000.0
CONV_WIDTH = 512
CONV_K = 31
SC_WIDTH = 512
SC_K = 3
N_EXPERTS = 16
CAP_FACTOR = 2
EPS = 1e-6
NEG_INF = -1e30

LANES = 128
SUBLANES = 8
LOG2E = 1.4426950408889634
Q_SCALE = HEAD_DIM ** -0.5 * LOG2E
HALO = 16
ROW_GROUP = 4
KEY_ROWS = ROW_GROUP + WIN_ROWS
BIAS_PAD = WIN_ROWS
N_PAIR_BLOCKS = WIN_ROWS - 1 + BIAS_PAD + KEY_ROWS - 1
VMEM_LIMIT = 56 * 1024 * 1024


def _cparams(*sem):
    return pltpu.CompilerParams(dimension_semantics=sem, vmem_limit_bytes=VMEM_LIMIT)


def _dot(a, b):
    return jnp.dot(a, b, preferred_element_type=F32)


def _dot_t(a, b):
    return lax.dot_general(a, b, (((1,), (1,)), ((), ())), preferred_element_type=F32)


def _sigmoid(x):
    return 0.5 * jnp.tanh(0.5 * x) + 0.5


def _silu(x):
    return x * _sigmoid(x)


def _ada_kernel(cc_ref, w_ref, b_ref, o_ref):
    s = _silu(cc_ref[...]).astype(BF16)
    o_ref[0] = _dot(s, w_ref[0].astype(BF16)) + b_ref[0]


def _ada(cc, w_ada, b_ada):
    L, D, W = w_ada.shape
    tn = 1024
    return pl.pallas_call(
        _ada_kernel,
        grid=(L, W // tn),
        in_specs=[pl.BlockSpec((8, D), lambda l, j: (0, 0)),
                  pl.BlockSpec((1, D, tn), lambda l, j: (l, 0, j)),
                  pl.BlockSpec((1, 1, tn), lambda l, j: (l, 0, j))],
        out_specs=pl.BlockSpec((1, 8, tn), lambda l, j: (l, 0, j)),
        out_shape=jax.ShapeDtypeStruct((L, 8, W), F32),
        compiler_params=_cparams("arbitrary", "arbitrary"),
        name="ada",
    )(cc, w_ada, b_ada.reshape(L, 1, W))


def _mod_spec(nmod, tiles_per_mod, D):
    if nmod == 1:
        return pl.BlockSpec((1, 1, D), lambda i: (0, 0, 0))
    return pl.BlockSpec((1, 1, D), lambda i: (i // tiles_per_mod, 0, 0))


def _normmod_kernel(x_ref, g_ref, sh_ref, sc_ref, h_ref):
    x = x_ref[...]
    ms = jnp.mean(x * x, axis=-1, keepdims=True)
    y = x * lax.rsqrt(ms + EPS) * g_ref[...]
    h_ref[...] = (y * (1.0 + sc_ref[0]) + sh_ref[0]).astype(BF16)


def _normmod(x, g, shift, scale, seq_len):
    rows, D = x.shape[0], g.shape[1]
    tm = min(1024, seq_len)
    mspec = _mod_spec(shift.shape[0], seq_len // tm, D)
    xspec = pl.BlockSpec((tm, D), lambda i: (i, 0))
    return pl.pallas_call(
        _normmod_kernel, grid=(rows // tm,),
        in_specs=[xspec, pl.BlockSpec((1, D), lambda i: (0, 0)), mspec, mspec],
        out_specs=xspec, out_shape=jax.ShapeDtypeStruct((rows, D), BF16),
        compiler_params=_cparams("arbitrary"), name="normmod",
    )(x, g, shift, scale)


def _qkv_kernel(*refs, rope):
    if rope:
        (h_ref, w_ref, b_ref, gq_ref, gk_ref, bd_ref, ex_ref, cos_ref, sa_ref, sb_ref,
         qn_ref, kn_ref, v_ref, qr_ref, kr_ref) = refs
    else:
        h_ref, w_ref, b_ref, gq_ref, gk_ref, bd_ref, ex_ref, qn_ref, kn_ref, v_ref = refs
    h = h_ref[...]
    aw = ATTN_WIDTH

    def proj(s):
        return _dot(h, w_ref[0, :, s * aw:(s + 1) * aw]) + b_ref[:, s * aw:(s + 1) * aw]

    def headnorm(a, g_ref):
        ss = _dot((a * a).astype(BF16), bd_ref[...])
        inv = lax.rsqrt(ss * (1.0 / HEAD_DIM) + EPS)
        inv_hi = inv.astype(BF16)
        inv_lo = (inv - inv_hi.astype(F32)).astype(BF16)
        full = _dot(inv_hi, ex_ref[...]) + _dot(inv_lo, ex_ref[...])
        return a * full * g_ref[...]

    def store_plain(xn, out_ref):
        for c in range(N_HEAD_PAIRS):
            out_ref[c] = xn[:, c * LANES:(c + 1) * LANES].astype(BF16)

    def store_rope(xn, out_ref):
        cos, sa, sb = cos_ref[...], sa_ref[...], sb_ref[...]
        for c in range(N_HEAD_PAIRS):
            xc = xn[:, c * LANES:(c + 1) * LANES]
            r = xc * cos + pltpu.roll(xc, LANES - 16, 1) * sa + pltpu.roll(xc, 16, 1) * sb
            out_ref[c] = r.astype(BF16)

    qn = headnorm(proj(0), gq_ref) * Q_SCALE
    store_plain(qn, qn_ref)
    if rope:
        store_rope(qn, qr_ref)
    kn = headnorm(proj(1), gk_ref)
    store_plain(kn, kn_ref)
    if rope:
        store_rope(kn, kr_ref)
    store_plain(proj(2), v_ref)


def _qkv(h, w, l, b, gq, gk, bd, ex, rope_tabs, seq_len):
    rows, D = h.shape
    tm = min(512, seq_len)
    rope = rope_tabs is not None
    const = lambda i: (0, 0)
    in_specs = [pl.BlockSpec((tm, D), lambda i: (i, 0)),
                pl.BlockSpec((1, D, 3 * ATTN_WIDTH), lambda i: (l, 0, 0), pipeline_mode=pl.Buffered(1)),
                pl.BlockSpec((1, 3 * ATTN_WIDTH), const),
                pl.BlockSpec((1, ATTN_WIDTH), const),
                pl.BlockSpec((1, ATTN_WIDTH), const),
                pl.BlockSpec((ATTN_WIDTH, LANES), const),
                pl.BlockSpec((LANES, ATTN_WIDTH), const)]
    ins = [h, w, b, gq, gk, bd, ex]
    n_out = 3
    if rope:
        tiles_per_seq = seq_len // tm
        tspec = pl.BlockSpec((tm, LANES), lambda i: (i % tiles_per_seq, 0))
        in_specs += [tspec, tspec, tspec]
        ins += list(rope_tabs)
        n_out = 5
    ospec = pl.BlockSpec((N_HEAD_PAIRS, tm, LANES), lambda i: (0, i, 0))
    return pl.pallas_call(
        functools.partial(_qkv_kernel, rope=rope),
        grid=(rows // tm,), in_specs=in_specs,
        out_specs=[ospec] * n_out,
        out_shape=[jax.ShapeDtypeStruct((N_HEAD_PAIRS, rows, LANES), BF16)] * n_out,
        compiler_params=_cparams("arbitrary"), name="qkv",
    )(*ins)


def _proj_kernel(h_ref, w_ref, b_ref, o_ref, wb_ref):
    @pl.when(pl.program_id(1) == 0)
    def _():
        wb_ref[...] = w_ref[0].astype(BF16)

    o_ref[...] = (_dot(h_ref[...], wb_ref[...]) + b_ref[0]).astype(o_ref.dtype)


def _proj(h, w, b, l, col0, ncols):
    rows, D = h.shape
    tm = min(2048, rows)
    tn = 512
    j0 = col0 // tn
    return pl.pallas_call(
        _proj_kernel, grid=(ncols // tn, rows // tm),
        in_specs=[pl.BlockSpec((tm, D), lambda j, i: (i, 0)),
                  pl.BlockSpec((1, D, tn), lambda j, i: (l, 0, j + j0)),
                  pl.BlockSpec((1, 1, tn), lambda j, i: (l, 0, j + j0))],
        out_specs=pl.BlockSpec((tm, tn), lambda j, i: (i, j)),
        out_shape=jax.ShapeDtypeStruct((rows, ncols), BF16),
        scratch_shapes=[pltpu.VMEM((D, tn), BF16)],
        compiler_params=_cparams("arbitrary", "arbitrary"), name="proj",
    )(h, w, b)


def _attn_kernel(qn_ref, qr_ref, kr_ref, v_ref, kc_ref, vc_ref, fp_ref, o_ref, *, rows):
    gq = ROW_GROUP * GRID_W
    nk = KEY_ROWS * GRID_W
    lane = lax.broadcasted_iota(jnp.int32, (1, LANES), 1)
    head_mask = (lane < HEAD_DIM, lane >= HEAD_DIM)
    key_row = lax.broadcasted_iota(jnp.int32, (1, nk), 1) // GRID_W
    kc = kc_ref[0]
    vc = [jnp.where(head_mask[hh], vc_ref[0], 1) for hh in range(2)]

    def group(gi, carry):
        r0 = gi * ROW_GROUP
        ws = jnp.clip(r0 - WIN_ROWS // 2, 0, rows - KEY_ROWS)
        tok0 = pl.multiple_of(gi * gq, gq)
        key0 = pl.multiple_of(ws * GRID_W, GRID_W)
        qr = qr_ref[0, pl.ds(tok0, gq), :]
        qn = qn_ref[0, pl.ds(tok0, gq), :]
        kw = kr_ref[0, pl.ds(key0, nk), :]
        vw = v_ref[0, pl.ds(key0, nk), :]
        scores = [(_dot_t(jnp.where(head_mask[hh], qr, 0), kw),
                   _dot_t(jnp.where(head_mask[hh], qn, 0), kc))
                  for hh in range(2)]
        outs = []
        for hh in range(2):
            s_loc, s_ctx = scores[hh]
            slabs = []
            for i in range(ROW_GROUP):
                n0 = ws - r0 - i + (WIN_ROWS - 1) + BIAS_PAD
                bias = jnp.concatenate([fp_ref[hh, n0 + 2 * jp] for jp in range(KEY_ROWS // 2)], axis=1)
                lo = jnp.clip(r0 + i - WIN_ROWS // 2, 0, rows - WIN_ROWS) - ws
                valid = (key_row >= lo) & (key_row < lo + WIN_ROWS)
                slabs.append(jnp.where(valid, s_loc[i * GRID_W:(i + 1) * GRID_W] + bias, NEG_INF))
            s_loc = jnp.concatenate(slabs, axis=0)
            m = jnp.maximum(jnp.max(s_loc, axis=-1, keepdims=True), jnp.max(s_ctx, axis=-1, keepdims=True))
            p_loc = jnp.exp2(s_loc - m).astype(BF16)
            p_ctx = jnp.exp2(s_ctx - m).astype(BF16)
            o = _dot(p_loc, jnp.where(head_mask[hh], vw, 1)) + _dot(p_ctx, vc[hh])
            outs.append(o / pltpu.roll(o, HEAD_DIM, 1))
        o_ref[0, pl.ds(tok0, gq), :] = jnp.where(head_mask[0], outs[0], outs[1]).astype(BF16)
        return carry

    lax.fori_loop(0, rows // ROW_GROUP, group, 0, unroll=2)


def _attn(qn, qr, kr, v, kc, vc, fp, B, S, C):
    rows = S // GRID_W
    tok = pl.BlockSpec((1, S, LANES), lambda b, p: (p, b, 0))
    ctx = pl.BlockSpec((1, C, LANES), lambda b, p: (p, b, 0))
    return pl.pallas_call(
        functools.partial(_attn_kernel, rows=rows),
        grid=(B, N_HEAD_PAIRS),
        in_specs=[tok, tok, tok, tok, ctx, ctx,
                  pl.BlockSpec((2, N_PAIR_BLOCKS, GRID_W, LANES), lambda b, p: (p, 0, 0, 0))],
        out_specs=tok,
        out_shape=jax.ShapeDtypeStruct((N_HEAD_PAIRS, B * S, LANES), BF16),
        compiler_params=_cparams("arbitrary", "arbitrary"), name="attn",
    )(qn, qr, kr, v, kc, vc, fp)


def _ctx_attn_kernel(q_ref, k_ref, v_ref, o_ref):
    lane = lax.broadcasted_iota(jnp.int32, (1, LANES), 1)
    head_mask = (lane < HEAD_DIM, lane >= HEAD_DIM)
    q, k, v = q_ref[0], k_ref[0], v_ref[0]
    outs = []
    for hh in range(2):
        s = _dot_t(jnp.where(head_mask[hh], q, 0), k)
        m = jnp.max(s, axis=-1, keepdims=True)
        p = jnp.exp2(s - m)
        outs.append(_dot(p.astype(BF16), v) / jnp.sum(p, axis=-1, keepdims=True))
    o_ref[0] = jnp.where(head_mask[0], outs[0], outs[1]).astype(BF16)


def _ctx_attn(q, k, v, B, C):
    spec = pl.BlockSpec((1, C, LANES), lambda b, p: (p, b, 0))
    return pl.pallas_call(
        _ctx_attn_kernel, grid=(B, N_HEAD_PAIRS),
        in_specs=[spec, spec, spec], out_specs=spec,
        out_shape=jax.ShapeDtypeStruct((N_HEAD_PAIRS, B * C, LANES), BF16),
        compiler_params=_cparams("arbitrary", "arbitrary"), name="ctx_attn",
    )(q, k, v)


def _branch_kernel(attn_ref, pr_ref, pcp_ref, pcn_ref, wa_ref, wc_ref, ws_ref,
                   dww_ref, dwb_ref, lng_ref, lnb_ref, scw_ref, o_ref, glu_ext, sc_ext, *, tm, seq_len, D):
    i = pl.program_id(0)
    pos0 = (i * tm) % seq_len
    keep_prev = (pos0 != 0).astype(F32)
    keep_next = (pos0 + tm != seq_len).astype(F32)
    cw, sw = CONV_WIDTH, SC_WIDTH

    def glu(blk):
        return blk[:, 0:cw].astype(F32) * _sigmoid(blk[:, cw:2 * cw].astype(F32))

    def gated_x(blk):
        o = 2 * cw + sw
        return blk[:, o:o + sw].astype(F32) * blk[:, o + sw:o + 2 * sw].astype(F32)

    pcw = 2 * cw + 3 * sw
    cur, prv, nxt = pr_ref[:, 0:pcw], pcp_ref[...], pcn_ref[...]
    glu_ext[0:HALO, :] = glu(prv) * keep_prev
    glu_ext[HALO:HALO + tm, :] = glu(cur)
    glu_ext[HALO + tm:2 * HALO + tm, :] = glu(nxt) * keep_next
    sc_ext[0:HALO, :] = gated_x(prv) * keep_prev
    sc_ext[HALO:HALO + tm, :] = gated_x(cur)
    sc_ext[HALO + tm:2 * HALO + tm, :] = gated_x(nxt) * keep_next

    chunk = 32
    window = chunk + 2 * HALO
    conv_rows = []
    for r in range(0, tm, chunk):
        win = glu_ext[r:r + window, :]
        acc = jnp.zeros((chunk, cw), F32)
        for res in range(SUBLANES):
            shifted = win if res == 0 else pltpu.roll(win, window - res, 0)
            for k in range(CONV_K):
                off = HALO - CONV_K // 2 + k
                if off % SUBLANES == res:
                    acc = acc + dww_ref[k:k + 1, :] * shifted[off - res:off - res + chunk, :]
        conv_rows.append(acc)
    u = jnp.concatenate(conv_rows, axis=0) + dwb_ref[...]
    mu = jnp.mean(u, axis=-1, keepdims=True)
    uc = u - mu
    var = jnp.mean(uc * uc, axis=-1, keepdims=True)
    u = _silu(uc * lax.rsqrt(var + EPS) * lng_ref[...] + lnb_ref[...])
    y_b = _dot(u.astype(BF16), wc_ref[0])

    c3 = jnp.zeros((tm, sw), F32)
    for k in range(SC_K):
        base = HALO - SC_K // 2 + k
        c3 = c3 + scw_ref[k:k + 1, :] * sc_ext[base:base + tm, :]
    sc_b = cur[:, 2 * cw:2 * cw + sw].astype(F32)
    y_c = _dot((sc_b * c3).astype(BF16), ws_ref[0])

    attn = jnp.concatenate([attn_ref[c] for c in range(N_HEAD_PAIRS)], axis=1)
    y_a = _dot(attn, wa_ref[0])

    g = pr_ref[:, pcw:pcw + 3 * D]
    merged = (_sigmoid(g[:, 0:D].astype(F32)) * y_a
              + _sigmoid(g[:, D:2 * D].astype(F32)) * y_b
              + _sigmoid(g[:, 2 * D:3 * D].astype(F32)) * y_c)
    o_ref[...] = merged.astype(BF16)


def _branch(attn, pr, wa, wc, ws, l, dww, dwb, lng, lnb, scw, seq_len, D):
    rows, prw = pr.shape
    tm = 256
    pcw = 2 * CONV_WIDTH + 3 * SC_WIDTH
    nhalo = rows // HALO
    const = lambda i: (0, 0)
    layer = lambda i: (l, 0, 0)
    return pl.pallas_call(
        functools.partial(_branch_kernel, tm=tm, seq_len=seq_len, D=D),
        grid=(rows // tm,),
        in_specs=[pl.BlockSpec((N_HEAD_PAIRS, tm, LANES), lambda i: (0, i, 0)),
                  pl.BlockSpec((tm, prw), lambda i: (i, 0)),
                  pl.BlockSpec((HALO, pcw), lambda i: (jnp.maximum(i * (tm // HALO) - 1, 0), 0)),
                  pl.BlockSpec((HALO, pcw), lambda i: (jnp.minimum((i + 1) * (tm // HALO), nhalo - 1), 0)),
                  pl.BlockSpec((1, ATTN_WIDTH, D), layer),
                  pl.BlockSpec((1, CONV_WIDTH, D), layer),
                  pl.BlockSpec((1, SC_WIDTH, D), layer),
                  pl.BlockSpec((32, CONV_WIDTH), const),
                  pl.BlockSpec((1, CONV_WIDTH), const),
                  pl.BlockSpec((1, CONV_WIDTH), const),
                  pl.BlockSpec((1, CONV_WIDTH), const),
                  pl.BlockSpec((8, SC_WIDTH), const)],
        out_specs=pl.BlockSpec((tm, D), lambda i: (i, 0)),
        out_shape=jax.ShapeDtypeStruct((rows, D), BF16),
        scratch_shapes=[pltpu.VMEM((tm + 2 * HALO, CONV_WIDTH), F32),
                        pltpu.VMEM((tm + 2 * HALO, SC_WIDTH), F32)],
        compiler_params=_cparams("arbitrary"), name="branch",
    )(attn, pr, pr, pr, wa, wc, ws, dww, dwb, lng, lnb, scw)


def _outproj_kernel(m_ref, x_ref, wo_ref, gate_ref, g2_ref, sh_ref, sc_ref, wr_ref, xh_ref, aff_ref):
    D = x_ref.shape[1]
    x = x_ref[...] + gate_ref[0] * _dot(m_ref[...], wo_ref[0])
    xh_ref[:, 0:D] = x
    ms = jnp.mean(x * x, axis=-1, keepdims=True)
    y = x * lax.rsqrt(ms + EPS) * g2_ref[...]
    h2 = (y * (1.0 + sc_ref[0]) + sh_ref[0]).astype(BF16)
    xh_ref[:, D:] = h2.astype(F32)
    logits = _dot(h2, wr_ref[...])
    lane = lax.broadcasted_iota(jnp.int32, logits.shape, 1)
    logits = jnp.where(lane < N_EXPERTS, logits, NEG_INF)
    e = jnp.exp(logits - jnp.max(logits, axis=-1, keepdims=True))
    aff_ref[...] = e / jnp.sum(e, axis=-1, keepdims=True)


def _outproj(merged, x, wo, l, gate, g2, shift, scale, wr, seq_len):
    rows, D = merged.shape
    tm = 256
    mspec = _mod_spec(gate.shape[0], seq_len // tm, D)
    row = lambda i: (i, 0)
    const = lambda i: (0, 0)
    return pl.pallas_call(
        _outproj_kernel, grid=(rows // tm,),
        in_specs=[pl.BlockSpec((tm, D), row), pl.BlockSpec((tm, D), row),
                  pl.BlockSpec((1, D, D), lambda i: (l, 0, 0)), mspec,
                  pl.BlockSpec((1, D), const), mspec, mspec,
                  pl.BlockSpec((D, LANES), const)],
        out_specs=[pl.BlockSpec((tm, 2 * D), row), pl.BlockSpec((tm, LANES), row)],
        out_shape=[jax.ShapeDtypeStruct((rows, 2 * D), F32),
                   jax.ShapeDtypeStruct((rows, LANES), F32)],
        compiler_params=_cparams("arbitrary"), name="outproj",
    )(merged, x, wo, gate, g2, shift, scale, wr)


MAX_BISECT = 160
BISECT_PER_CHECK = 4
POS_LANE = 3 * N_EXPERTS
TOK_SPLIT = 64


def _select_kernel(aff_ref, tri_ref, tv_ref, idx_ref, g_ref, s1_ref, r_ref, *, n, cap, blk, tchunk):
    E = N_EXPERTS
    aff = aff_ref[0]
    a = aff.T[0:E, :]
    capf = float(cap)

    def cond(st):
        it, _, _, ndone = st
        return jnp.logical_and(it < MAX_BISECT, ndone < float(E))

    def in_range(lo, hi):
        return jnp.logical_and(a >= lo, a < hi)

    def body(st):
        it, lo, hi, _ = st
        for _ in range(BISECT_PER_CHECK):
            mid = 0.5 * (lo + hi)
            cnt = jnp.sum((a >= mid).astype(F32), axis=-1, keepdims=True)
            ge = cnt >= capf
            lo = jnp.where(ge, mid, lo)
            hi = jnp.where(ge, hi, mid)
        r = in_range(lo, hi)
        vmin = jnp.min(jnp.where(r, a, 4.0), axis=-1, keepdims=True)
        vmax = jnp.max(jnp.where(r, a, -1.0), axis=-1, keepdims=True)
        ndone = jnp.sum((vmin == vmax).astype(F32))
        return it + BISECT_PER_CHECK, lo, hi, ndone

    _, lo, hi, _ = lax.while_loop(cond, body, (jnp.int32(0), jnp.zeros((E, 1), F32), jnp.full((E, 1), 2.0, F32),
                                               jnp.float32(0.0)))
    thr = jnp.max(jnp.where(in_range(lo, hi), a, -1.0), axis=-1, keepdims=True)

    def cumsum(x):
        outs, carry = [], jnp.zeros((E, 1), F32)
        for c in range(n // blk):
            part = _dot(x[:, c * blk:(c + 1) * blk].astype(BF16), tri_ref[...]) + carry
            outs.append(part)
            carry = part[:, blk - 1:blk]
        return jnp.concatenate(outs, axis=1)

    gt = a > thr
    eq = (a == thr).astype(F32)
    need = capf - jnp.sum(gt.astype(F32), axis=-1, keepdims=True)
    eq_rank = cumsum(eq) - eq
    sel = jnp.logical_or(gt, jnp.logical_and(eq > 0.0, eq_rank < need)).astype(F32)
    s1_ref[...] = sel * cumsum(sel)

    a_hi = aff.astype(BF16).astype(F32)
    rest = aff - a_hi
    a_mid = rest.astype(BF16).astype(F32)
    a_lo = rest - a_mid
    r_ref[...] = (a_hi + pltpu.roll(a_mid, E, 1) + pltpu.roll(a_lo, 2 * E, 1) + tv_ref[...]).astype(BF16)

    slot_ids = lax.broadcasted_iota(jnp.int32, (cap, 1), 0).astype(F32) + 1.0
    lane = lax.broadcasted_iota(jnp.int32, (1, LANES), 1)

    def per_expert(e, carry):
        row = s1_ref[pl.ds(e, 1), :]
        acc = jnp.zeros((cap, LANES), F32)
        for c in range(n // tchunk):
            onehot = (row[:, c * tchunk:(c + 1) * tchunk] == slot_ids).astype(BF16)
            acc = acc + _dot(onehot, r_ref[c * tchunk:(c + 1) * tchunk, :])
        mine = jnp.logical_and(lane % E == e, lane < POS_LANE)
        g = jnp.sum(jnp.where(mine, acc, 0.0), axis=1, keepdims=True)
        tok = jnp.sum(jnp.where(lane == POS_LANE, acc * float(TOK_SPLIT), jnp.where(lane == POS_LANE + 1, acc, 0.0)),
                      axis=1, keepdims=True)
        g_ref[0, e] = jnp.broadcast_to(g, (cap, LANES))
        idx_ref[0, e] = jnp.broadcast_to(tok, (cap, LANES)).astype(jnp.int32)
        return carry

    lax.fori_loop(0, E, per_expert, 0)


def _select(aff, B, n, cap):
    blk = min(512, n)
    tchunk = min(1024, n)
    tri = (np.arange(blk)[:, None] <= np.arange(blk)[None, :]).astype(np.float32)
    tv = np.zeros((n, LANES), np.float32)
    tv[:, POS_LANE] = np.arange(n) // TOK_SPLIT
    tv[:, POS_LANE + 1] = np.arange(n) % TOK_SPLIT
    out = pl.BlockSpec((1, N_EXPERTS, cap, LANES), lambda b: (b, 0, 0, 0))
    return pl.pallas_call(
        functools.partial(_select_kernel, n=n, cap=cap, blk=blk, tchunk=tchunk),
        grid=(B,),
        in_specs=[pl.BlockSpec((1, n, LANES), lambda b: (b, 0, 0)),
                  pl.BlockSpec((blk, blk), lambda b: (0, 0)),
                  pl.BlockSpec((n, LANES), lambda b: (0, 0))],
        out_specs=[out, out],
        out_shape=[jax.ShapeDtypeStruct((B, N_EXPERTS, cap, LANES), jnp.int32),
                   jax.ShapeDtypeStruct((B, N_EXPERTS, cap, LANES), F32)],
        scratch_shapes=[pltpu.VMEM((N_EXPERTS, n), F32), pltpu.VMEM((n, LANES), BF16)],
        compiler_params=_cparams("arbitrary"), name="select",
    )(aff.reshape(B, n, LANES), jnp.asarray(tri, BF16), jnp.asarray(tv))


MIN_CHUNK_ROWS = 256
LATENT_BATCHES_PER_STEP = 2


def _expert_kernel(idx_ref, g_ref, gate_ref, w1_ref, w3_ref, w2_ref, xh_in, xh_hbm,
                   buf, sem, *, bb, cap, n, D, gate_rows, n_chunks):
    del xh_in
    e = pl.program_id(0)
    b0 = pl.program_id(1) * bb
    rows = bb * cap

    def token_row(j):
        b = b0 + j // cap
        return b * n + idx_ref[(b * N_EXPERTS + e) * cap + j % cap]

    def loop_rows(fn, lo, hi):
        for bl in range(bb):
            seg_lo, seg_hi = max(lo, bl * cap), min(hi, (bl + 1) * cap)
            if seg_lo >= seg_hi:
                continue
            b = b0 + bl
            base = (b * N_EXPERTS + e) * cap - bl * cap

            def body(t, carry, b=b, base=base):
                for u in range(SUBLANES):
                    fn(t, u, b * n + idx_ref[base + t * SUBLANES + u])
                return carry

            lax.fori_loop(seg_lo // SUBLANES, seg_hi // SUBLANES, body, 0)

    def inline_rows(fn, lo, hi):
        for j in range(lo, hi):
            fn(j // SUBLANES, j % SUBLANES, token_row(j))

    def row_load(t, u, r):
        return pltpu.make_async_copy(xh_hbm.at[pl.ds(r, 1), :], buf.at[t, pl.ds(u, 1), :], sem.at[0])

    def row_store(t, u, r):
        return pltpu.make_async_copy(buf.at[t, pl.ds(u, 1), pl.ds(0, D)], xh_hbm.at[pl.ds(r, 1), pl.ds(0, D)],
                                     sem.at[1])

    def start_loads(t, u, r):
        row_load(t, u, r).start()

    def wait_loads(t, u, r):
        row_load(t, u, r).wait()

    def start_store(t, u, r):
        row_store(t, u, r).start()

    def wait_store(t, u, r):
        row_store(t, u, r).wait()

    g = g_ref[...].reshape(rows, LANES)

    def compute(lo, hi):
        ts = slice(lo // SUBLANES, hi // SUBLANES)
        x = buf[ts, :, D:].reshape(hi - lo, D).astype(BF16)
        hidden = (_silu(_dot(x, w1_ref[0, 0])) * _dot(x, w3_ref[0, 0])).astype(BF16)
        y = _dot(hidden, w2_ref[0, 0]) * jnp.concatenate([g[lo:hi]] * (D // LANES), axis=1)
        for bl in range(bb):
            seg_lo, seg_hi = max(lo, bl * cap), min(hi, (bl + 1) * cap)
            if seg_lo >= seg_hi:
                continue
            gate_row = gate_ref[bl if gate_rows > 1 else 0]
            seg = slice(seg_lo // SUBLANES, seg_hi // SUBLANES)
            upd = y[seg_lo - lo:seg_hi - lo] * gate_row
            buf[seg, :, 0:D] += upd.reshape((seg_hi - seg_lo) // SUBLANES, SUBLANES, D)

    bounds = [(k * rows // n_chunks, (k + 1) * rows // n_chunks) for k in range(n_chunks)]
    loop_rows(start_loads, *bounds[0])
    loop_rows(wait_loads, *bounds[0])
    for k in range(n_chunks):
        if k + 1 < n_chunks:
            inline_rows(start_loads, *bounds[k + 1])
        if k >= 1:
            inline_rows(start_store, *bounds[k - 1])
        compute(*bounds[k])
        if k + 1 < n_chunks:
            loop_rows(wait_loads, *bounds[k + 1])
    loop_rows(start_store, *bounds[-1])
    loop_rows(wait_store, 0, rows)


def _experts(idx, g, gate, w1, w3, w2, l, xh, B, n, cap, bb):
    _, E, D, FF = w1.shape
    rows = bb * cap
    gate_rows = bb if gate.shape[0] > 1 else 1
    gate_spec = (pl.BlockSpec((bb, 1, D), lambda e, b, idx: (b, 0, 0)) if gate.shape[0] > 1
                 else pl.BlockSpec((1, 1, D), lambda e, b, idx: (0, 0, 0)))
    grid_spec = pltpu.PrefetchScalarGridSpec(
        num_scalar_prefetch=1,
        grid=(E, B // bb),
        in_specs=[pl.BlockSpec((bb, 1, cap, LANES), lambda e, b, idx: (b, e, 0, 0)),
                  gate_spec,
                  pl.BlockSpec((1, 1, D, FF), lambda e, b, idx: (l, e, 0, 0)),
                  pl.BlockSpec((1, 1, D, FF), lambda e, b, idx: (l, e, 0, 0)),
                  pl.BlockSpec((1, 1, FF, D), lambda e, b, idx: (l, e, 0, 0)),
                  pl.BlockSpec(memory_space=pl.ANY)],
        out_specs=pl.BlockSpec(memory_space=pl.ANY),
        scratch_shapes=[pltpu.VMEM((rows // SUBLANES, SUBLANES, xh.shape[1]), F32),
                        pltpu.SemaphoreType.DMA((2,))])
    return pl.pallas_call(
        functools.partial(_expert_kernel, bb=bb, cap=cap, n=n, D=D, gate_rows=gate_rows,
                          n_chunks=max(1, rows // MIN_CHUNK_ROWS)),
        grid_spec=grid_spec,
        out_shape=jax.ShapeDtypeStruct(xh.shape, F32),
        input_output_aliases={6: 0},
        compiler_params=_cparams("arbitrary", "arbitrary"), name="experts",
    )(idx, g, gate, w1, w3, w2, xh)


def _moe(xh, aff, gate, w1, w3, w2, l, B, n, bb):
    cap = max(1, CAP_FACTOR * n // N_EXPERTS)
    idx, g = _select(aff, B, n, cap)
    return _experts(idx[..., 0].reshape(-1), g, gate, w1, w3, w2, l, xh, B, n, cap, bb)


def _rope_tables(S):
    half, quarter = HEAD_DIM // 2, HEAD_DIM // 4
    t = jnp.arange(S)
    freqs = 1.0 / (ROPE_BASE ** (jnp.arange(quarter, dtype=F32) / quarter))
    d = np.arange(LANES) % HEAD_DIM
    pos = jnp.where(jnp.asarray(d < half)[None, :], (t // GRID_W)[:, None], (t % GRID_W)[:, None]).astype(F32)
    ang = pos * freqs[jnp.asarray(d % quarter)][None, :]
    first = jnp.asarray((d % half) < quarter)[None, :]
    sin = jnp.sin(ang)
    return jnp.cos(ang), jnp.where(first, -sin, 0.0), jnp.where(first, 0.0, sin)


def _bias_pair_blocks(rpb):
    cq = np.arange(GRID_W)
    ck = np.arange(GRID_W)
    cstart = np.clip(cq - WIN_COLS // 2, 0, GRID_W - WIN_COLS)
    valid = (ck[None, :] >= cstart[:, None]) & (ck[None, :] < cstart[:, None] + WIN_COLS)
    col_off = np.clip(ck[None, :] - cq[:, None] + WIN_COLS - 1, 0, 2 * WIN_COLS - 2)
    place = (col_off[None] == np.arange(2 * WIN_COLS - 1)[:, None, None]).astype(np.float32)
    ro = np.clip(np.arange(N_PAIR_BLOCKS + 1) - BIAS_PAD, 0, 2 * WIN_ROWS - 2)
    T = jnp.einsum('hrd,dck->hrck', rpb[:, ro], jnp.asarray(place), precision=lax.Precision.HIGHEST)
    T = jnp.where(jnp.asarray(valid)[None, None], T * LOG2E, NEG_INF)
    return jnp.concatenate([T[:, :-1], T[:, 1:]], axis=-1)


def _head_sum_tables():
    lane = np.arange(ATTN_WIDTH)
    bd = (lane[:, None] // HEAD_DIM == np.arange(LANES)[None, :]).astype(np.float32)
    return jnp.asarray(bd, BF16), jnp.asarray(bd.T, BF16)


def kernel(x, c, ctx, c_ctx, w_ada, b_ada, g_norm1, g_norm2, w_in, b_in, g_q, g_k, rpb, w_attn_o, conv_dw_w,
           conv_dw_b, conv_ln_g, conv_ln_b, w_conv_o, sc_w, w_sc_o, w_o, w_router, w_e1, w_e3, w_e2):
    B, S, D = x.shape
    C = ctx.shape[1]
    L = w_ada.shape[0]
    aw3 = 3 * ATTN_WIDTH
    prw = 2 * CONV_WIDTH + 3 * SC_WIDTH + 3 * D

    cc = jnp.zeros((8, D), F32).at[0:B].set(c).at[B].set(c_ctx)
    mod = _ada(cc, w_ada, b_ada)
    rope_tabs = _rope_tables(S)
    bd, ex = _head_sum_tables()

    xl = x.reshape(B * S, D)
    xc = ctx.reshape(B * C, D)
    b_in3 = b_in.reshape(L, 1, -1)
    w_qkv = w_in[:, :, :aw3].astype(BF16)
    wa, wc, ws, wo = (w.astype(BF16) for w in (w_attn_o, w_conv_o, w_sc_o, w_o))
    we = (w_e1.astype(BF16), w_e3.astype(BF16), w_e2.astype(BF16))

    for l in range(L):
        last = l == L - 1
        ml = lambda j: mod[l, 0:B, j * D:(j + 1) * D].reshape(B, 1, D)
        mc = lambda j: mod[l, B:B + 1, j * D:(j + 1) * D].reshape(1, 1, D)
        g1 = g_norm1[l].reshape(1, D)
        g2 = g_norm2[l].reshape(1, D)
        b_qkv = b_in[l][:aw3].reshape(1, aw3)
        gq = jnp.tile(g_q[l], N_HEADS).reshape(1, ATTN_WIDTH)
        gk = jnp.tile(g_k[l], N_HEADS).reshape(1, ATTN_WIDTH)
        branch_w = (wa, wc, ws, l,
                    jnp.zeros((32, CONV_WIDTH), F32).at[:CONV_K].set(conv_dw_w[l]),
                    conv_dw_b[l].reshape(1, -1), conv_ln_g[l].reshape(1, -1), conv_ln_b[l].reshape(1, -1),
                    jnp.zeros((8, SC_WIDTH), F32).at[:SC_K].set(sc_w[l]))
        wr = jnp.zeros((D, LANES), F32).at[:, :N_EXPERTS].set(w_router[l]).astype(BF16)

        hc = _normmod(xc, g1, mc(0), mc(1), C)
        qc, kc, vc = _qkv(hc, w_qkv, l, b_qkv, gq, gk, bd, ex, None, C)
        if not last:
            attn_c = _ctx_attn(qc, kc, vc, B, C)
            pr_c = _proj(hc, w_in, b_in3, l, aw3, prw)
            merged_c = _branch(attn_c, pr_c, *branch_w, C, D)
            xh_c, aff_c = _outproj(merged_c, xc, wo, l, mc(2), g2, mc(3), mc(4), wr, C)
            xc = _moe(xh_c, aff_c, mc(5), *we, l, B, C, B)

        h = _normmod(xl, g1, ml(0), ml(1), S)
        qn, kn, v, qr, kr = _qkv(h, w_qkv, l, b_qkv, gq, gk, bd, ex, rope_tabs, S)
        attn = _attn(qn, qr, kr, v, kc, vc, _bias_pair_blocks(rpb[l]), B, S, C)
        pr = _proj(h, w_in, b_in3, l, aw3, prw)
        merged = _branch(attn, pr, *branch_w, S, D)
        xh, aff = _outproj(merged, xl, wo, l, ml(2), g2, ml(3), ml(4), wr, S)
        xl = _moe(xh, aff, ml(5), *we, l, B, S, LATENT_BATCHES_PER_STEP)

    return xl[:, :D].reshape(B, S, D)
```
